```python
import math
import jax
import jax.numpy as jnp
from jax import lax
import numpy as np

D_MODEL = 2048
BATCH = 8
SEQ = 2048
DEPTH = 2
DEC_BATCH = 32
DEC_SEQ = 8
PAST_LEN = 8192
PAGE_SIZE = 128

EPS = 1e-6
NEG = -1e30
N_EVEN = (DEPTH + 1) // 2
N_ODD = DEPTH // 2
D_CONV = D_MODEL // 2
CONV_W = 31
HEAD_DIM = 128
DIL_GROUPS = ((128, 1), (512, 4), (2048, 16))
N_DGROUPS = len(DIL_GROUPS)
HEADS_PER_GROUP = 4
D_ATT = N_DGROUPS * HEADS_PER_GROUP * HEAD_DIM
D_ATT_OUT = HEADS_PER_GROUP * HEAD_DIM
CHUNK = 128
D_GATE = D_MODEL
N_SG = 8
D_SG = D_GATE // N_SG
N_MEM = 256
MEM_HEADS = 4
MEM_HEAD_DIM = 128
D_MEMATT = MEM_HEADS * MEM_HEAD_DIM
D_FF = 4 * D_MODEL

kernel_name = 'hybrid_conv_dilated_gmlp_decoder_step'


def rmsnorm(x, g):
    xf = x.astype(jnp.float32)
    y = xf * lax.rsqrt(jnp.mean(xf * xf, axis=-1, keepdims=True) + EPS)
    return (y * g.astype(jnp.float32)).astype(x.dtype)


def layernorm(x, g, b):
    xf = x.astype(jnp.float32)
    xc = xf - jnp.mean(xf, axis=-1, keepdims=True)
    y = xc * lax.rsqrt(jnp.mean(xc * xc, axis=-1, keepdims=True) + EPS)
    return (y * g.astype(jnp.float32) + b.astype(jnp.float32)).astype(x.dtype)


def causal_dwconv(xpad, w, b):
    y = lax.conv_general_dilated(xpad, w[:, None, :].astype(xpad.dtype), window_strides=(1,), padding='VALID',
                                 dimension_numbers=('NWC', 'WIO', 'NWC'), feature_group_count=xpad.shape[-1])
    return y + b


def dilated_attn_prompt(q, k, v, window, dil):
    B, T, H, Dh = q.shape
    n = window // dil
    S = T // dil
    nb = -(-S // n)
    Sp = nb * n

    def to_sub(a):
        a = a.reshape(B, S, dil, H, Dh).transpose(0, 2, 1, 3, 4)
        return jnp.pad(a, ((0, 0), (0, 0), (0, Sp - S), (0, 0), (0, 0)))

    def band(a):
        a = jnp.pad(to_sub(a), ((0, 0), (0, 0), (n, 0), (0, 0), (0, 0))).reshape(B, dil, nb + 1, n, H, Dh)
        return jnp.concatenate([a[:, :, :-1], a[:, :, 1:]], axis=3)

    qs = to_sub(q).reshape(B, dil, nb, n, H, Dh)
    kb, vb = band(k), band(v)
    s = jnp.einsum('brcqhd,brckhd->brchqk', qs, kb, preferred_element_type=jnp.float32)
    c = jnp.arange(nb)[:, None, None]
    qi = jnp.arange(n)[None, :, None]
    kj = jnp.arange(2 * n)[None, None, :]
    dist = qi + n - kj
    mask = (dist >= 0) & (dist < n) & (c * n - n + kj >= 0)
    s = jnp.where(mask[:, None], s, NEG)
    lse = jax.nn.logsumexp(s, axis=-1)
    p = jnp.exp(s - lse[..., None])
    o = jnp.einsum('brchqk,brckhd->brcqhd', p.astype(v.dtype), vb)
    o = o.reshape(B, dil, Sp, H, Dh)[:, :, :S].transpose(0, 2, 1, 3, 4).reshape(B, T, H, Dh)
    lse = lse.transpose(0, 1, 2, 4, 3).reshape(B, dil, Sp, H)[:, :, :S].transpose(0, 2, 1, 3).reshape(B, T, H)
    return o, lse


def dilated_attn_sample(q, k_cache, v_cache, k_new, v_new, window, dil):
    B, DS, H, Dh = q.shape
    L = k_cache.shape[1]
    n = window // dil
    k_all = jnp.concatenate([k_cache, k_new], axis=1)
    v_all = jnp.concatenate([v_cache, v_new], axis=1)
    idx = L + jnp.arange(DS)[:, None] - dil * jnp.arange(n)[None, :]
    valid = idx >= 0
    idx = jnp.maximum(idx, 0)
    kg, vg = k_all[:, idx], v_all[:, idx]
    s = jnp.einsum('bqhd,bqnhd->bqhn', q, kg, preferred_element_type=jnp.float32)
    s = jnp.where(valid[None, :, None, :], s, NEG)
    lse = jax.nn.logsumexp(s, axis=-1)
    p = jnp.exp(s - lse[..., None])
    o = jnp.einsum('bqhn,bqnhd->bqhd', p.astype(vg.dtype), vg)
    return o, lse


def even_mixer(h, w_in, conv_w, conv_b, ln_g, ln_b, qn_g, kn_g, w_out, conv_state=None, kv_cache=None):
    B, T, _ = h.shape
    z = h @ w_in
    a = z[..., :D_CONV] * jax.nn.sigmoid(z[..., D_CONV:2 * D_CONV])
    if conv_state is None:
        apad = jnp.pad(a, ((0, 0), (CONV_W - 1, 0), (0, 0)))
    else:
        apad = jnp.concatenate([conv_state, a], axis=1)
    new_conv = apad[:, -(CONV_W - 1):]
    a_out = jax.nn.silu(layernorm(causal_dwconv(apad, conv_w, conv_b), ln_g, ln_b))
    qkv = z[..., 2 * D_CONV:].reshape(B, T, 3, N_DGROUPS, HEADS_PER_GROUP, HEAD_DIM)
    q = rmsnorm(qkv[:, :, 0], qn_g[:, None, :]) * (HEAD_DIM ** -0.5)
    k = rmsnorm(qkv[:, :, 1], kn_g[:, None, :])
    v = qkv[:, :, 2]
    outs, lses, new_kv = [], [], []
    for gi, (win, dil) in enumerate(DIL_GROUPS):
        qg, kg, vg = q[:, :, gi], k[:, :, gi], v[:, :, gi]
        if kv_cache is None:
            o, l = dilated_attn_prompt(qg, kg, vg, win, dil)
            keep = min(win, T)
            new_kv += [kg[:, T - keep:], vg[:, T - keep:]]
        else:
            o, l = dilated_attn_sample(qg, kv_cache[2 * gi], kv_cache[2 * gi + 1], kg, vg, win, dil)
            new_kv += [kg, vg]
        outs.append(o)
        lses.append(l)
    wts = jax.nn.softmax(jnp.stack(lses), axis=0)
    b_out = jnp.einsum('gbth,gbthd->bthd', wts, jnp.stack(outs).astype(jnp.float32)).astype(h.dtype)
    y = jnp.concatenate([a_out, b_out.reshape(B, T, D_ATT_OUT)], axis=-1) @ w_out
    return y, new_conv, new_kv


def odd_mixer(h, w_in, b_in, vln_g, vln_b, w_s, b_s, w_out):
    B, T, _ = h.shape
    z = jax.nn.gelu(h @ w_in + b_in, approximate=False)
    u = z[..., :D_GATE]
    v = layernorm(z[..., D_GATE:], vln_g, vln_b)
    C = min(CHUNK, T)
    ws = w_s[:, :C, :C] * jnp.tril(jnp.ones((C, C), w_s.dtype))
    vc = v.reshape(B, T // C, C, N_SG, D_SG)
    sv = jnp.einsum('gts,bcsgk->bctgk', ws, vc) + b_s[:, :C].T[None, None, :, :, None]
    y = (u * sv.reshape(B, T, D_GATE)) @ w_out
    return y, v


def mem_kv(mem, g_mem, wk, wv, kn):
    B, N, _ = mem.shape
    m = rmsnorm(mem, g_mem)
    k = rmsnorm((m @ wk).reshape(B, N, MEM_HEADS, MEM_HEAD_DIM), kn)
    v = (m @ wv).reshape(B, N, MEM_HEADS, MEM_HEAD_DIM)
    return k, v


def mem_attn(h, k, v, wq, qn, wo):
    B, T, _ = h.shape
    q = rmsnorm((h @ wq).reshape(B, T, MEM_HEADS, MEM_HEAD_DIM), qn) * (MEM_HEAD_DIM ** -0.5)
    s = jnp.einsum('bthd,bnhd->bhtn', q, k, preferred_element_type=jnp.float32)
    p = jax.nn.softmax(s, axis=-1)
    o = jnp.einsum('bhtn,bnhd->bthd', p.astype(v.dtype), v).reshape(B, T, D_MEMATT)
    return o @ wo


def sq_relu_mlp(h, w1, w2):
    return jnp.square(jax.nn.relu(h @ w1)) @ w2


def setup_inputs(seed: int = 0) -> dict:
    key = jax.random.key(seed)
    ks = list(jax.random.split(key, 48))

    def nrm(shape, scale):
        return scale * jax.random.normal(ks.pop(), shape, jnp.float32)

    def gain(shape):
        return 1.0 + nrm(shape, 0.02)

    d = {}
    d['x_prompt'] = nrm((BATCH, SEQ, D_MODEL), 1.0)
    d['x_sample'] = nrm((DEC_BATCH, DEC_SEQ, D_MODEL), 1.0)
    d['mem_prompt'] = nrm((BATCH, N_MEM, D_MODEL), 1.0)
    d['state_conv'] = nrm((N_EVEN, DEC_BATCH, CONV_W - 1, D_CONV), 0.5)
    for win, _ in DIL_GROUPS:
        L = min(win, PAST_LEN)
        d['cache_k_w%d' % win] = nrm((N_EVEN, DEC_BATCH, L, HEADS_PER_GROUP, HEAD_DIM), 1.0)
        d['cache_v_w%d' % win] = nrm((N_EVEN, DEC_BATCH, L, HEADS_PER_GROUP, HEAD_DIM), 1.0)
    d['cache_mem_k'] = nrm((DEPTH, DEC_BATCH, N_MEM, MEM_HEADS, MEM_HEAD_DIM), 1.0)
    d['cache_mem_v'] = nrm((DEPTH, DEC_BATCH, N_MEM, MEM_HEADS, MEM_HEAD_DIM), 1.0)
    d['g_mix'] = gain((DEPTH, D_MODEL))
    d['w_in_e'] = nrm((N_EVEN, D_MODEL, 2 * D_CONV + 3 * D_ATT), D_MODEL ** -0.5)
    d['conv_w'] = nrm((N_EVEN, CONV_W, D_CONV), CONV_W ** -0.5)
    d['conv_b'] = nrm((N_EVEN, D_CONV), 0.02)
    d['conv_ln_g'] = gain((N_EVEN, D_CONV))
    d['conv_ln_b'] = nrm((N_EVEN, D_CONV), 0.02)
    d['q_norm_e'] = gain((N_EVEN, N_DGROUPS, HEAD_DIM))
    d['k_norm_e'] = gain((N_EVEN, N_DGROUPS, HEAD_DIM))
    d['w_out_e'] = nrm((N_EVEN, D_CONV + D_ATT_OUT, D_MODEL), (D_CONV + D_ATT_OUT) ** -0.5)
    d['w_in_o'] = nrm((N_ODD, D_MODEL, 2 * D_GATE), D_MODEL ** -0.5)
    d['b_in_o'] = nrm((N_ODD, 2 * D_GATE), 0.02)
    d['v_ln_g'] = gain((N_ODD, D_GATE))
    d['v_ln_b'] = nrm((N_ODD, D_GATE), 0.02)
    d['w_s'] = nrm((N_ODD, N_SG, CHUNK, CHUNK), CHUNK ** -0.5)
    d['b_s'] = gain((N_ODD, N_SG, CHUNK))
    d['w_out_o'] = nrm((N_ODD, D_GATE, D_MODEL), D_GATE ** -0.5)
    d['g_xmem'] = gain((DEPTH, D_MODEL))
    d['g_mem'] = gain((DEPTH, D_MODEL))
    d['wq_mem'] = nrm((DEPTH, D_MODEL, D_MEMATT), D_MODEL ** -0.5)
    d['wk_mem'] = nrm((DEPTH, D_MODEL, D_MEMATT), D_MODEL ** -0.5)
    d['wv_mem'] = nrm((DEPTH, D_MODEL, D_MEMATT), D_MODEL ** -0.5)
    d['q_norm_mem'] = gain((DEPTH, MEM_HEAD_DIM))
    d['k_norm_mem'] = gain((DEPTH, MEM_HEAD_DIM))
    d['wo_mem'] = nrm((DEPTH, D_MEMATT, D_MODEL), D_MEMATT ** -0.5)
    d['g_ffn'] = gain((DEPTH, D_MODEL))
    d['w_ffn1'] = nrm((DEPTH, D_MODEL, D_FF), D_MODEL ** -0.5)
    d['w_ffn2'] = nrm((DEPTH, D_FF, D_MODEL), D_FF ** -0.5)
    return d


def reference(x_prompt, x_sample, mem_prompt, state_conv, cache_k_w128, cache_v_w128, cache_k_w512, cache_v_w512,
              cache_k_w2048, cache_v_w2048, cache_mem_k, cache_mem_v, g_mix, w_in_e, conv_w, conv_b, conv_ln_g,
              conv_ln_b, q_norm_e, k_norm_e, w_out_e, w_in_o, b_in_o, v_ln_g, v_ln_b, w_s, b_s, w_out_o, g_xmem,
              g_mem, wq_mem, wk_mem, wv_mem, q_norm_mem, k_norm_mem, wo_mem, g_ffn, w_ffn1, w_ffn2):
    kv_in = (cache_k_w128, cache_v_w128, cache_k_w512, cache_v_w512, cache_k_w2048, cache_v_w2048)
    xp, xs = x_prompt, x_sample
    conv_pl, conv_sl, kv_pl, kv_sl, memk_pl, memv_pl, chunk_sl = [], [], [], [], [], [], []
    for i in range(DEPTH):
        j = i // 2
        hp = rmsnorm(xp, g_mix[i])
        hs = rmsnorm(xs, g_mix[i])
        if i % 2 == 0:
            ew = (w_in_e[j], conv_w[j], conv_b[j], conv_ln_g[j], conv_ln_b[j], q_norm_e[j], k_norm_e[j], w_out_e[j])
            yp, cp, kvp = even_mixer(hp, *ew)
            ys, cs, kvs = even_mixer(hs, *ew, conv_state=state_conv[j], kv_cache=tuple(c[j] for c in kv_in))
            conv_pl.append(cp)
            conv_sl.append(cs)
            kv_pl.append(kvp)
            kv_sl.append(kvs)
        else:
            ow = (w_in_o[j], b_in_o[j], v_ln_g[j], v_ln_b[j], w_s[j], b_s[j], w_out_o[j])
            yp, _ = odd_mixer(hp, *ow)
            ys, vs = odd_mixer(hs, *ow)
            chunk_sl.append(vs)
        xp = xp + yp
        xs = xs + ys
        mk, mv = mem_kv(mem_prompt, g_mem[i], wk_mem[i], wv_mem[i], k_norm_mem[i])
        memk_pl.append(mk)
        memv_pl.append(mv)
        xp = xp + mem_attn(rmsnorm(xp, g_xmem[i]), mk, mv, wq_mem[i], q_norm_mem[i], wo_mem[i])
        xs = xs + mem_attn(rmsnorm(xs, g_xmem[i]), cache_mem_k[i], cache_mem_v[i], wq_mem[i], q_norm_mem[i], wo_mem[i])
        xp = xp + sq_relu_mlp(rmsnorm(xp, g_ffn[i]), w_ffn1[i], w_ffn2[i])
        xs = xs + sq_relu_mlp(rmsnorm(xs, g_ffn[i]), w_ffn1[i], w_ffn2[i])
    conv_p = jnp.stack(conv_pl)
    conv_s = jnp.stack(conv_sl)
    k128_p, v128_p, k512_p, v512_p, k2048_p, v2048_p = [jnp.stack([kv[n] for kv in kv_pl]) for n in range(2 * N_DGROUPS)]
    k128_s, v128_s, k512_s, v512_s, k2048_s, v2048_s = [jnp.stack([kv[n] for kv in kv_sl]) for n in range(2 * N_DGROUPS)]
    memk_p = jnp.stack(memk_pl)
    memv_p = jnp.stack(memv_pl)
    chunkv_s = jnp.stack(chunk_sl)
    return (xp, xs, conv_p, conv_s, k128_p, v128_p, k512_p, v512_p, k2048_p, v2048_p,
            k128_s, v128_s, k512_s, v512_s, k2048_s, v2048_s, memk_p, memv_p, chunkv_s)
```

```python
import functools
import math

import numpy as np
import jax
import jax.numpy as jnp
from jax import lax
from jax.experimental import pallas as pl
from jax.experimental.pallas import tpu as pltpu

D_MODEL = 2048
EPS = 1e-6
NEG = -1e30
D_CONV = D_MODEL // 2
CONV_W = 31
HEAD_DIM = 128
DIL_GROUPS = ((128, 1), (512, 4), (2048, 16))
N_DGROUPS = len(DIL_GROUPS)
HEADS_PER_GROUP = 4
D_GRP = HEADS_PER_GROUP * HEAD_DIM
D_ATT = N_DGROUPS * D_GRP
CHUNK = 128
D_GATE = D_MODEL
N_SG = 8
D_SG = D_GATE // N_SG
N_MEM = 256
MEM_HEADS = 4
D_MEMATT = MEM_HEADS * HEAD_DIM
D_FF = 4 * D_MODEL

F32 = jnp.float32
BF16 = jnp.bfloat16
MIB = 1024 * 1024
NORM_ROWS = 256
CONV_ROWS = 32
CONV_HALO = 32
SUBLANES = 8


def _params(sem, vmem_mib):
    return pltpu.CompilerParams(dimension_semantics=sem, vmem_limit_bytes=vmem_mib * MIB)


def _dot(a, b):
    return jnp.dot(a, b, preferred_element_type=F32)


def _dot_nt(a, b):
    return lax.dot_general(a, b, (((1,), (1,)), ((), ())), preferred_element_type=F32)


def _rms(x):
    return x * lax.rsqrt(jnp.mean(x * x, axis=-1, keepdims=True) + EPS)


def _norm_to_scratch(x_ref, g_ref, h_ref):
    rows = x_ref.shape[0]
    step = min(NORM_ROWS, rows)

    def body(c, carry):
        r = pl.multiple_of(c * step, step)
        x = x_ref[pl.ds(r, step), :]
        h_ref[pl.ds(r, step), :] = (_rms(x) * g_ref[...]).astype(BF16)
        return carry

    lax.fori_loop(0, rows // step, body, 0)


def _head_norm(acc, gain):
    parts = [_rms(acc[:, h * HEAD_DIM:(h + 1) * HEAD_DIM]) for h in range(acc.shape[1] // HEAD_DIM)]
    return jnp.concatenate(parts, axis=-1) * gain


def _glu_kernel(x_ref, g_ref, wv_ref, wg_ref, o_ref, h_ref):
    @pl.when(pl.program_id(1) == 0)
    def _():
        _norm_to_scratch(x_ref, g_ref, h_ref)

    h = h_ref[...]
    val = _dot(h, wv_ref[...])
    gate = _dot(h, wg_ref[...])
    o_ref[...] = val * jax.nn.sigmoid(gate)


def glu_proj(x, g, w_in, tm):
    m = x.shape[0]
    tn = 512
    nj = D_CONV // tn
    return pl.pallas_call(
        _glu_kernel,
        grid=(m // tm, nj),
        in_specs=[
            pl.BlockSpec((tm, D_MODEL), lambda i, j: (i, 0)),
            pl.BlockSpec((1, D_MODEL), lambda i, j: (0, 0)),
            pl.BlockSpec((D_MODEL, tn), lambda i, j: (0, j)),
            pl.BlockSpec((D_MODEL, tn), lambda i, j: (0, j + nj)),
        ],
        out_specs=pl.BlockSpec((tm, tn), lambda i, j: (i, j)),
        out_shape=jax.ShapeDtypeStruct((m, D_CONV), F32),
        scratch_shapes=[pltpu.VMEM((tm, D_MODEL), BF16)],
        compiler_params=_params(("parallel", "arbitrary"), 48),
        name="glu_proj",
    )(x, g, w_in, w_in)


def _headnorm_proj_kernel(x_ref, g_ref, w_ref, gain_ref, o_ref, h_ref, *, n_normed):
    j = pl.program_id(1)

    @pl.when(j == 0)
    def _():
        _norm_to_scratch(x_ref, g_ref, h_ref)

    acc = _dot(h_ref[...], w_ref[...])

    @pl.when(j < n_normed)
    def _():
        o_ref[...] = _head_norm(acc, gain_ref[0])

    @pl.when(j >= n_normed)
    def _():
        o_ref[...] = acc


def headnorm_proj(x, g, w, col_block0, gains, n_normed, tm):
    m = x.shape[0]
    nj = gains.shape[0]
    return pl.pallas_call(
        functools.partial(_headnorm_proj_kernel, n_normed=n_normed),
        grid=(m // tm, nj),
        in_specs=[
            pl.BlockSpec((tm, D_MODEL), lambda i, j: (i, 0)),
            pl.BlockSpec((1, D_MODEL), lambda i, j: (0, 0)),
            pl.BlockSpec((D_MODEL, D_GRP), lambda i, j: (0, j + col_block0)),
            pl.BlockSpec((1, 1, D_GRP), lambda i, j: (j, 0, 0)),
        ],
        out_specs=pl.BlockSpec((tm, D_GRP), lambda i, j: (i, j)),
        out_shape=jax.ShapeDtypeStruct((m, nj * D_GRP), F32),
        scratch_shapes=[pltpu.VMEM((tm, D_MODEL), BF16)],
        compiler_params=_params(("parallel", "arbitrary"), 48),
        name="headnorm_proj",
    )(x, g, w, gains)


def _gelu_proj_kernel(x_ref, g_ref, w_ref, b_ref, o_ref, h_ref):
    @pl.when(pl.program_id(1) == 0)
    def _():
        _norm_to_scratch(x_ref, g_ref, h_ref)

    z = _dot(h_ref[...], w_ref[...]) + b_ref[...]
    o_ref[...] = 0.5 * z * (1.0 + lax.erf(z * np.float32(math.sqrt(0.5))))


def gelu_proj(x, g, w, b, tm):
    m = x.shape[0]
    n = w.shape[1]
    tn = 512
    return pl.pallas_call(
        _gelu_proj_kernel,
        grid=(m // tm, n // tn),
        in_specs=[
            pl.BlockSpec((tm, D_MODEL), lambda i, j: (i, 0)),
            pl.BlockSpec((1, D_MODEL), lambda i, j: (0, 0)),
            pl.BlockSpec((D_MODEL, tn), lambda i, j: (0, j)),
            pl.BlockSpec((1, tn), lambda i, j: (0, j)),
        ],
        out_specs=pl.BlockSpec((tm, tn), lambda i, j: (i, j)),
        out_shape=jax.ShapeDtypeStruct((m, n), F32),
        scratch_shapes=[pltpu.VMEM((tm, D_MODEL), BF16)],
        compiler_params=_params(("parallel", "arbitrary"), 48),
        name="gelu_proj",
    )(x, g, w, b)


def _ffn_kernel(x_ref, g_ref, w1_ref, w2_ref, o_ref, h_ref):
    @pl.when(pl.program_id(1) == 0)
    def _():
        _norm_to_scratch(x_ref, g_ref, h_ref)
        o_ref[...] = x_ref[...]

    hid = jnp.maximum(_dot(h_ref[...], w1_ref[...]), 0.0)
    o_ref[...] += _dot((hid * hid).astype(BF16), w2_ref[...])


def ffn(x, g, w1, w2, tm, tf):
    m = x.shape[0]
    return pl.pallas_call(
        _ffn_kernel,
        grid=(m // tm, D_FF // tf),
        in_specs=[
            pl.BlockSpec((tm, D_MODEL), lambda i, f: (i, 0)),
            pl.BlockSpec((1, D_MODEL), lambda i, f: (0, 0)),
            pl.BlockSpec((D_MODEL, tf), lambda i, f: (0, f)),
            pl.BlockSpec((tf, D_MODEL), lambda i, f: (f, 0)),
        ],
        out_specs=pl.BlockSpec((tm, D_MODEL), lambda i, f: (i, 0)),
        out_shape=jax.ShapeDtypeStruct((m, D_MODEL), F32),
        scratch_shapes=[pltpu.VMEM((tm, D_MODEL), BF16)],
        compiler_params=_params(("parallel", "arbitrary"), 56),
        name="ffn",
    )(x, g, w1, w2)


def _ln_silu(acc, lg_ref, lb_ref):
    mu = jnp.mean(acc, axis=-1, keepdims=True)
    xc = acc - mu
    y = xc * lax.rsqrt(jnp.mean(xc * xc, axis=-1, keepdims=True) + EPS)
    y = y * lg_ref[...] + lb_ref[...]
    return y * jax.nn.sigmoid(y)


def _conv_prompt_kernel(a_ref, halo_ref, w_ref, cb_ref, lg_ref, lb_ref, o_ref, sh_ref):
    tt = a_ref.shape[0]
    first = pl.program_id(1) == 0
    sh_ref[0, 0:CONV_HALO, :] = jnp.where(first, 0.0, halo_ref[...])
    sh_ref[0, CONV_HALO:, :] = a_ref[...]
    span = tt + CONV_HALO - SUBLANES
    for s in range(1, SUBLANES):
        sh_ref[s, 0:span, :] = sh_ref[0, s:s + span, :]
    lead = CONV_HALO - (CONV_W - 1)

    def body(c, carry):
        r = pl.multiple_of(c * CONV_ROWS, CONV_ROWS)
        acc = jnp.zeros((CONV_ROWS, D_CONV), F32) + cb_ref[...]
        for k in range(CONV_W):
            q, s = divmod(k + lead, SUBLANES)
            acc = acc + sh_ref[s, pl.ds(r + q * SUBLANES, CONV_ROWS), :] * w_ref[k]
        o_ref[pl.ds(r, CONV_ROWS), :] = _ln_silu(acc, lg_ref, lb_ref).astype(o_ref.dtype)
        return carry

    lax.fori_loop(0, tt // CONV_ROWS, body, 0)


def conv_prompt(a, w, cb, lg, lb, tt):
    b, t, _ = a.shape
    hb = tt // CONV_HALO
    vec = pl.BlockSpec((1, D_CONV), lambda i, j: (0, 0))
    return pl.pallas_call(
        _conv_prompt_kernel,
        grid=(b, t // tt),
        in_specs=[
            pl.BlockSpec((None, tt, D_CONV), lambda i, j: (i, j, 0)),
            pl.BlockSpec((None, CONV_HALO, D_CONV), lambda i, j: (i, jnp.maximum(j * hb - 1, 0), 0)),
            pl.BlockSpec((CONV_W, 1, D_CONV), lambda i, j: (0, 0, 0)),
            vec, vec, vec,
        ],
        out_specs=pl.BlockSpec((None, tt, D_CONV), lambda i, j: (i, j, 0)),
        out_shape=jax.ShapeDtypeStruct((b, t, D_CONV), BF16),
        scratch_shapes=[pltpu.VMEM((SUBLANES, CONV_HALO + tt, D_CONV), F32)],
        compiler_params=_params(("parallel", "arbitrary"), 40),
        name="conv_prompt",
    )(a, a, w, cb, lg, lb)


def _conv_sample_kernel(apad_ref, w_ref, cb_ref, lg_ref, lb_ref, o_ref):
    rows = o_ref.shape[0]
    acc = jnp.zeros((rows, D_CONV), F32) + cb_ref[...]
    for k in range(CONV_W):
        acc = acc + apad_ref[k:k + rows, :] * w_ref[k]
    o_ref[...] = _ln_silu(acc, lg_ref, lb_ref).astype(o_ref.dtype)


def conv_sample(apad, w, cb, lg, lb):
    b, tp, _ = apad.shape
    t = tp - (CONV_W - 1)
    vec = pl.BlockSpec((1, D_CONV), lambda i: (0, 0))
    return pl.pallas_call(
        _conv_sample_kernel,
        grid=(b,),
        in_specs=[
            pl.BlockSpec((None, tp, D_CONV), lambda i: (i, 0, 0)),
            pl.BlockSpec((CONV_W, 1, D_CONV), lambda i: (0, 0, 0)),
            vec, vec, vec,
        ],
        out_specs=pl.BlockSpec((None, t, D_CONV), lambda i: (i, 0, 0)),
        out_shape=jax.ShapeDtypeStruct((b, t, D_CONV), BF16),
        compiler_params=_params(("parallel",), 32),
        name="conv_sample",
    )(apad, w, cb, lg, lb)


def _softmax_pv(s, v):
    mx = jnp.max(s, axis=-1, keepdims=True)
    p = jnp.exp(s - mx)
    l = jnp.sum(p, axis=-1, keepdims=True)
    o = _dot(p.astype(BF16), v.astype(BF16)) / l
    return o, mx + jnp.log(l)


def _dil_prompt_kernel(q_ref, k_ref, v_ref, kh_ref, vh_ref, o_ref, lse_ref, kbuf_ref, vbuf_ref, *, n):
    tq = q_ref.shape[0]
    first = pl.program_id(2) == 0
    kbuf_ref[0:n, :] = kh_ref[...].astype(BF16)
    vbuf_ref[0:n, :] = vh_ref[...].astype(BF16)
    kbuf_ref[n:, :] = k_ref[...].astype(BF16)
    vbuf_ref[n:, :] = v_ref[...].astype(BF16)
    qi = lax.broadcasted_iota(jnp.int32, (n, 2 * n), 0)
    kj = lax.broadcasted_iota(jnp.int32, (n, 2 * n), 1)
    band = (kj > qi) & (kj <= qi + n)
    band_first = band & (kj >= jnp.where(first, n, 0))
    for blk in range(tq // n):
        mask = band_first if blk == 0 else band
        for h in range(HEADS_PER_GROUP):
            cols = slice(h * HEAD_DIM, (h + 1) * HEAD_DIM)
            q = q_ref[blk * n:(blk + 1) * n, cols].astype(BF16)
            k = kbuf_ref[blk * n:(blk + 2) * n, cols]
            v = vbuf_ref[blk * n:(blk + 2) * n, cols]
            s = jnp.where(mask, _dot_nt(q, k), NEG)
            o, lse = _softmax_pv(s, v)
            o_ref[blk * n:(blk + 1) * n, cols] = o
            lse_ref[blk * n:(blk + 1) * n, cols] = jnp.broadcast_to(lse, (n, HEAD_DIM))


def dilated_attn_prompt(qkv, gi, batch, seq):
    win, dil = DIL_GROUPS[gi]
    n = win // dil
    s = seq // dil
    tq = min(s, 512)
    assert s % n == 0 and tq % n == 0 and n % 8 == 0
    ncol = 3 * N_DGROUPS
    view = qkv.reshape(batch, s, dil * ncol * D_GRP)
    hb = tq // n

    def spec(which, halo):
        rows = n if halo else tq

        def imap(b, r, c):
            row = jnp.maximum(c * hb - 1, 0) if halo else c
            return (b, row, r * ncol + which * N_DGROUPS + gi)

        return pl.BlockSpec((None, rows, D_GRP), imap)

    out_spec = pl.BlockSpec((None, tq, D_GRP), lambda b, r, c: (b, c, r))
    out_sds = jax.ShapeDtypeStruct((batch, s, dil * D_GRP), F32)
    o, lse = pl.pallas_call(
        functools.partial(_dil_prompt_kernel, n=n),
        grid=(batch, dil, s // tq),
        in_specs=[spec(0, False), spec(1, False), spec(2, False), spec(1, True), spec(2, True)],
        out_specs=[out_spec, out_spec],
        out_shape=[out_sds, out_sds],
        scratch_shapes=[pltpu.VMEM((n + tq, D_GRP), BF16), pltpu.VMEM((n + tq, D_GRP), BF16)],
        compiler_params=_params(("parallel", "parallel", "arbitrary"), 32),
        name="dil_attn_prompt_g%d" % gi,
    )(view, view, view, view, view)
    return o.reshape(batch * seq, D_GRP), lse.reshape(batch * seq, D_GRP)


def _dil_sample_kernel(q_ref, kn_ref, vn_ref, kc_ref, vc_ref, o_ref, lse_ref, *, dil, n):
    ds, cache_len = q_ref.shape[0], kc_ref.shape[0]
    qi = lax.broadcasted_iota(jnp.int32, (ds, cache_len), 0)
    pj = lax.broadcasted_iota(jnp.int32, (ds, cache_len), 1)
    dist = cache_len + qi - pj
    mask_c = (dist % dil == 0) & (dist <= dil * (n - 1))
    qn = lax.broadcasted_iota(jnp.int32, (ds, ds), 0)
    pn = lax.broadcasted_iota(jnp.int32, (ds, ds), 1)
    mask_n = (qn >= pn) & ((qn - pn) % dil == 0)
    for h in range(HEADS_PER_GROUP):
        cols = slice(h * HEAD_DIM, (h + 1) * HEAD_DIM)
        q = q_ref[:, cols].astype(BF16)
        s_c = jnp.where(mask_c, _dot_nt(q, kc_ref[:, cols].astype(BF16)), NEG)
        s_n = jnp.where(mask_n, _dot_nt(q, kn_ref[:, cols].astype(BF16)), NEG)
        mx = jnp.maximum(jnp.max(s_c, axis=-1, keepdims=True), jnp.max(s_n, axis=-1, keepdims=True))
        p_c = jnp.exp(s_c - mx)
        p_n = jnp.exp(s_n - mx)
        l = jnp.sum(p_c, axis=-1, keepdims=True) + jnp.sum(p_n, axis=-1, keepdims=True)
        o = _dot(p_c.astype(BF16), vc_ref[:, cols].astype(BF16)) + _dot(p_n.astype(BF16), vn_ref[:, cols].astype(BF16))
        o_ref[:, cols] = o / l
        lse_ref[:, cols] = jnp.broadcast_to(mx + jnp.log(l), (ds, HEAD_DIM))


def dilated_attn_sample(qkv, k_cache, v_cache, gi):
    win, dil = DIL_GROUPS[gi]
    n = win // dil
    b, ds, _ = qkv.shape
    cache_len = k_cache.shape[1]
    assert cache_len - dil * (n - 1) >= 0

    def new(which):
        return pl.BlockSpec((None, ds, D_GRP), lambda i: (i, 0, which * N_DGROUPS + gi))

    cache = pl.BlockSpec((None, cache_len, D_GRP), lambda i: (i, 0, 0))
    out_spec = pl.BlockSpec((None, ds, D_GRP), lambda i: (i, 0, 0))
    out_sds = jax.ShapeDtypeStruct((b, ds, D_GRP), F32)
    return pl.pallas_call(
        functools.partial(_dil_sample_kernel, dil=dil, n=n),
        grid=(b,),
        in_specs=[new(0), new(1), new(2), cache, cache],
        out_specs=[out_spec, out_spec],
        out_shape=[out_sds, out_sds],
        compiler_params=_params(("parallel",), 40),
        name="dil_attn_sample_g%d" % gi,
    )(qkv, qkv, qkv, k_cache, v_cache)


def _mix_out_kernel(x_ref, a_ref, o0_ref, o1_ref, o2_ref, l0_ref, l1_ref, l2_ref, wa_ref, wb_ref, y_ref):
    l0, l1, l2 = l0_ref[...], l1_ref[...], l2_ref[...]
    mx = jnp.maximum(jnp.maximum(l0, l1), l2)
    e0, e1, e2 = jnp.exp(l0 - mx), jnp.exp(l1 - mx), jnp.exp(l2 - mx)
    b = (e0 * o0_ref[...] + e1 * o1_ref[...] + e2 * o2_ref[...]) / (e0 + e1 + e2)
    y_ref[...] = x_ref[...] + _dot(a_ref[...], wa_ref[...]) + _dot(b.astype(BF16), wb_ref[...])


def mix_out(x, a, outs, lses, w_out, tm):
    m = x.shape[0]
    row = lambda width: pl.BlockSpec((tm, width), lambda i: (i, 0))
    return pl.pallas_call(
        _mix_out_kernel,
        grid=(m // tm,),
        in_specs=[row(D_MODEL), row(D_CONV)] + [row(D_GRP)] * 6 + [
            pl.BlockSpec((D_CONV, D_MODEL), lambda i: (0, 0)),
            pl.BlockSpec((D_GRP, D_MODEL), lambda i: (D_CONV // D_GRP, 0)),
        ],
        out_specs=row(D_MODEL),
        out_shape=jax.ShapeDtypeStruct((m, D_MODEL), F32),
        compiler_params=_params(("parallel",), 48),
        name="mix_out",
    )(x, a, *outs, *lses, w_out, w_out)


def _sgu_kernel(x_ref, u_ref, gv_ref, lg_ref, lb_ref, ws_ref, bs_ref, w_ref, *out_refs, emit_v):
    y_ref = out_refs[0]
    gv = gv_ref[...]
    mu = jnp.mean(gv, axis=-1, keepdims=True)
    vc = gv - mu
    v = vc * lax.rsqrt(jnp.mean(vc * vc, axis=-1, keepdims=True) + EPS) * lg_ref[...] + lb_ref[...]
    if emit_v:
        out_refs[1][...] = v
    vb = v.astype(BF16)
    rows, cm = x_ref.shape[0], ws_ref.shape[1]
    gated = []
    for g in range(N_SG):
        cols = slice(g * D_SG, (g + 1) * D_SG)
        sv = jnp.concatenate([_dot(ws_ref[g], vb[c * cm:(c + 1) * cm, cols]) + bs_ref[g]
                              for c in range(rows // cm)], axis=0)
        gated.append((u_ref[:, cols] * sv).astype(BF16))
    y_ref[...] = x_ref[...] + _dot(jnp.concatenate(gated, axis=-1), w_ref[...])


def sgu_out(x, z, lg, lb, ws, bs, w_out, tm, emit_v):
    m = x.shape[0]
    cm = ws.shape[1]
    assert tm % cm == 0
    row = lambda jblk: pl.BlockSpec((tm, D_GATE), lambda i: (i, jblk))
    vec = pl.BlockSpec((1, D_GATE), lambda i: (0, 0))
    out_specs = [row(0)]
    out_shape = [jax.ShapeDtypeStruct((m, D_MODEL), F32)]
    if emit_v:
        out_specs.append(row(0))
        out_shape.append(jax.ShapeDtypeStruct((m, D_GATE), F32))
    res = pl.pallas_call(
        functools.partial(_sgu_kernel, emit_v=emit_v),
        grid=(m // tm,),
        in_specs=[row(0), row(0), row(1), vec, vec,
                  pl.BlockSpec((N_SG, cm, cm), lambda i: (0, 0, 0)),
                  pl.BlockSpec((N_SG, cm, D_SG), lambda i: (0, 0, 0)),
                  pl.BlockSpec((D_GATE, D_MODEL), lambda i: (0, 0))],
        out_specs=out_specs,
        out_shape=out_shape,
        compiler_params=_params(("parallel",), 52),
        name="sgu_out",
    )(x, z, z, lg, lb, ws, bs, w_out)
    return res if emit_v else (res[0], None)


def _mem_attn_kernel(x_ref, g_ref, wq_ref, qg_ref, k_ref, v_ref, wo_ref, y_ref):
    x = x_ref[...]
    h = (_rms(x) * g_ref[...]).astype(BF16)
    q = _head_norm(_dot(h, wq_ref[...]), qg_ref[...]).astype(BF16)
    outs = []
    for hd in range(MEM_HEADS):
        cols = slice(hd * HEAD_DIM, (hd + 1) * HEAD_DIM)
        s = _dot_nt(q[:, cols], k_ref[:, cols].astype(BF16))
        o, _ = _softmax_pv(s, v_ref[:, cols])
        outs.append(o.astype(BF16))
    y_ref[...] = x + _dot(jnp.concatenate(outs, axis=-1), wo_ref[...])


def mem_attn(x, g, wq, q_gain, k, v, wo, tm):
    b, t, _ = x.shape
    full = lambda shape: pl.BlockSpec(shape, lambda i, j: (0,) * len(shape))
    kv = pl.BlockSpec((None, N_MEM, D_MEMATT), lambda i, j: (i, 0, 0))
    xs = pl.BlockSpec((None, tm, D_MODEL), lambda i, j: (i, j, 0))
    return pl.pallas_call(
        _mem_attn_kernel,
        grid=(b, t // tm),
        in_specs=[xs, full((1, D_MODEL)), full((D_MODEL, D_MEMATT)), full((1, D_MEMATT)), kv, kv,
                  full((D_MEMATT, D_MODEL))],
        out_specs=xs,
        out_shape=jax.ShapeDtypeStruct((b, t, D_MODEL), F32),
        compiler_params=_params(("parallel", "parallel"), 40),
        name="mem_attn",
    )(x, g, wq, q_gain, k, v, wo)


def _tile4(g):
    return jnp.tile(g, HEADS_PER_GROUP)


def _row_tile(m):
    return min(m, 1024)


def kernel(x_prompt, x_sample, mem_prompt, state_conv, cache_k_w128, cache_v_w128, cache_k_w512, cache_v_w512,
           cache_k_w2048, cache_v_w2048, cache_mem_k, cache_mem_v, g_mix, w_in_e, conv_w, conv_b, conv_ln_g,
           conv_ln_b, q_norm_e, k_norm_e, w_out_e, w_in_o, b_in_o, v_ln_g, v_ln_b, w_s, b_s, w_out_o, g_xmem,
           g_mem, wq_mem, wk_mem, wv_mem, q_norm_mem, k_norm_mem, wo_mem, g_ffn, w_ffn1, w_ffn2):
    kv_in = (cache_k_w128, cache_v_w128, cache_k_w512, cache_v_w512, cache_k_w2048, cache_v_w2048)
    depth = g_mix.shape[0]
    bp, tp, _ = x_prompt.shape
    bs, ts, _ = x_sample.shape
    mp, ms = bp * tp, bs * ts
    scale = HEAD_DIM ** -0.5
    row = lambda v: v.reshape(1, -1)

    xp = x_prompt.reshape(mp, D_MODEL)
    xs = x_sample.reshape(ms, D_MODEL)
    mem = mem_prompt.reshape(bp * N_MEM, D_MODEL)

    conv_pl, conv_sl, kv_pl, kv_sl, memk_pl, memv_pl, chunk_sl = [], [], [], [], [], [], []
    for i in range(depth):
        j = i // 2
        if i % 2 == 0:
            w_in = w_in_e[j].astype(BF16)
            w_out = w_out_e[j].astype(BF16)
            gains = jnp.concatenate([
                jnp.stack([_tile4(q_norm_e[j, gi]) * scale for gi in range(N_DGROUPS)]),
                jnp.stack([_tile4(k_norm_e[j, gi]) for gi in range(N_DGROUPS)]),
                jnp.ones((N_DGROUPS, D_GRP), F32)])[:, None, :]
            cw = conv_w[j][:, None, :]
            conv_args = (cw, row(conv_b[j]), row(conv_ln_g[j]), row(conv_ln_b[j]))

            a = glu_proj(xp, row(g_mix[i]), w_in, _row_tile(mp))
            qkv = headnorm_proj(xp, row(g_mix[i]), w_in, 2 * D_CONV // D_GRP, gains, 2 * N_DGROUPS, _row_tile(mp))
            a3 = a.reshape(bp, tp, D_CONV)
            conv_pl.append(a3[:, tp - (CONV_W - 1):])
            a_out = conv_prompt(a3, *conv_args, tt=512).reshape(mp, D_CONV)
            outs, lses, new_kv = [], [], []
            qkv5 = qkv.reshape(bp, tp, 3, N_DGROUPS, HEADS_PER_GROUP, HEAD_DIM)
            for gi, (win, _) in enumerate(DIL_GROUPS):
                o, l = dilated_attn_prompt(qkv, gi, bp, tp)
                outs.append(o)
                lses.append(l)
                keep = min(win, tp)
                new_kv += [qkv5[:, tp - keep:, 1, gi], qkv5[:, tp - keep:, 2, gi]]
            kv_pl.append(new_kv)
            xp = mix_out(xp, a_out, outs, lses, w_out, 512)

            a = glu_proj(xs, row(g_mix[i]), w_in, _row_tile(ms))
            qkv = headnorm_proj(xs, row(g_mix[i]), w_in, 2 * D_CONV // D_GRP, gains, 2 * N_DGROUPS, _row_tile(ms))
            apad = jnp.concatenate([state_conv[j], a.reshape(bs, ts, D_CONV)], axis=1)
            conv_sl.append(apad[:, apad.shape[1] - (CONV_W - 1):])
            a_out = conv_sample(apad, *conv_args).reshape(ms, D_CONV)
            outs, lses, new_kv = [], [], []
            qkv3 = qkv.reshape(bs, ts, 3 * D_ATT)
            qkv5 = qkv.reshape(bs, ts, 3, N_DGROUPS, HEADS_PER_GROUP, HEAD_DIM)
            for gi in range(N_DGROUPS):
                kc = kv_in[2 * gi][j].reshape(bs, -1, D_GRP)
                vc = kv_in[2 * gi + 1][j].reshape(bs, -1, D_GRP)
                o, l = dilated_attn_sample(qkv3, kc, vc, gi)
                outs.append(o.reshape(ms, D_GRP))
                lses.append(l.reshape(ms, D_GRP))
                new_kv += [qkv5[:, :, 1, gi], qkv5[:, :, 2, gi]]
            kv_sl.append(new_kv)
            xs = mix_out(xs, a_out, outs, lses, w_out, _row_tile(ms))
        else:
            w_in = w_in_o[j].astype(BF16)
            w_out = w_out_o[j].astype(BF16)
            tril = jnp.tril(jnp.ones((CHUNK, CHUNK), F32))
            for which in range(2):
                x, t = (xp, tp) if which == 0 else (xs, ts)
                m = x.shape[0]
                c = min(CHUNK, t)
                ws = w_s[j][:, :c, :c] * tril[:c, :c]
                bias = b_s[j][:, :c]
                tm = 256
                if c < CHUNK:
                    ws = jnp.einsum('ab,gts->gatbs', jnp.eye(tm // c, dtype=F32), ws).reshape(N_SG, tm, tm)
                    bias = jnp.tile(bias, (1, tm // c))
                bias = jnp.broadcast_to(bias[:, :, None], bias.shape + (D_SG,))
                z = gelu_proj(x, row(g_mix[i]), w_in, row(b_in_o[j]), _row_tile(m))
                y, v = sgu_out(x, z, row(v_ln_g[j]), row(v_ln_b[j]), ws.astype(BF16), bias, w_out, tm,
                               emit_v=which == 1)
                if which == 0:
                    xp = y
                else:
                    xs = y
                    chunk_sl.append(v.reshape(bs, ts, D_GATE))

        kgain = jnp.stack([_tile4(k_norm_mem[i]), jnp.ones((D_MEMATT,), F32)])[:, None, :]
        wkv = jnp.concatenate([wk_mem[i], wv_mem[i]], axis=1).astype(BF16)
        mkv = headnorm_proj(mem, row(g_mem[i]), wkv, 0, kgain, 1, _row_tile(mem.shape[0]))
        mk = mkv[:, :D_MEMATT].reshape(bp, N_MEM, D_MEMATT)
        mv = mkv[:, D_MEMATT:].reshape(bp, N_MEM, D_MEMATT)
        memk_pl.append(mk.reshape(bp, N_MEM, MEM_HEADS, HEAD_DIM))
        memv_pl.append(mv.reshape(bp, N_MEM, MEM_HEADS, HEAD_DIM))
        wq = wq_mem[i].astype(BF16)
        wo = wo_mem[i].astype(BF16)
        qgain = row(_tile4(q_norm_mem[i]) * scale)
        xp = mem_attn(xp.reshape(bp, tp, D_MODEL), row(g_xmem[i]), wq, qgain, mk, mv, wo, 512).reshape(mp, D_MODEL)
        xs = mem_attn(xs.reshape(bs, ts, D_MODEL), row(g_xmem[i]), wq, qgain,
                      cache_mem_k[i].reshape(bs, N_MEM, D_MEMATT), cache_mem_v[i].reshape(bs, N_MEM, D_MEMATT),
                      wo, ts).reshape(ms, D_MODEL)

        w1 = w_ffn1[i].astype(BF16)
        w2 = w_ffn2[i].astype(BF16)
        xp = ffn(xp, row(g_ffn[i]), w1, w2, _row_tile(mp), 512)
        xs = ffn(xs, row(g_ffn[i]), w1, w2, _row_tile(ms), 512)

    stack = lambda items: jnp.stack(items)
    kv_p = [stack([kv[n] for kv in kv_pl]) for n in range(2 * N_DGROUPS)]
    kv_s = [stack([kv[n] for kv in kv_sl]) for n in range(2 * N_DGROUPS)]
    return (xp.reshape(bp, tp, D_MODEL), xs.reshape(bs, ts, D_MODEL), stack(conv_pl), stack(conv_sl),
            *kv_p, *kv_s, stack(memk_pl), stack(memv_pl), stack(chunk_sl))
```

```python
import functools
import math

import numpy as np
import jax
import jax.numpy as jnp
from jax import lax
from jax.experimental import pallas as pl
from jax.experimental.pallas import tpu as pltpu

D_MODEL = 2048
EPS = 1e-6
NEG = -1e30
D_CONV = D_MODEL // 2
CONV_W = 31
HEAD_DIM = 128
DIL_GROUPS = ((128, 1), (512, 4), (2048, 16))
N_DGROUPS = len(DIL_GROUPS)
HEADS = 4
D_GRP = HEADS * HEAD_DIM
D_ATT = N_DGROUPS * D_GRP
CHUNK = 128
D_GATE = D_MODEL
N_SG = 8
D_SG = D_GATE // N_SG
N_MEM = 256
D_MEMATT = HEADS * HEAD_DIM
D_FF = 4 * D_MODEL

F32 = jnp.float32
BF16 = jnp.bfloat16
MIB = 1024 * 1024
NORM_ROWS = 256
CONV_ROWS = 32
CONV_HALO = 32
SUBLANES = 8
MERGE_ROWS = 256


def _params(sem, vmem_mib):
    return pltpu.CompilerParams(dimension_semantics=sem, vmem_limit_bytes=vmem_mib * MIB)


def _dot(a, b):
    return jnp.dot(a, b, preferred_element_type=F32)


def _dot_nt(a, b):
    return lax.dot_general(a, b, (((1,), (1,)), ((), ())), preferred_element_type=F32)


def _rms(x):
    return x * lax.rsqrt(jnp.mean(x * x, axis=-1, keepdims=True) + EPS)


def _norm_to_scratch(x_ref, g_ref, h_ref):
    rows = x_ref.shape[0]
    step = min(NORM_ROWS, rows)

    def body(c, carry):
        r = pl.multiple_of(c * step, step)
        x = x_ref[pl.ds(r, step), :]
        h_ref[pl.ds(r, step), :] = (_rms(x) * g_ref[...]).astype(BF16)
        return carry

    lax.fori_loop(0, rows // step, body, 0)


def _head_norm(acc, gain):
    parts = [_rms(acc[:, h * HEAD_DIM:(h + 1) * HEAD_DIM]) for h in range(acc.shape[1] // HEAD_DIM)]
    return jnp.concatenate(parts, axis=-1) * gain


def _head_rows(ref, h, rows):
    return ref[pl.ds(h, rows, stride=HEADS), :]


def _softmax_pv(s, v):
    mx = jnp.max(s, axis=-1, keepdims=True)
    p = jnp.exp(s - mx)
    l = jnp.sum(p, axis=-1, keepdims=True)
    return _dot(p.astype(BF16), v) / l, mx + jnp.log(l)


def _merge3(outs, lses):
    mx = jnp.maximum(jnp.maximum(lses[0], lses[1]), lses[2])
    es = [jnp.exp(l - mx) for l in lses]
    return (es[0] * outs[0] + es[1] * outs[1] + es[2] * outs[2]) / (es[0] + es[1] + es[2])


def _glu_kernel(x_ref, g_ref, wv_ref, wg_ref, o_ref, h_ref):
    @pl.when(pl.program_id(1) == 0)
    def _():
        _norm_to_scratch(x_ref, g_ref, h_ref)

    h = h_ref[...]
    val = _dot(h, wv_ref[...])
    gate = _dot(h, wg_ref[...])
    o_ref[...] = val * jax.nn.sigmoid(gate)


def glu_proj(x, g, w_in, layer, tm):
    m = x.shape[0]
    tn = 512
    nj = D_CONV // tn
    return pl.pallas_call(
        _glu_kernel,
        grid=(m // tm, nj),
        in_specs=[
            pl.BlockSpec((tm, D_MODEL), lambda i, j: (i, 0)),
            pl.BlockSpec((1, D_MODEL), lambda i, j: (0, 0)),
            pl.BlockSpec((None, D_MODEL, tn), lambda i, j: (layer, 0, j)),
            pl.BlockSpec((None, D_MODEL, tn), lambda i, j: (layer, 0, j + nj)),
        ],
        out_specs=pl.BlockSpec((tm, tn), lambda i, j: (i, j)),
        out_shape=jax.ShapeDtypeStruct((m, D_CONV), F32),
        scratch_shapes=[pltpu.VMEM((tm, D_MODEL), BF16)],
        compiler_params=_params(("parallel", "arbitrary"), 48),
        name="glu_proj",
    )(x, g, w_in, w_in)


def _headnorm_proj_kernel(x_ref, g_ref, w_ref, gain_ref, *rest, n_normed, emit_main, th_tiles):
    n_th = len(th_tiles)
    if emit_main:
        o_ref, th_refs, h_ref = rest[0], rest[1:1 + n_th], rest[1 + n_th]
    else:
        th_refs, h_ref, o_ref = rest[:n_th], rest[n_th], rest[n_th + 1]
    i, j = pl.program_id(0), pl.program_id(1)
    tm = x_ref.shape[0]

    @pl.when(j == 0)
    def _():
        _norm_to_scratch(x_ref, g_ref, h_ref)

    acc = _dot(h_ref[...], w_ref[...])

    @pl.when(j < n_normed)
    def _():
        o_ref[...] = _head_norm(acc, gain_ref[0])

    @pl.when(j >= n_normed)
    def _():
        o_ref[...] = acc

    for th_ref, (col, period) in zip(th_refs, th_tiles):
        rows = th_ref.shape[0] // HEADS

        @pl.when((j == col) & (i % period == period - 1))
        def _():
            for h in range(HEADS):
                th_ref[pl.ds(h, rows, stride=HEADS), :] = o_ref[tm - rows:, h * HEAD_DIM:(h + 1) * HEAD_DIM]


def _th_index(period, i, j):
    return (i // period, 0)


def headnorm_proj(x, g, w, layer, col_block0, gains, n_normed, tm, th_tiles, emit_main):
    m = x.shape[0]
    nj = gains.shape[0]
    out_specs, out_shape = [], []
    if emit_main:
        out_specs.append(pl.BlockSpec((tm, D_GRP), lambda i, j: (i, j)))
        out_shape.append(jax.ShapeDtypeStruct((m, nj * D_GRP), F32))
    for _, keep, period in th_tiles:
        assert keep <= tm and (m // tm) % period == 0
        out_specs.append(pl.BlockSpec((keep * HEADS, HEAD_DIM), functools.partial(_th_index, period)))
        out_shape.append(jax.ShapeDtypeStruct((m // tm // period * keep * HEADS, HEAD_DIM), F32))
    scratch = [pltpu.VMEM((tm, D_MODEL), BF16)]
    if not emit_main:
        scratch.append(pltpu.VMEM((tm, D_GRP), F32))
    return pl.pallas_call(
        functools.partial(_headnorm_proj_kernel, n_normed=n_normed, emit_main=emit_main,
                          th_tiles=tuple((col, period) for col, _, period in th_tiles)),
        grid=(m // tm, nj),
        in_specs=[
            pl.BlockSpec((tm, D_MODEL), lambda i, j: (i, 0)),
            pl.BlockSpec((1, D_MODEL), lambda i, j: (0, 0)),
            pl.BlockSpec((None, D_MODEL, D_GRP), lambda i, j: (layer, 0, j + col_block0)),
            pl.BlockSpec((1, 1, D_GRP), lambda i, j: (j, 0, 0)),
        ],
        out_specs=out_specs,
        out_shape=out_shape,
        scratch_shapes=scratch,
        compiler_params=_params(("arbitrary", "arbitrary"), 52),
        name="headnorm_proj",
    )(x, g, w, gains)


def _gelu_proj_kernel(x_ref, g_ref, w_ref, b_ref, o_ref, h_ref):
    @pl.when(pl.program_id(1) == 0)
    def _():
        _norm_to_scratch(x_ref, g_ref, h_ref)

    z = _dot(h_ref[...], w_ref[...]) + b_ref[...]
    o_ref[...] = 0.5 * z * (1.0 + lax.erf(z * np.float32(math.sqrt(0.5))))


def gelu_proj(x, g, w, layer, b, tm):
    m = x.shape[0]
    n = w.shape[2]
    tn = 512
    return pl.pallas_call(
        _gelu_proj_kernel,
        grid=(m // tm, n // tn),
        in_specs=[
            pl.BlockSpec((tm, D_MODEL), lambda i, j: (i, 0)),
            pl.BlockSpec((1, D_MODEL), lambda i, j: (0, 0)),
            pl.BlockSpec((None, D_MODEL, tn), lambda i, j: (layer, 0, j)),
            pl.BlockSpec((1, tn), lambda i, j: (0, j)),
        ],
        out_specs=pl.BlockSpec((tm, tn), lambda i, j: (i, j)),
        out_shape=jax.ShapeDtypeStruct((m, n), F32),
        scratch_shapes=[pltpu.VMEM((tm, D_MODEL), BF16)],
        compiler_params=_params(("parallel", "arbitrary"), 48),
        name="gelu_proj",
    )(x, g, w, b)


def _ffn_kernel(x_ref, g_ref, w1_ref, w2_ref, o_ref, h_ref):
    @pl.when(pl.program_id(1) == 0)
    def _():
        _norm_to_scratch(x_ref, g_ref, h_ref)
        o_ref[...] = x_ref[...]

    hid = jnp.maximum(_dot(h_ref[...], w1_ref[...]), 0.0)
    o_ref[...] += _dot((hid * hid).astype(BF16), w2_ref[...])


def ffn(x, g, w1, w2, layer, tm, tf):
    m = x.shape[0]
    return pl.pallas_call(
        _ffn_kernel,
        grid=(m // tm, D_FF // tf),
        in_specs=[
            pl.BlockSpec((tm, D_MODEL), lambda i, f: (i, 0)),
            pl.BlockSpec((1, D_MODEL), lambda i, f: (0, 0)),
            pl.BlockSpec((None, D_MODEL, tf), lambda i, f: (layer, 0, f)),
            pl.BlockSpec((None, tf, D_MODEL), lambda i, f: (layer, f, 0)),
        ],
        out_specs=pl.BlockSpec((tm, D_MODEL), lambda i, f: (i, 0)),
        out_shape=jax.ShapeDtypeStruct((m, D_MODEL), F32),
        scratch_shapes=[pltpu.VMEM((tm, D_MODEL), BF16)],
        compiler_params=_params(("parallel", "arbitrary"), 56),
        name="ffn",
    )(x, g, w1, w2)


def _ln_silu(acc, lg_ref, lb_ref):
    mu = jnp.mean(acc, axis=-1, keepdims=True)
    xc = acc - mu
    y = xc * lax.rsqrt(jnp.mean(xc * xc, axis=-1, keepdims=True) + EPS)
    y = y * lg_ref[...] + lb_ref[...]
    return y * jax.nn.sigmoid(y)


def _conv_prompt_kernel(a_ref, halo_ref, w_ref, cb_ref, lg_ref, lb_ref, o_ref, sh_ref):
    tt = a_ref.shape[0]
    first = pl.program_id(1) == 0
    sh_ref[0, 0:CONV_HALO, :] = jnp.where(first, 0.0, halo_ref[...])
    sh_ref[0, CONV_HALO:, :] = a_ref[...]
    span = tt + CONV_HALO - SUBLANES
    for s in range(1, SUBLANES):
        sh_ref[s, 0:span, :] = sh_ref[0, s:s + span, :]
    lead = CONV_HALO - (CONV_W - 1)

    def body(c, carry):
        r = pl.multiple_of(c * CONV_ROWS, CONV_ROWS)
        acc = jnp.zeros((CONV_ROWS, D_CONV), F32) + cb_ref[...]
        for k in range(CONV_W):
            q, s = divmod(k + lead, SUBLANES)
            acc = acc + sh_ref[s, pl.ds(r + q * SUBLANES, CONV_ROWS), :] * w_ref[k]
        o_ref[pl.ds(r, CONV_ROWS), :] = _ln_silu(acc, lg_ref, lb_ref).astype(o_ref.dtype)
        return carry

    lax.fori_loop(0, tt // CONV_ROWS, body, 0)


def conv_prompt(a, w, cb, lg, lb, tt):
    b, t, _ = a.shape
    hb = tt // CONV_HALO
    vec = pl.BlockSpec((1, D_CONV), lambda i, j: (0, 0))
    return pl.pallas_call(
        _conv_prompt_kernel,
        grid=(b, t // tt),
        in_specs=[
            pl.BlockSpec((None, tt, D_CONV), lambda i, j: (i, j, 0)),
            pl.BlockSpec((None, CONV_HALO, D_CONV), lambda i, j: (i, jnp.maximum(j * hb - 1, 0), 0)),
            pl.BlockSpec((CONV_W, 1, D_CONV), lambda i, j: (0, 0, 0)),
            vec, vec, vec,
        ],
        out_specs=pl.BlockSpec((None, tt, D_CONV), lambda i, j: (i, j, 0)),
        out_shape=jax.ShapeDtypeStruct((b, t, D_CONV), BF16),
        scratch_shapes=[pltpu.VMEM((SUBLANES, CONV_HALO + tt, D_CONV), F32)],
        compiler_params=_params(("parallel", "arbitrary"), 40),
        name="conv_prompt",
    )(a, a, w, cb, lg, lb)


def _conv_sample_kernel(apad_ref, w_ref, cb_ref, lg_ref, lb_ref, o_ref):
    rows = o_ref.shape[0]
    acc = jnp.zeros((rows, D_CONV), F32) + cb_ref[...]
    for k in range(CONV_W):
        acc = acc + apad_ref[k:k + rows, :] * w_ref[k]
    o_ref[...] = _ln_silu(acc, lg_ref, lb_ref).astype(o_ref.dtype)


def conv_sample(apad, w, cb, lg, lb):
    b, tp, _ = apad.shape
    t = tp - (CONV_W - 1)
    vec = pl.BlockSpec((1, D_CONV), lambda i: (0, 0))
    return pl.pallas_call(
        _conv_sample_kernel,
        grid=(b,),
        in_specs=[
            pl.BlockSpec((None, tp, D_CONV), lambda i: (i, 0, 0)),
            pl.BlockSpec((CONV_W, 1, D_CONV), lambda i: (0, 0, 0)),
            vec, vec, vec,
        ],
        out_specs=pl.BlockSpec((None, t, D_CONV), lambda i: (i, 0, 0)),
        out_shape=jax.ShapeDtypeStruct((b, t, D_CONV), BF16),
        compiler_params=_params(("parallel",), 32),
        name="conv_sample",
    )(apad, w, cb, lg, lb)


def _dil_prompt_kernel(*refs):
    qkv_refs = refs[:3 * N_DGROUPS]
    b_ref, qbuf, kbuf, vbuf, o_sc, l_sc = refs[3 * N_DGROUPS:]
    t = b_ref.shape[0]
    n = kbuf.shape[0] - qbuf.shape[0]
    kbuf[0:n, :] = jnp.zeros((n, HEAD_DIM), BF16)
    vbuf[0:n, :] = jnp.zeros((n, HEAD_DIM), BF16)
    qi = lax.broadcasted_iota(jnp.int32, (n, 2 * n), 0)
    kj = lax.broadcasted_iota(jnp.int32, (n, 2 * n), 1)
    band = (kj > qi) & (kj <= qi + n)
    causal = lax.broadcasted_iota(jnp.int32, (n, n), 1) <= lax.broadcasted_iota(jnp.int32, (n, n), 0)

    for gi, (win, dil) in enumerate(DIL_GROUPS):
        assert win // dil == n
        q_ref, k_ref, v_ref = qkv_refs[3 * gi:3 * gi + 3]
        s_len = t // dil
        n_blk = s_len // n

        def residue(r, carry, q_ref=q_ref, k_ref=k_ref, v_ref=v_ref, dil=dil, s_len=s_len, n_blk=n_blk, gi=gi):
            rows = pl.ds(r, s_len, stride=dil) if dil > 1 else pl.ds(0, s_len)
            qbuf[0:s_len, :] = q_ref[rows, :].astype(BF16)
            kbuf[n:n + s_len, :] = k_ref[rows, :].astype(BF16)
            vbuf[n:n + s_len, :] = v_ref[rows, :].astype(BF16)

            def block(blk, c2):
                b0 = pl.multiple_of(blk * n, n)
                q = qbuf[pl.ds(b0, n), :]
                if n_blk == 1:
                    s = jnp.where(causal, _dot_nt(q, kbuf[n:2 * n, :]), NEG)
                    o, lse = _softmax_pv(s, vbuf[n:2 * n, :])
                else:
                    mask = band & (kj >= jnp.where(blk == 0, n, 0))
                    s = jnp.where(mask, _dot_nt(q, kbuf[pl.ds(b0, 2 * n), :]), NEG)
                    o, lse = _softmax_pv(s, vbuf[pl.ds(b0, 2 * n), :])
                start = r + b0 * dil
                dst = pl.ds(start, n, stride=dil) if dil > 1 else pl.ds(pl.multiple_of(start, n), n)
                o_sc[gi, dst, :] = o
                l_sc[gi, dst, :] = jnp.broadcast_to(lse, (n, HEAD_DIM))
                return c2

            lax.fori_loop(0, n_blk, block, 0)
            return carry

        lax.fori_loop(0, dil, residue, 0)

    def merge(c, carry):
        rows = pl.ds(pl.multiple_of(c * MERGE_ROWS, MERGE_ROWS), MERGE_ROWS)
        outs = [o_sc[gi, rows, :] for gi in range(N_DGROUPS)]
        lses = [l_sc[gi, rows, :] for gi in range(N_DGROUPS)]
        b_ref[rows, :] = _merge3(outs, lses).astype(b_ref.dtype)
        return carry

    lax.fori_loop(0, t // MERGE_ROWS, merge, 0)


def dilated_attn_prompt(qkv, batch, seq):
    n = DIL_GROUPS[0][0] // DIL_GROUPS[0][1]
    s_max = seq // min(d for _, d in DIL_GROUPS)
    assert seq % (n * max(d for _, d in DIL_GROUPS)) == 0 and seq % MERGE_ROWS == 0

    def spec(which, gi):
        return pl.BlockSpec((seq, HEAD_DIM), lambda b, h: (b, (which * N_DGROUPS + gi) * HEADS + h))

    in_specs = [spec(which, gi) for gi in range(N_DGROUPS) for which in range(3)]
    return pl.pallas_call(
        _dil_prompt_kernel,
        grid=(batch, HEADS),
        in_specs=in_specs,
        out_specs=pl.BlockSpec((seq, HEAD_DIM), lambda b, h: (b, h)),
        out_shape=jax.ShapeDtypeStruct((batch * seq, D_GRP), BF16),
        scratch_shapes=[pltpu.VMEM((s_max, HEAD_DIM), BF16), pltpu.VMEM((n + s_max, HEAD_DIM), BF16),
                        pltpu.VMEM((n + s_max, HEAD_DIM), BF16),
                        pltpu.VMEM((N_DGROUPS, seq, HEAD_DIM), F32), pltpu.VMEM((N_DGROUPS, seq, HEAD_DIM), F32)],
        compiler_params=_params(("parallel", "parallel"), 40),
        name="dil_attn_prompt",
    )(*([qkv] * (3 * N_DGROUPS)))


def _dil_sample_kernel(q_ref, kn_ref, vn_ref, *rest):
    cache_refs, b_ref = rest[:2 * N_DGROUPS], rest[2 * N_DGROUPS]
    ds = q_ref.shape[0]
    qn = lax.broadcasted_iota(jnp.int32, (ds, ds), 0)
    pn = lax.broadcasted_iota(jnp.int32, (ds, ds), 1)
    for h in range(HEADS):
        outs, lses = [], []
        for gi, (win, dil) in enumerate(DIL_GROUPS):
            n = win // dil
            kc_ref, vc_ref = cache_refs[2 * gi], cache_refs[2 * gi + 1]
            cache_len = kc_ref.shape[0] // HEADS
            cols = slice(gi * D_GRP + h * HEAD_DIM, gi * D_GRP + (h + 1) * HEAD_DIM)
            qi = lax.broadcasted_iota(jnp.int32, (ds, cache_len), 0)
            pj = lax.broadcasted_iota(jnp.int32, (ds, cache_len), 1)
            dist = cache_len + qi - pj
            mask_c = (dist % dil == 0) & (dist <= dil * (n - 1))
            mask_n = (qn >= pn) & ((qn - pn) % dil == 0)
            q = q_ref[:, cols].astype(BF16)
            s_c = jnp.where(mask_c, _dot_nt(q, _head_rows(kc_ref, h, cache_len).astype(BF16)), NEG)
            s_n = jnp.where(mask_n, _dot_nt(q, kn_ref[:, cols].astype(BF16)), NEG)
            mx = jnp.maximum(jnp.max(s_c, axis=-1, keepdims=True), jnp.max(s_n, axis=-1, keepdims=True))
            p_c = jnp.exp(s_c - mx)
            p_n = jnp.exp(s_n - mx)
            l = jnp.sum(p_c, axis=-1, keepdims=True) + jnp.sum(p_n, axis=-1, keepdims=True)
            o = (_dot(p_c.astype(BF16), _head_rows(vc_ref, h, cache_len).astype(BF16))
                 + _dot(p_n.astype(BF16), vn_ref[:, cols].astype(BF16)))
            outs.append(o / l)
            lses.append(mx + jnp.log(l))
        b_ref[:, h * HEAD_DIM:(h + 1) * HEAD_DIM] = _merge3(outs, lses).astype(b_ref.dtype)


def dilated_attn_sample(qkv, caches, layer):
    b, ds, _ = qkv.shape
    cache_specs = []
    for gi, (win, dil) in enumerate(DIL_GROUPS):
        rows = caches[2 * gi].shape[2]
        assert rows // HEADS - dil * (win // dil - 1) >= 0
        cache_specs += [pl.BlockSpec((None, None, rows, HEAD_DIM), lambda i: (layer, i, 0, 0))] * 2

    new = lambda which: pl.BlockSpec((None, ds, D_ATT), lambda i: (i, 0, which))
    return pl.pallas_call(
        _dil_sample_kernel,
        grid=(b,),
        in_specs=[new(0), new(1), new(2)] + cache_specs,
        out_specs=pl.BlockSpec((None, ds, D_GRP), lambda i: (i, 0, 0)),
        out_shape=jax.ShapeDtypeStruct((b, ds, D_GRP), BF16),
        compiler_params=_params(("parallel",), 48),
        name="dil_attn_sample",
    )(qkv, qkv, qkv, *caches)


def _mix_out_kernel(x_ref, a_ref, b_ref, wa_ref, wb_ref, y_ref):
    y_ref[...] = x_ref[...] + _dot(a_ref[...], wa_ref[...]) + _dot(b_ref[...], wb_ref[...])


def mix_out(x, a, b, w_out, layer, tm):
    m = x.shape[0]
    row = lambda width: pl.BlockSpec((tm, width), lambda i: (i, 0))
    return pl.pallas_call(
        _mix_out_kernel,
        grid=(m // tm,),
        in_specs=[row(D_MODEL), row(D_CONV), row(D_GRP),
                  pl.BlockSpec((None, D_CONV, D_MODEL), lambda i: (layer, 0, 0)),
                  pl.BlockSpec((None, D_GRP, D_MODEL), lambda i: (layer, D_CONV // D_GRP, 0))],
        out_specs=row(D_MODEL),
        out_shape=jax.ShapeDtypeStruct((m, D_MODEL), F32),
        compiler_params=_params(("parallel",), 48),
        name="mix_out",
    )(x, a, b, w_out, w_out)


def _sgu_kernel(x_ref, u_ref, gv_ref, lg_ref, lb_ref, ws_ref, bs_ref, w_ref, *out_refs, emit_v):
    y_ref = out_refs[0]
    gv = gv_ref[...]
    mu = jnp.mean(gv, axis=-1, keepdims=True)
    vc = gv - mu
    v = vc * lax.rsqrt(jnp.mean(vc * vc, axis=-1, keepdims=True) + EPS) * lg_ref[...] + lb_ref[...]
    if emit_v:
        out_refs[1][...] = v
    vb = v.astype(BF16)
    rows, cm = x_ref.shape[0], ws_ref.shape[1]
    gated = []
    for g in range(N_SG):
        cols = slice(g * D_SG, (g + 1) * D_SG)
        sv = jnp.concatenate([_dot(ws_ref[g], vb[c * cm:(c + 1) * cm, cols]) + bs_ref[g]
                              for c in range(rows // cm)], axis=0)
        gated.append((u_ref[:, cols] * sv).astype(BF16))
    y_ref[...] = x_ref[...] + _dot(jnp.concatenate(gated, axis=-1), w_ref[...])


def sgu_out(x, z, lg, lb, ws, bs, w_out, layer, tm, emit_v):
    m = x.shape[0]
    cm = ws.shape[1]
    assert tm % cm == 0
    row = lambda jblk: pl.BlockSpec((tm, D_GATE), lambda i: (i, jblk))
    vec = pl.BlockSpec((1, D_GATE), lambda i: (0, 0))
    out_specs = [row(0)]
    out_shape = [jax.ShapeDtypeStruct((m, D_MODEL), F32)]
    if emit_v:
        out_specs.append(row(0))
        out_shape.append(jax.ShapeDtypeStruct((m, D_GATE), F32))
    res = pl.pallas_call(
        functools.partial(_sgu_kernel, emit_v=emit_v),
        grid=(m // tm,),
        in_specs=[row(0), row(0), row(1), vec, vec,
                  pl.BlockSpec((N_SG, cm, cm), lambda i: (0, 0, 0)),
                  pl.BlockSpec((N_SG, cm, D_SG), lambda i: (0, 0, 0)),
                  pl.BlockSpec((None, D_GATE, D_MODEL), lambda i: (layer, 0, 0))],
        out_specs=out_specs,
        out_shape=out_shape,
        compiler_params=_params(("parallel",), 52),
        name="sgu_out",
    )(x, z, z, lg, lb, ws, bs, w_out)
    return res if emit_v else (res[0], None)


def _mem_attn_kernel(x_ref, g_ref, wq_ref, qg_ref, k_ref, v_ref, wo_ref, y_ref):
    x = x_ref[...]
    h = (_rms(x) * g_ref[...]).astype(BF16)
    q = _head_norm(_dot(h, wq_ref[...]), qg_ref[...]).astype(BF16)
    n_mem = k_ref.shape[0] // HEADS
    outs = []
    for hd in range(HEADS):
        cols = slice(hd * HEAD_DIM, (hd + 1) * HEAD_DIM)
        s = _dot_nt(q[:, cols], _head_rows(k_ref, hd, n_mem).astype(BF16))
        o, _ = _softmax_pv(s, _head_rows(v_ref, hd, n_mem).astype(BF16))
        outs.append(o.astype(BF16))
    y_ref[...] = x + _dot(jnp.concatenate(outs, axis=-1), wo_ref[...])


def mem_attn(x, g, wq, q_gain, k, v, kv_layer, wo, layer, tm):
    b, t, _ = x.shape
    full = lambda shape: pl.BlockSpec(shape, lambda i, j: (0,) * len(shape))
    kv = pl.BlockSpec((None, None, k.shape[2], HEAD_DIM), lambda i, j: (kv_layer, i, 0, 0))
    xs = pl.BlockSpec((None, tm, D_MODEL), lambda i, j: (i, j, 0))
    return pl.pallas_call(
        _mem_attn_kernel,
        grid=(b, t // tm),
        in_specs=[xs, full((1, D_MODEL)),
                  pl.BlockSpec((None, D_MODEL, D_MEMATT), lambda i, j: (layer, 0, 0)),
                  full((1, D_MEMATT)), kv, kv,
                  pl.BlockSpec((None, D_MEMATT, D_MODEL), lambda i, j: (layer, 0, 0))],
        out_specs=xs,
        out_shape=jax.ShapeDtypeStruct((b, t, D_MODEL), F32),
        compiler_params=_params(("parallel", "parallel"), 40),
        name="mem_attn",
    )(x, g, wq, q_gain, k, v, wo)


def _tile_heads(g):
    return jnp.tile(g, HEADS)


def _row_tile(m):
    return min(m, 1024)


def kernel(x_prompt, x_sample, mem_prompt, state_conv, cache_k_w128, cache_v_w128, cache_k_w512, cache_v_w512,
           cache_k_w2048, cache_v_w2048, cache_mem_k, cache_mem_v, g_mix, w_in_e, conv_w, conv_b, conv_ln_g,
           conv_ln_b, q_norm_e, k_norm_e, w_out_e, w_in_o, b_in_o, v_ln_g, v_ln_b, w_s, b_s, w_out_o, g_xmem,
           g_mem, wq_mem, wk_mem, wv_mem, q_norm_mem, k_norm_mem, wo_mem, g_ffn, w_ffn1, w_ffn2):
    depth = g_mix.shape[0]
    bp, tp, _ = x_prompt.shape
    bs, ts, _ = x_sample.shape
    mp, ms = bp * tp, bs * ts
    scale = HEAD_DIM ** -0.5
    row = lambda v: v.reshape(1, -1)
    th_rows = lambda c: c.reshape(c.shape[0], c.shape[1], c.shape[2] * HEADS, HEAD_DIM)
    caches = [th_rows(c) for c in (cache_k_w128, cache_v_w128, cache_k_w512, cache_v_w512,
                                   cache_k_w2048, cache_v_w2048)]
    mem_k_s, mem_v_s = th_rows(cache_mem_k), th_rows(cache_mem_v)

    w_in_e, w_out_e, w_in_o, w_out_o = (w.astype(BF16) for w in (w_in_e, w_out_e, w_in_o, w_out_o))
    wq_mem, wo_mem, w_ffn1, w_ffn2 = (w.astype(BF16) for w in (wq_mem, wo_mem, w_ffn1, w_ffn2))
    wkv_mem = jnp.concatenate([wk_mem, wv_mem], axis=2).astype(BF16)

    xp = x_prompt.reshape(mp, D_MODEL)
    xs = x_sample.reshape(ms, D_MODEL)
    mem = mem_prompt.reshape(bp * N_MEM, D_MODEL)
    tm_p, tm_s, tm_mem = _row_tile(mp), _row_tile(ms), _row_tile(bp * N_MEM)
    assert tp % tm_p == 0 or tm_p % tp == 0

    conv_pl, conv_sl, kv_pl, kv_sl, memk_pl, memv_pl, chunk_sl = [], [], [], [], [], [], []
    for i in range(depth):
        j = i // 2
        if i % 2 == 0:
            gains = jnp.concatenate([
                jnp.stack([_tile_heads(q_norm_e[j, gi]) * scale for gi in range(N_DGROUPS)]),
                jnp.stack([_tile_heads(k_norm_e[j, gi]) for gi in range(N_DGROUPS)]),
                jnp.ones((N_DGROUPS, D_GRP), F32)])[:, None, :]
            conv_args = (conv_w[j][:, None, :], row(conv_b[j]), row(conv_ln_g[j]), row(conv_ln_b[j]))
            qkv_col0 = 2 * D_CONV // D_GRP

            def kv_tiles(seq, tm):
                tiles = []
                for gi, (win, _) in enumerate(DIL_GROUPS):
                    keep = min(win, seq)
                    assert keep == seq or (keep <= tm and seq % tm == 0)
                    spec = (tm, 1) if keep == seq else (keep, seq // tm)
                    tiles += [((1 + which) * N_DGROUPS + gi,) + spec for which in range(2)]
                return tiles

            a = glu_proj(xp, row(g_mix[i]), w_in_e, j, tm_p)
            qkv, *new_kv = headnorm_proj(xp, row(g_mix[i]), w_in_e, j, qkv_col0, gains, 2 * N_DGROUPS, tm_p,
                                         kv_tiles(tp, tm_p), True)
            kv_pl.append([kv.reshape(bp, -1, HEADS, HEAD_DIM) for kv in new_kv])
            a3 = a.reshape(bp, tp, D_CONV)
            conv_pl.append(a3[:, tp - (CONV_W - 1):])
            a_out = conv_prompt(a3, *conv_args, tt=512).reshape(mp, D_CONV)
            b_out = dilated_attn_prompt(qkv, bp, tp)
            xp = mix_out(xp, a_out, b_out, w_out_e, j, 512)

            a = glu_proj(xs, row(g_mix[i]), w_in_e, j, tm_s)
            qkv, *new_kv = headnorm_proj(xs, row(g_mix[i]), w_in_e, j, qkv_col0, gains, 2 * N_DGROUPS, tm_s,
                                         kv_tiles(ts, tm_s), True)
            kv_sl.append([kv.reshape(bs, -1, HEADS, HEAD_DIM) for kv in new_kv])
            apad = jnp.concatenate([state_conv[j], a.reshape(bs, ts, D_CONV)], axis=1)
            conv_sl.append(apad[:, apad.shape[1] - (CONV_W - 1):])
            a_out = conv_sample(apad, *conv_args).reshape(ms, D_CONV)
            b_out = dilated_attn_sample(qkv.reshape(bs, ts, 3 * D_ATT), caches, j).reshape(ms, D_GRP)
            xs = mix_out(xs, a_out, b_out, w_out_e, j, tm_s)
        else:
            tril = jnp.tril(jnp.ones((CHUNK, CHUNK), F32))
            for which in range(2):
                x, t = (xp, tp) if which == 0 else (xs, ts)
                m = x.shape[0]
                c = min(CHUNK, t)
                ws = w_s[j][:, :c, :c] * tril[:c, :c]
                bias = b_s[j][:, :c]
                tm = 256
                if c < CHUNK:
                    ws = jnp.einsum('ab,gts->gatbs', jnp.eye(tm // c, dtype=F32), ws).reshape(N_SG, tm, tm)
                    bias = jnp.tile(bias, (1, tm // c))
                bias = jnp.broadcast_to(bias[:, :, None], bias.shape + (D_SG,))
                z = gelu_proj(x, row(g_mix[i]), w_in_o, j, row(b_in_o[j]), _row_tile(m))
                y, v = sgu_out(x, z, row(v_ln_g[j]), row(v_ln_b[j]), ws.astype(BF16), bias, w_out_o, j, tm,
                               emit_v=which == 1)
                if which == 0:
                    xp = y
                else:
                    xs = y
                    chunk_sl.append(v.reshape(bs, ts, D_GATE))

        kgain = jnp.stack([_tile_heads(k_norm_mem[i]), jnp.ones((D_MEMATT,), F32)])[:, None, :]
        mk, mv = headnorm_proj(mem, row(g_mem[i]), wkv_mem, i, 0, kgain, 1, tm_mem,
                               [(0, tm_mem, 1), (1, tm_mem, 1)], False)
        memk_pl.append(mk.reshape(bp, N_MEM, HEADS, HEAD_DIM))
        memv_pl.append(mv.reshape(bp, N_MEM, HEADS, HEAD_DIM))
        mem_k_p = mk.reshape(1, bp, N_MEM * HEADS, HEAD_DIM)
        mem_v_p = mv.reshape(1, bp, N_MEM * HEADS, HEAD_DIM)
        qgain = row(_tile_heads(q_norm_mem[i]) * scale)
        xp = mem_attn(xp.reshape(bp, tp, D_MODEL), row(g_xmem[i]), wq_mem, qgain, mem_k_p, mem_v_p, 0,
                      wo_mem, i, 512).reshape(mp, D_MODEL)
        xs = mem_attn(xs.reshape(bs, ts, D_MODEL), row(g_xmem[i]), wq_mem, qgain, mem_k_s, mem_v_s, i,
                      wo_mem, i, ts).reshape(ms, D_MODEL)

        xp = ffn(xp, row(g_ffn[i]), w_ffn1, w_ffn2, i, tm_p, 512)
        xs = ffn(xs, row(g_ffn[i]), w_ffn1, w_ffn2, i, tm_s, 512)

    stack = lambda items: jnp.stack(items)
    kv_p = [stack([kv[n] for kv in kv_pl]) for n in range(2 * N_DGROUPS)]
    kv_s = [stack([kv[n] for kv in kv_sl]) for n in range(2 * N_DGROUPS)]
    return (xp.reshape(bp, tp, D_MODEL), xs.reshape(bs, ts, D_MODEL), stack(conv_pl), stack(conv_sl),
            *kv_p, *kv_s, stack(memk_pl), stack(memv_pl), stack(chunk_sl))
```

```python
import functools
import math

import numpy as np
import jax
import jax.numpy as jnp
from jax import lax
from jax.experimental import pallas as pl
from jax.experimental.pallas import tpu as pltpu

D_MODEL = 2048
EPS = 1e-6
NEG = -1e30
D_CONV = D_MODEL // 2
CONV_W = 31
HEAD_DIM = 128
DIL_GROUPS = ((128, 1), (512, 4), (2048, 16))
N_DGROUPS = len(DIL_GROUPS)
HEADS = 4
D_GRP = HEADS * HEAD_DIM
D_ATT = N_DGROUPS * D_GRP
CHUNK = 128
D_GATE = D_MODEL
N_SG = 8
D_SG = D_GATE // N_SG
N_MEM = 256
D_MEMATT = HEADS * HEAD_DIM
D_FF = 4 * D_MODEL

F32 = jnp.float32
BF16 = jnp.bfloat16
MIB = 1024 * 1024
NORM_ROWS = 256
CONV_HALO = 32
SUBLANES = 8
MERGE_ROWS = 256
BAND = DIL_GROUPS[0][0] // DIL_GROUPS[0][1]
ATTN_UNROLL = 8
CONV_ROWS = 16
LN_ROWS = 128
LANES = 128


def _params(sem, vmem_mib):
    return pltpu.CompilerParams(dimension_semantics=sem, vmem_limit_bytes=vmem_mib * MIB)


def _dot(a, b):
    return jnp.dot(a, b, preferred_element_type=F32)


def _dot_nt(a, b):
    return lax.dot_general(a, b, (((1,), (1,)), ((), ())), preferred_element_type=F32)


def _rms(x):
    return x * lax.rsqrt(jnp.mean(x * x, axis=-1, keepdims=True) + EPS)


def _norm_to_scratch(x_ref, g_ref, h_ref):
    rows = x_ref.shape[0]
    step = min(NORM_ROWS, rows)

    def body(c, carry):
        r = pl.multiple_of(c * step, step)
        x = x_ref[pl.ds(r, step), :]
        h_ref[pl.ds(r, step), :] = (_rms(x) * g_ref[...]).astype(BF16)
        return carry

    lax.fori_loop(0, rows // step, body, 0)


def _head_norm(acc, gain):
    parts = [_rms(acc[:, h * HEAD_DIM:(h + 1) * HEAD_DIM]) for h in range(acc.shape[1] // HEAD_DIM)]
    return jnp.concatenate(parts, axis=-1) * gain


def _head_rows(ref, h, rows):
    return ref[pl.ds(h, rows, stride=HEADS), :]


def _softmax_pv(s, v):
    mx = jnp.max(s, axis=-1, keepdims=True)
    p = jnp.exp(s - mx)
    l = jnp.sum(p, axis=-1, keepdims=True)
    return _dot(p.astype(BF16), v) / l, mx + jnp.log(l)


def _merge3(outs, lses):
    mx = jnp.maximum(jnp.maximum(lses[0], lses[1]), lses[2])
    es = [jnp.exp(l - mx) for l in lses]
    return (es[0] * outs[0] + es[1] * outs[1] + es[2] * outs[2]) / (es[0] + es[1] + es[2])


def _glu_kernel(x_ref, g_ref, wv_ref, wg_ref, o_ref, h_ref):
    @pl.when(pl.program_id(1) == 0)
    def _():
        _norm_to_scratch(x_ref, g_ref, h_ref)

    h = h_ref[...]
    val = _dot(h, wv_ref[...])
    gate = _dot(h, wg_ref[...])
    o_ref[...] = val * jax.nn.sigmoid(gate)


def glu_proj(x, g, w_in, layer, tm):
    m = x.shape[0]
    tn = 512
    nj = D_CONV // tn
    return pl.pallas_call(
        _glu_kernel,
        grid=(m // tm, nj),
        in_specs=[
            pl.BlockSpec((tm, D_MODEL), lambda i, j: (i, 0)),
            pl.BlockSpec((1, D_MODEL), lambda i, j: (0, 0)),
            pl.BlockSpec((None, D_MODEL, tn), lambda i, j: (layer, 0, j)),
            pl.BlockSpec((None, D_MODEL, tn), lambda i, j: (layer, 0, j + nj)),
        ],
        out_specs=pl.BlockSpec((tm, tn), lambda i, j: (i, j)),
        out_shape=jax.ShapeDtypeStruct((m, D_CONV), F32),
        scratch_shapes=[pltpu.VMEM((tm, D_MODEL), BF16)],
        compiler_params=_params(("parallel", "arbitrary"), 48),
        name="glu_proj",
    )(x, g, w_in, w_in)


def _headnorm_proj_kernel(x_ref, g_ref, w_ref, gain_ref, *rest, n_normed, emit_main, th_tiles):
    n_th = len(th_tiles)
    if emit_main:
        o_ref, th_refs, h_ref = rest[0], rest[1:1 + n_th], rest[1 + n_th]
    else:
        th_refs, h_ref, o_ref = rest[:n_th], rest[n_th], rest[n_th + 1]
    i, j = pl.program_id(0), pl.program_id(1)
    tm = x_ref.shape[0]

    @pl.when(j == 0)
    def _():
        _norm_to_scratch(x_ref, g_ref, h_ref)

    acc = _dot(h_ref[...], w_ref[...])
    o_ref[...] = jnp.where(j < n_normed, _head_norm(acc, gain_ref[0]), acc)

    for th_ref, (col, period) in zip(th_refs, th_tiles):
        rows = th_ref.shape[0] // HEADS

        @pl.when((j == col) & (i % period == period - 1))
        def _():
            for h in range(HEADS):
                th_ref[pl.ds(h, rows, stride=HEADS), :] = o_ref[tm - rows:, h * HEAD_DIM:(h + 1) * HEAD_DIM]


def _th_index(period, i, j):
    return (i // period, 0)


def headnorm_proj(x, g, w, layer, col_block0, gains, n_normed, tm, th_tiles, emit_main):
    m = x.shape[0]
    nj = gains.shape[0]
    out_specs, out_shape = [], []
    if emit_main:
        out_specs.append(pl.BlockSpec((tm, D_GRP), lambda i, j: (i, j)))
        out_shape.append(jax.ShapeDtypeStruct((m, nj * D_GRP), F32))
    for _, keep, period in th_tiles:
        assert keep <= tm and (m // tm) % period == 0
        out_specs.append(pl.BlockSpec((keep * HEADS, HEAD_DIM), functools.partial(_th_index, period)))
        out_shape.append(jax.ShapeDtypeStruct((m // tm // period * keep * HEADS, HEAD_DIM), F32))
    scratch = [pltpu.VMEM((tm, D_MODEL), BF16)]
    if not emit_main:
        scratch.append(pltpu.VMEM((tm, D_GRP), F32))
    return pl.pallas_call(
        functools.partial(_headnorm_proj_kernel, n_normed=n_normed, emit_main=emit_main,
                          th_tiles=tuple((col, period) for col, _, period in th_tiles)),
        grid=(m // tm, nj),
        in_specs=[
            pl.BlockSpec((tm, D_MODEL), lambda i, j: (i, 0)),
            pl.BlockSpec((1, D_MODEL), lambda i, j: (0, 0)),
            pl.BlockSpec((None, D_MODEL, D_GRP), lambda i, j: (layer, 0, j + col_block0)),
            pl.BlockSpec((1, 1, D_GRP), lambda i, j: (j, 0, 0)),
        ],
        out_specs=out_specs,
        out_shape=out_shape,
        scratch_shapes=scratch,
        compiler_params=_params(("arbitrary", "arbitrary"), 52),
        name="headnorm_proj",
    )(x, g, w, gains)


def _gelu_proj_kernel(x_ref, g_ref, w_ref, b_ref, o_ref, h_ref):
    @pl.when(pl.program_id(1) == 0)
    def _():
        _norm_to_scratch(x_ref, g_ref, h_ref)

    z = _dot(h_ref[...], w_ref[...]) + b_ref[...]
    o_ref[...] = 0.5 * z * (1.0 + lax.erf(z * np.float32(math.sqrt(0.5))))


def gelu_proj(x, g, w, layer, b, tm):
    m = x.shape[0]
    n = w.shape[2]
    tn = 512
    return pl.pallas_call(
        _gelu_proj_kernel,
        grid=(m // tm, n // tn),
        in_specs=[
            pl.BlockSpec((tm, D_MODEL), lambda i, j: (i, 0)),
            pl.BlockSpec((1, D_MODEL), lambda i, j: (0, 0)),
            pl.BlockSpec((None, D_MODEL, tn), lambda i, j: (layer, 0, j)),
            pl.BlockSpec((1, tn), lambda i, j: (0, j)),
        ],
        out_specs=pl.BlockSpec((tm, tn), lambda i, j: (i, j)),
        out_shape=jax.ShapeDtypeStruct((m, n), F32),
        scratch_shapes=[pltpu.VMEM((tm, D_MODEL), BF16)],
        compiler_params=_params(("parallel", "arbitrary"), 48),
        name="gelu_proj",
    )(x, g, w, b)


def _ffn_kernel(x_ref, g_ref, w1_ref, w2_ref, o_ref, h_ref):
    @pl.when(pl.program_id(1) == 0)
    def _():
        _norm_to_scratch(x_ref, g_ref, h_ref)
        o_ref[...] = x_ref[...]

    hid = jnp.maximum(_dot(h_ref[...], w1_ref[...]), 0.0)
    o_ref[...] += _dot((hid * hid).astype(BF16), w2_ref[...])


def ffn(x, g, w1, w2, layer, tm, tf):
    m = x.shape[0]
    return pl.pallas_call(
        _ffn_kernel,
        grid=(m // tm, D_FF // tf),
        in_specs=[
            pl.BlockSpec((tm, D_MODEL), lambda i, f: (i, 0)),
            pl.BlockSpec((1, D_MODEL), lambda i, f: (0, 0)),
            pl.BlockSpec((None, D_MODEL, tf), lambda i, f: (layer, 0, f)),
            pl.BlockSpec((None, tf, D_MODEL), lambda i, f: (layer, f, 0)),
        ],
        out_specs=pl.BlockSpec((tm, D_MODEL), lambda i, f: (i, 0)),
        out_shape=jax.ShapeDtypeStruct((m, D_MODEL), F32),
        scratch_shapes=[pltpu.VMEM((tm, D_MODEL), BF16)],
        compiler_params=_params(("parallel", "arbitrary"), 56),
        name="ffn",
    )(x, g, w1, w2)


def _ln_silu(acc, lg_ref, lb_ref):
    mu = jnp.mean(acc, axis=-1, keepdims=True)
    xc = acc - mu
    y = xc * lax.rsqrt(jnp.mean(xc * xc, axis=-1, keepdims=True) + EPS)
    y = y * lg_ref[...] + lb_ref[...]
    return y * jax.nn.sigmoid(y)


def _conv_prompt_kernel(a_ref, halo_ref, w_ref, cb_ref, lg_ref, lb_ref, o_ref, sh_ref, y_ref):
    tt = a_ref.shape[0]
    first = pl.program_id(1) == 0
    sh_ref[0, 0:CONV_HALO, :] = jnp.where(first, 0.0, halo_ref[...])
    sh_ref[0, CONV_HALO:, :] = a_ref[...]
    span = tt + CONV_HALO - SUBLANES
    for s in range(1, SUBLANES):
        sh_ref[s, 0:span, :] = sh_ref[0, s:s + span, :]
    lead = CONV_HALO - (CONV_W - 1)
    groups = CONV_ROWS // SUBLANES

    def conv_rows(c, carry):
        r = pl.multiple_of(c * CONV_ROWS, CONV_ROWS)
        acc = jnp.zeros((groups, SUBLANES, D_CONV), F32) + cb_ref[...]
        for k in range(CONV_W):
            q, s = divmod(k + lead, SUBLANES)
            x = sh_ref[s, pl.ds(r + q * SUBLANES, CONV_ROWS), :]
            acc = acc + x.reshape(groups, SUBLANES, D_CONV) * w_ref[k]
        y_ref[pl.ds(r, CONV_ROWS), :] = acc.reshape(CONV_ROWS, D_CONV)
        return carry

    lax.fori_loop(0, tt // CONV_ROWS, conv_rows, 0)

    def norm_rows(c, carry):
        r = pl.multiple_of(c * LN_ROWS, LN_ROWS)
        o_ref[pl.ds(r, LN_ROWS), :] = _ln_silu(y_ref[pl.ds(r, LN_ROWS), :], lg_ref, lb_ref).astype(o_ref.dtype)
        return carry

    lax.fori_loop(0, tt // LN_ROWS, norm_rows, 0)


def conv_prompt(a, w, cb, lg, lb, tt):
    b, t, _ = a.shape
    hb = tt // CONV_HALO
    vec = pl.BlockSpec((1, D_CONV), lambda i, j: (0, 0))
    return pl.pallas_call(
        _conv_prompt_kernel,
        grid=(b, t // tt),
        in_specs=[
            pl.BlockSpec((None, tt, D_CONV), lambda i, j: (i, j, 0)),
            pl.BlockSpec((None, CONV_HALO, D_CONV), lambda i, j: (i, jnp.maximum(j * hb - 1, 0), 0)),
            pl.BlockSpec((CONV_W, SUBLANES, D_CONV), lambda i, j: (0, 0, 0)),
            vec, vec, vec,
        ],
        out_specs=pl.BlockSpec((None, tt, D_CONV), lambda i, j: (i, j, 0)),
        out_shape=jax.ShapeDtypeStruct((b, t, D_CONV), BF16),
        scratch_shapes=[pltpu.VMEM((SUBLANES, CONV_HALO + tt, D_CONV), F32), pltpu.VMEM((tt, D_CONV), F32)],
        compiler_params=_params(("parallel", "arbitrary"), 40),
        name="conv_prompt",
    )(a, a, w, cb, lg, lb)


def _conv_sample_kernel(apad_ref, w_ref, cb_ref, lg_ref, lb_ref, o_ref):
    rows = o_ref.shape[0]
    acc = jnp.zeros((rows, D_CONV), F32) + cb_ref[...]
    for k in range(CONV_W):
        acc = acc + apad_ref[k:k + rows, :] * w_ref[k]
    o_ref[...] = _ln_silu(acc, lg_ref, lb_ref).astype(o_ref.dtype)


def conv_sample(apad, w, cb, lg, lb):
    b, tp, _ = apad.shape
    t = tp - (CONV_W - 1)
    vec = pl.BlockSpec((1, D_CONV), lambda i: (0, 0))
    return pl.pallas_call(
        _conv_sample_kernel,
        grid=(b,),
        in_specs=[
            pl.BlockSpec((None, tp, D_CONV), lambda i: (i, 0, 0)),
            pl.BlockSpec((CONV_W, SUBLANES, D_CONV), lambda i: (0, 0, 0)),
            vec, vec, vec,
        ],
        out_specs=pl.BlockSpec((None, t, D_CONV), lambda i: (i, 0, 0)),
        out_shape=jax.ShapeDtypeStruct((b, t, D_CONV), BF16),
        compiler_params=_params(("parallel",), 32),
        name="conv_sample",
    )(apad, w, cb, lg, lb)


def _dil_prompt_kernel(*refs):
    qkv_refs = refs[:3 * N_DGROUPS]
    b_ref, qbuf, kbuf, vbuf, o_sc, l_sc = refs[3 * N_DGROUPS:]
    t = b_ref.shape[0]
    n = BAND
    qi = lax.broadcasted_iota(jnp.int32, (n, 2 * n), 0)
    kj = lax.broadcasted_iota(jnp.int32, (n, 2 * n), 1)
    band = (kj > qi) & (kj <= qi + n)

    for gi, (win, dil) in enumerate(DIL_GROUPS):
        assert win // dil == n
        q_ref, k_ref, v_ref = qkv_refs[3 * gi:3 * gi + 3]
        s_len = t // dil
        n_blk = s_len // n
        pitch = s_len + n

        def gather(r, carry, q_ref=q_ref, k_ref=k_ref, v_ref=v_ref, dil=dil, s_len=s_len, pitch=pitch):
            rows = pl.ds(r, s_len, stride=dil) if dil > 1 else pl.ds(0, s_len)
            k0 = pl.multiple_of(r * pitch, n)
            qbuf[pl.ds(pl.multiple_of(r * s_len, n), s_len), :] = q_ref[rows, :].astype(BF16)
            kbuf[pl.ds(k0, n), :] = jnp.zeros((n, HEAD_DIM), BF16)
            vbuf[pl.ds(k0, n), :] = jnp.zeros((n, HEAD_DIM), BF16)
            kbuf[pl.ds(k0 + n, s_len), :] = k_ref[rows, :].astype(BF16)
            vbuf[pl.ds(k0 + n, s_len), :] = v_ref[rows, :].astype(BF16)
            return carry

        lax.fori_loop(0, dil, gather, 0)

        def units(it, carry, dil=dil, n_blk=n_blk, pitch=pitch, gi=gi):
            for j in range(ATTN_UNROLL):
                u = it * ATTN_UNROLL + j
                if n_blk == 1:
                    r, blk = u, 0
                elif dil == 1:
                    r, blk = 0, u
                else:
                    r, blk = lax.div(u, jnp.int32(n_blk)), lax.rem(u, jnp.int32(n_blk))
                q = qbuf[pl.ds(pl.multiple_of(u * n, n), n), :]
                k0 = pl.multiple_of(r * pitch + blk * n, n)
                mask = band & (kj >= jnp.where(blk == 0, n, 0))
                s = jnp.where(mask, _dot_nt(q, kbuf[pl.ds(k0, 2 * n), :]), NEG)
                o, lse = _softmax_pv(s, vbuf[pl.ds(k0, 2 * n), :])
                start = r + blk * (n * dil)
                dst = pl.ds(start, n, stride=dil) if dil > 1 else pl.ds(pl.multiple_of(start, n), n)
                o_sc[gi, dst, :] = o
                l_sc[gi, dst, :] = jnp.broadcast_to(lse, (n, HEAD_DIM))
            return carry

        assert (dil * n_blk) % ATTN_UNROLL == 0
        lax.fori_loop(0, dil * n_blk // ATTN_UNROLL, units, 0)

    def merge(c, carry):
        rows = pl.ds(pl.multiple_of(c * MERGE_ROWS, MERGE_ROWS), MERGE_ROWS)
        outs = [o_sc[gi, rows, :] for gi in range(N_DGROUPS)]
        lses = [l_sc[gi, rows, :] for gi in range(N_DGROUPS)]
        b_ref[rows, :] = _merge3(outs, lses).astype(b_ref.dtype)
        return carry

    lax.fori_loop(0, t // MERGE_ROWS, merge, 0)


def dilated_attn_prompt(qkv, batch, seq):
    assert seq % (BAND * max(d for _, d in DIL_GROUPS)) == 0 and seq % MERGE_ROWS == 0
    kv_rows = max(seq + dil * BAND for _, dil in DIL_GROUPS)

    def spec(which, gi):
        return pl.BlockSpec((seq, HEAD_DIM), lambda b, h: (b, (which * N_DGROUPS + gi) * HEADS + h))

    in_specs = [spec(which, gi) for gi in range(N_DGROUPS) for which in range(3)]
    return pl.pallas_call(
        _dil_prompt_kernel,
        grid=(batch, HEADS),
        in_specs=in_specs,
        out_specs=pl.BlockSpec((seq, HEAD_DIM), lambda b, h: (b, h)),
        out_shape=jax.ShapeDtypeStruct((batch * seq, D_GRP), BF16),
        scratch_shapes=[pltpu.VMEM((seq, HEAD_DIM), BF16), pltpu.VMEM((kv_rows, HEAD_DIM), BF16),
                        pltpu.VMEM((kv_rows, HEAD_DIM), BF16),
                        pltpu.VMEM((N_DGROUPS, seq, HEAD_DIM), F32), pltpu.VMEM((N_DGROUPS, seq, HEAD_DIM), F32)],
        compiler_params=_params(("parallel", "parallel"), 40),
        name="dil_attn_prompt",
    )(*([qkv] * (3 * N_DGROUPS)))


def _dil_sample_kernel(q_ref, kn_ref, vn_ref, *rest):
    cache_refs, b_ref = rest[:2 * N_DGROUPS], rest[2 * N_DGROUPS]
    ds = q_ref.shape[0]
    qn = lax.broadcasted_iota(jnp.int32, (ds, ds), 0)
    pn = lax.broadcasted_iota(jnp.int32, (ds, ds), 1)
    for h in range(HEADS):
        outs, lses = [], []
        for gi, (win, dil) in enumerate(DIL_GROUPS):
            n = win // dil
            kc_ref, vc_ref = cache_refs[2 * gi], cache_refs[2 * gi + 1]
            cache_len = kc_ref.shape[0] // HEADS
            cols = slice(gi * D_GRP + h * HEAD_DIM, gi * D_GRP + (h + 1) * HEAD_DIM)
            qi = lax.broadcasted_iota(jnp.int32, (ds, cache_len), 0)
            pj = lax.broadcasted_iota(jnp.int32, (ds, cache_len), 1)
            dist = cache_len + qi - pj
            mask_c = (dist % dil == 0) & (dist <= dil * (n - 1))
            mask_n = (qn >= pn) & ((qn - pn) % dil == 0)
            q = q_ref[:, cols].astype(BF16)
            s_c = jnp.where(mask_c, _dot_nt(q, _head_rows(kc_ref, h, cache_len).astype(BF16)), NEG)
            s_n = jnp.where(mask_n, _dot_nt(q, kn_ref[:, cols].astype(BF16)), NEG)
            mx = jnp.maximum(jnp.max(s_c, axis=-1, keepdims=True), jnp.max(s_n, axis=-1, keepdims=True))
            p_c = jnp.exp(s_c - mx)
            p_n = jnp.exp(s_n - mx)
            l = jnp.sum(p_c, axis=-1, keepdims=True) + jnp.sum(p_n, axis=-1, keepdims=True)
            o = (_dot(p_c.astype(BF16), _head_rows(vc_ref, h, cache_len).astype(BF16))
                 + _dot(p_n.astype(BF16), vn_ref[:, cols].astype(BF16)))
            outs.append(o / l)
            lses.append(mx + jnp.log(l))
        b_ref[:, h * HEAD_DIM:(h + 1) * HEAD_DIM] = _merge3(outs, lses).astype(b_ref.dtype)


def dilated_attn_sample(qkv, caches, layer):
    b, ds, _ = qkv.shape
    cache_specs = []
    for gi, (win, dil) in enumerate(DIL_GROUPS):
        rows = caches[2 * gi].shape[2]
        assert rows // HEADS - dil * (win // dil - 1) >= 0
        cache_specs += [pl.BlockSpec((None, None, rows, HEAD_DIM), lambda i: (layer, i, 0, 0))] * 2

    new = lambda which: pl.BlockSpec((None, ds, D_ATT), lambda i: (i, 0, which))
    return pl.pallas_call(
        _dil_sample_kernel,
        grid=(b,),
        in_specs=[new(0), new(1), new(2)] + cache_specs,
        out_specs=pl.BlockSpec((None, ds, D_GRP), lambda i: (i, 0, 0)),
        out_shape=jax.ShapeDtypeStruct((b, ds, D_GRP), BF16),
        compiler_params=_params(("parallel",), 48),
        name="dil_attn_sample",
    )(qkv, qkv, qkv, *caches)


def _mix_out_kernel(x_ref, a_ref, b_ref, wa_ref, wb_ref, y_ref):
    y_ref[...] = x_ref[...] + _dot(a_ref[...], wa_ref[...]) + _dot(b_ref[...], wb_ref[...])


def mix_out(x, a, b, w_out, layer, tm):
    m = x.shape[0]
    row = lambda width: pl.BlockSpec((tm, width), lambda i: (i, 0))
    return pl.pallas_call(
        _mix_out_kernel,
        grid=(m // tm,),
        in_specs=[row(D_MODEL), row(D_CONV), row(D_GRP),
                  pl.BlockSpec((None, D_CONV, D_MODEL), lambda i: (layer, 0, 0)),
                  pl.BlockSpec((None, D_GRP, D_MODEL), lambda i: (layer, D_CONV // D_GRP, 0))],
        out_specs=row(D_MODEL),
        out_shape=jax.ShapeDtypeStruct((m, D_MODEL), F32),
        compiler_params=_params(("parallel",), 48),
        name="mix_out",
    )(x, a, b, w_out, w_out)


def _sgu_kernel(x_ref, u_ref, gv_ref, lg_ref, lb_ref, ws_ref, bs_ref, w_ref, *out_refs, emit_v):
    y_ref = out_refs[0]
    gv = gv_ref[...]
    mu = jnp.mean(gv, axis=-1, keepdims=True)
    vc = gv - mu
    v = vc * lax.rsqrt(jnp.mean(vc * vc, axis=-1, keepdims=True) + EPS) * lg_ref[...] + lb_ref[...]
    if emit_v:
        out_refs[1][...] = v
    vb = v.astype(BF16)
    rows, cm = x_ref.shape[0], ws_ref.shape[1]
    gated = []
    for g in range(N_SG):
        cols = slice(g * D_SG, (g + 1) * D_SG)
        sv = jnp.concatenate([_dot(ws_ref[g], vb[c * cm:(c + 1) * cm, cols]) + bs_ref[g]
                              for c in range(rows // cm)], axis=0)
        gated.append((u_ref[:, cols] * sv).astype(BF16))
    y_ref[...] = x_ref[...] + _dot(jnp.concatenate(gated, axis=-1), w_ref[...])


def sgu_out(x, z, lg, lb, ws, bs, w_out, layer, tm, emit_v):
    m = x.shape[0]
    cm = ws.shape[1]
    assert tm % cm == 0
    row = lambda jblk: pl.BlockSpec((tm, D_GATE), lambda i: (i, jblk))
    vec = pl.BlockSpec((1, D_GATE), lambda i: (0, 0))
    out_specs = [row(0)]
    out_shape = [jax.ShapeDtypeStruct((m, D_MODEL), F32)]
    if emit_v:
        out_specs.append(row(0))
        out_shape.append(jax.ShapeDtypeStruct((m, D_GATE), F32))
    res = pl.pallas_call(
        functools.partial(_sgu_kernel, emit_v=emit_v),
        grid=(m // tm,),
        in_specs=[row(0), row(0), row(1), vec, vec,
                  pl.BlockSpec((N_SG, cm, cm), lambda i: (0, 0, 0)),
                  pl.BlockSpec((N_SG, cm, D_SG), lambda i: (0, 0, 0)),
                  pl.BlockSpec((None, D_GATE, D_MODEL), lambda i: (layer, 0, 0))],
        out_specs=out_specs,
        out_shape=out_shape,
        compiler_params=_params(("parallel",), 52),
        name="sgu_out",
    )(x, z, z, lg, lb, ws, bs, w_out)
    return res if emit_v else (res[0], None)


def _mem_attn_kernel(x_ref, g_ref, wq_ref, qg_ref, k_ref, v_ref, wo_ref, y_ref):
    bb, tm, _ = x_ref.shape
    x = x_ref[...].reshape(bb * tm, D_MODEL)
    h = (_rms(x) * g_ref[...]).astype(BF16)
    q = _head_norm(_dot(h, wq_ref[...]), qg_ref[...]).astype(BF16)
    n_mem = k_ref.shape[1] // HEADS
    per_seq = []
    for b in range(bb):
        outs = []
        for hd in range(HEADS):
            rows = pl.ds(hd, n_mem, stride=HEADS)
            s = _dot_nt(q[b * tm:(b + 1) * tm, hd * HEAD_DIM:(hd + 1) * HEAD_DIM], k_ref[b, rows, :].astype(BF16))
            o, _ = _softmax_pv(s, v_ref[b, rows, :].astype(BF16))
            outs.append(o.astype(BF16))
        per_seq.append(jnp.concatenate(outs, axis=-1))
    o = jnp.concatenate(per_seq, axis=0) if bb > 1 else per_seq[0]
    y_ref[...] = (x + _dot(o, wo_ref[...])).reshape(bb, tm, D_MODEL)


def mem_attn(x, g, wq, q_gain, k, v, kv_layer, wo, layer, bb, tm):
    b, t, _ = x.shape
    full = lambda shape: pl.BlockSpec(shape, lambda i, j: (0,) * len(shape))
    kv = pl.BlockSpec((None, bb, k.shape[2], HEAD_DIM), lambda i, j: (kv_layer, i, 0, 0))
    xs = pl.BlockSpec((bb, tm, D_MODEL), lambda i, j: (i, j, 0))
    return pl.pallas_call(
        _mem_attn_kernel,
        grid=(b // bb, t // tm),
        in_specs=[xs, full((1, D_MODEL)),
                  pl.BlockSpec((None, D_MODEL, D_MEMATT), lambda i, j: (layer, 0, 0)),
                  full((1, D_MEMATT)), kv, kv,
                  pl.BlockSpec((None, D_MEMATT, D_MODEL), lambda i, j: (layer, 0, 0))],
        out_specs=xs,
        out_shape=jax.ShapeDtypeStruct((b, t, D_MODEL), F32),
        compiler_params=_params(("parallel", "parallel"), 40),
        name="mem_attn",
    )(x, g, wq, q_gain, k, v, wo)


def _tile_heads(g):
    return jnp.tile(g, HEADS)


def _row_tile(m):
    return min(m, 1024)


def kernel(x_prompt, x_sample, mem_prompt, state_conv, cache_k_w128, cache_v_w128, cache_k_w512, cache_v_w512,
           cache_k_w2048, cache_v_w2048, cache_mem_k, cache_mem_v, g_mix, w_in_e, conv_w, conv_b, conv_ln_g,
           conv_ln_b, q_norm_e, k_norm_e, w_out_e, w_in_o, b_in_o, v_ln_g, v_ln_b, w_s, b_s, w_out_o, g_xmem,
           g_mem, wq_mem, wk_mem, wv_mem, q_norm_mem, k_norm_mem, wo_mem, g_ffn, w_ffn1, w_ffn2):
    depth = g_mix.shape[0]
    bp, tp, _ = x_prompt.shape
    bs, ts, _ = x_sample.shape
    mp, ms = bp * tp, bs * ts
    scale = HEAD_DIM ** -0.5
    row = lambda v: v.reshape(1, -1)
    th_rows = lambda c: c.reshape(c.shape[0], c.shape[1], c.shape[2] * HEADS, HEAD_DIM)
    caches = [th_rows(c) for c in (cache_k_w128, cache_v_w128, cache_k_w512, cache_v_w512,
                                   cache_k_w2048, cache_v_w2048)]
    mem_k_s, mem_v_s = th_rows(cache_mem_k), th_rows(cache_mem_v)

    w_in_e, w_out_e, w_in_o, w_out_o = (w.astype(BF16) for w in (w_in_e, w_out_e, w_in_o, w_out_o))
    wq_mem, wo_mem, w_ffn1, w_ffn2 = (w.astype(BF16) for w in (wq_mem, wo_mem, w_ffn1, w_ffn2))
    wkv_mem = jnp.concatenate([wk_mem, wv_mem], axis=2).astype(BF16)

    xp = x_prompt.reshape(mp, D_MODEL)
    xs = x_sample.reshape(ms, D_MODEL)
    mem = mem_prompt.reshape(bp * N_MEM, D_MODEL)
    tm_p, tm_s, tm_mem = _row_tile(mp), _row_tile(ms), _row_tile(bp * N_MEM)
    assert tp % tm_p == 0 or tm_p % tp == 0

    conv_pl, conv_sl, kv_pl, kv_sl, memk_pl, memv_pl, chunk_sl = [], [], [], [], [], [], []
    for i in range(depth):
        j = i // 2
        if i % 2 == 0:
            gains = jnp.concatenate([
                jnp.stack([_tile_heads(q_norm_e[j, gi]) * scale for gi in range(N_DGROUPS)]),
                jnp.stack([_tile_heads(k_norm_e[j, gi]) for gi in range(N_DGROUPS)]),
                jnp.ones((N_DGROUPS, D_GRP), F32)])[:, None, :]
            taps = jnp.broadcast_to(conv_w[j][:, None, :], (CONV_W, SUBLANES, D_CONV))
            conv_args = (taps, row(conv_b[j]), row(conv_ln_g[j]), row(conv_ln_b[j]))
            qkv_col0 = 2 * D_CONV // D_GRP

            def kv_tiles(seq, tm):
                tiles = []
                for gi, (win, _) in enumerate(DIL_GROUPS):
                    keep = min(win, seq)
                    assert keep == seq or (keep <= tm and seq % tm == 0)
                    spec = (tm, 1) if keep == seq else (keep, seq // tm)
                    tiles += [((1 + which) * N_DGROUPS + gi,) + spec for which in range(2)]
                return tiles

            a = glu_proj(xp, row(g_mix[i]), w_in_e, j, tm_p)
            qkv, *new_kv = headnorm_proj(xp, row(g_mix[i]), w_in_e, j, qkv_col0, gains, 2 * N_DGROUPS, tm_p,
                                         kv_tiles(tp, tm_p), True)
            kv_pl.append([kv.reshape(bp, -1, HEADS, HEAD_DIM) for kv in new_kv])
            a3 = a.reshape(bp, tp, D_CONV)
            conv_pl.append(a3[:, tp - (CONV_W - 1):])
            a_out = conv_prompt(a3, *conv_args, tt=512).reshape(mp, D_CONV)
            b_out = dilated_attn_prompt(qkv, bp, tp)
            xp = mix_out(xp, a_out, b_out, w_out_e, j, 512)

            a = glu_proj(xs, row(g_mix[i]), w_in_e, j, tm_s)
            qkv, *new_kv = headnorm_proj(xs, row(g_mix[i]), w_in_e, j, qkv_col0, gains, 2 * N_DGROUPS, tm_s,
                                         kv_tiles(ts, tm_s), True)
            kv_sl.append([kv.reshape(bs, -1, HEADS, HEAD_DIM) for kv in new_kv])
            apad = jnp.concatenate([state_conv[j], a.reshape(bs, ts, D_CONV)], axis=1)
            conv_sl.append(apad[:, apad.shape[1] - (CONV_W - 1):])
            a_out = conv_sample(apad, *conv_args).reshape(ms, D_CONV)
            b_out = dilated_attn_sample(qkv.reshape(bs, ts, 3 * D_ATT), caches, j).reshape(ms, D_GRP)
            xs = mix_out(xs, a_out, b_out, w_out_e, j, tm_s)
        else:
            tril = jnp.tril(jnp.ones((CHUNK, CHUNK), F32))
            for which in range(2):
                x, t = (xp, tp) if which == 0 else (xs, ts)
                m = x.shape[0]
                c = min(CHUNK, t)
                ws = w_s[j][:, :c, :c] * tril[:c, :c]
                bias = b_s[j][:, :c]
                tm = 256
                if c < CHUNK:
                    ws = jnp.einsum('ab,gts->gatbs', jnp.eye(tm // c, dtype=F32), ws).reshape(N_SG, tm, tm)
                    bias = jnp.tile(bias, (1, tm // c))
                bias = jnp.broadcast_to(bias[:, :, None], bias.shape + (D_SG,))
                z = gelu_proj(x, row(g_mix[i]), w_in_o, j, row(b_in_o[j]), _row_tile(m))
                y, v = sgu_out(x, z, row(v_ln_g[j]), row(v_ln_b[j]), ws.astype(BF16), bias, w_out_o, j, tm,
                               emit_v=which == 1)
                if which == 0:
                    xp = y
                else:
                    xs = y
                    chunk_sl.append(v.reshape(bs, ts, D_GATE))

        kgain = jnp.stack([_tile_heads(k_norm_mem[i]), jnp.ones((D_MEMATT,), F32)])[:, None, :]
        mk, mv = headnorm_proj(mem, row(g_mem[i]), wkv_mem, i, 0, kgain, 1, tm_mem,
                               [(0, tm_mem, 1), (1, tm_mem, 1)], False)
        memk_pl.append(mk.reshape(bp, N_MEM, HEADS, HEAD_DIM))
        memv_pl.append(mv.reshape(bp, N_MEM, HEADS, HEAD_DIM))
        mem_k_p = mk.reshape(1, bp, N_MEM * HEADS, HEAD_DIM)
        mem_v_p = mv.reshape(1, bp, N_MEM * HEADS, HEAD_DIM)
        qgain = row(_tile_heads(q_norm_mem[i]) * scale)
        xp = mem_attn(xp.reshape(bp, tp, D_MODEL), row(g_xmem[i]), wq_mem, qgain, mem_k_p, mem_v_p, 0,
                      wo_mem, i, 1, 512).reshape(mp, D_MODEL)
        xs = mem_attn(xs.reshape(bs, ts, D_MODEL), row(g_xmem[i]), wq_mem, qgain, mem_k_s, mem_v_s, i,
                      wo_mem, i, 8, ts).reshape(ms, D_MODEL)

        xp = ffn(xp, row(g_ffn[i]), w_ffn1, w_ffn2, i, tm_p, 512)
        xs = ffn(xs, row(g_ffn[i]), w_ffn1, w_ffn2, i, tm_s, 512)

    stack = lambda items: jnp.stack(items)
    kv_p = [stack([kv[n] for kv in kv_pl]) for n in range(2 * N_DGROUPS)]
    kv_s = [stack([kv[n] for kv in kv_sl]) for n in range(2 * N_DGROUPS)]
    return (xp.reshape(bp, tp, D_MODEL), xs.reshape(bs, ts, D_MODEL), stack(conv_pl), stack(conv_sl),
            *kv_p, *kv_s, stack(memk_pl), stack(memv_pl), stack(chunk_sl))
```

```python
import functools
import math

import numpy as np
import jax
import jax.numpy as jnp
from jax import lax
from jax.experimental import pallas as pl
from jax.experimental.pallas import tpu as pltpu

D_MODEL = 2048
EPS = 1e-6
NEG = -1e30
D_CONV = D_MODEL // 2
CONV_W = 31
HEAD_DIM = 128
DIL_GROUPS = ((128, 1), (512, 4), (2048, 16))
N_DGROUPS = len(DIL_GROUPS)
HEADS = 4
D_GRP = HEADS * HEAD_DIM
D_ATT = N_DGROUPS * D_GRP
CHUNK = 128
D_GATE = D_MODEL
N_SG = 8
D_SG = D_GATE // N_SG
N_MEM = 256
D_MEMATT = HEADS * HEAD_DIM
D_FF = 4 * D_MODEL

F32 = jnp.float32
BF16 = jnp.bfloat16
MIB = 1024 * 1024
NORM_ROWS = 256
CONV_HALO = 32
SUBLANES = 8
MERGE_ROWS = 256
BAND = DIL_GROUPS[0][0] // DIL_GROUPS[0][1]
ATTN_UNROLL = 8
CONV_ROWS = 16
LN_ROWS = 128
LANES = 128


def _params(sem, vmem_mib):
    return pltpu.CompilerParams(dimension_semantics=sem, vmem_limit_bytes=vmem_mib * MIB)


def _dot(a, b):
    return jnp.dot(a, b, preferred_element_type=F32)


def _dot_nt(a, b):
    return lax.dot_general(a, b, (((1,), (1,)), ((), ())), preferred_element_type=F32)


def _rms(x):
    return x * lax.rsqrt(jnp.mean(x * x, axis=-1, keepdims=True) + EPS)


def _norm_to_scratch(x_ref, g_ref, h_ref):
    rows = x_ref.shape[0]
    step = min(NORM_ROWS, rows)

    def body(c, carry):
        r = pl.multiple_of(c * step, step)
        x = x_ref[pl.ds(r, step), :]
        h_ref[pl.ds(r, step), :] = (_rms(x) * g_ref[...]).astype(BF16)
        return carry

    lax.fori_loop(0, rows // step, body, 0)


def _head_norm(acc, gain):
    parts = [_rms(acc[:, h * HEAD_DIM:(h + 1) * HEAD_DIM]) for h in range(acc.shape[1] // HEAD_DIM)]
    return jnp.concatenate(parts, axis=-1) * gain


def _head_rows(ref, h, rows):
    return ref[pl.ds(h, rows, stride=HEADS), :]


def _softmax_pv(s, v):
    mx = jnp.max(s, axis=-1, keepdims=True)
    p = jnp.exp(s - mx)
    l = jnp.sum(p, axis=-1, keepdims=True)
    return _dot(p.astype(BF16), v) / l, mx + jnp.log(l)


def _merge3(outs, lses):
    mx = jnp.maximum(jnp.maximum(lses[0], lses[1]), lses[2])
    es = [jnp.exp(l - mx) for l in lses]
    return (es[0] * outs[0] + es[1] * outs[1] + es[2] * outs[2]) / (es[0] + es[1] + es[2])


def _glu_kernel(x_ref, g_ref, wv_ref, wg_ref, o_ref, h_ref):
    @pl.when(pl.program_id(1) == 0)
    def _():
        _norm_to_scratch(x_ref, g_ref, h_ref)

    h = h_ref[...]
    val = _dot(h, wv_ref[...])
    gate = _dot(h, wg_ref[...])
    o_ref[...] = val * jax.nn.sigmoid(gate)


def glu_proj(x, g, w_in, layer, tm):
    m = x.shape[0]
    tn = 512
    nj = D_CONV // tn
    return pl.pallas_call(
        _glu_kernel,
        grid=(m // tm, nj),
        in_specs=[
            pl.BlockSpec((tm, D_MODEL), lambda i, j: (i, 0)),
            pl.BlockSpec((1, D_MODEL), lambda i, j: (0, 0)),
            pl.BlockSpec((None, D_MODEL, tn), lambda i, j: (layer, 0, j)),
            pl.BlockSpec((None, D_MODEL, tn), lambda i, j: (layer, 0, j + nj)),
        ],
        out_specs=pl.BlockSpec((tm, tn), lambda i, j: (i, j)),
        out_shape=jax.ShapeDtypeStruct((m, D_CONV), F32),
        scratch_shapes=[pltpu.VMEM((tm, D_MODEL), BF16)],
        compiler_params=_params(("parallel", "arbitrary"), 48),
        name="glu_proj",
    )(x, g, w_in, w_in)


def _headnorm_proj_kernel(x_ref, g_ref, w_ref, gain_ref, *rest, n_normed, emit_main, th_tiles):
    n_th = len(th_tiles)
    if emit_main:
        o_ref, th_refs, h_ref = rest[0], rest[1:1 + n_th], rest[1 + n_th]
    else:
        th_refs, h_ref, o_ref = rest[:n_th], rest[n_th], rest[n_th + 1]
    i, j = pl.program_id(0), pl.program_id(1)
    tm = x_ref.shape[0]

    @pl.when(j == 0)
    def _():
        _norm_to_scratch(x_ref, g_ref, h_ref)

    acc = _dot(h_ref[...], w_ref[...])
    o_ref[...] = jnp.where(j < n_normed, _head_norm(acc, gain_ref[0]), acc)

    for th_ref, (col, period) in zip(th_refs, th_tiles):
        rows = th_ref.shape[0] // HEADS

        @pl.when((j == col) & (i % period == period - 1))
        def _():
            for h in range(HEADS):
                th_ref[pl.ds(h, rows, stride=HEADS), :] = o_ref[tm - rows:, h * HEAD_DIM:(h + 1) * HEAD_DIM]


def _th_index(period, i, j):
    return (i // period, 0)


def headnorm_proj(x, g, w, layer, col_block0, gains, n_normed, tm, th_tiles, emit_main):
    m = x.shape[0]
    nj = gains.shape[0]
    out_specs, out_shape = [], []
    if emit_main:
        out_specs.append(pl.BlockSpec((tm, D_GRP), lambda i, j: (i, j)))
        out_shape.append(jax.ShapeDtypeStruct((m, nj * D_GRP), F32))
    for _, keep, period in th_tiles:
        assert keep <= tm and (m // tm) % period == 0
        out_specs.append(pl.BlockSpec((keep * HEADS, HEAD_DIM), functools.partial(_th_index, period)))
        out_shape.append(jax.ShapeDtypeStruct((m // tm // period * keep * HEADS, HEAD_DIM), F32))
    scratch = [pltpu.VMEM((tm, D_MODEL), BF16)]
    if not emit_main:
        scratch.append(pltpu.VMEM((tm, D_GRP), F32))
    return pl.pallas_call(
        functools.partial(_headnorm_proj_kernel, n_normed=n_normed, emit_main=emit_main,
                          th_tiles=tuple((col, period) for col, _, period in th_tiles)),
        grid=(m // tm, nj),
        in_specs=[
            pl.BlockSpec((tm, D_MODEL), lambda i, j: (i, 0)),
            pl.BlockSpec((1, D_MODEL), lambda i, j: (0, 0)),
            pl.BlockSpec((None, D_MODEL, D_GRP), lambda i, j: (layer, 0, j + col_block0)),
            pl.BlockSpec((1, 1, D_GRP), lambda i, j: (j, 0, 0)),
        ],
        out_specs=out_specs,
        out_shape=out_shape,
        scratch_shapes=scratch,
        compiler_params=_params(("arbitrary", "arbitrary"), 52),
        name="headnorm_proj",
    )(x, g, w, gains)


def _gelu_proj_kernel(x_ref, g_ref, w_ref, b_ref, o_ref, h_ref):
    @pl.when(pl.program_id(1) == 0)
    def _():
        _norm_to_scratch(x_ref, g_ref, h_ref)

    z = _dot(h_ref[...], w_ref[...]) + b_ref[...]
    o_ref[...] = 0.5 * z * (1.0 + lax.erf(z * np.float32(math.sqrt(0.5))))


def gelu_proj(x, g, w, layer, b, tm):
    m = x.shape[0]
    n = w.shape[2]
    tn = 1024
    return pl.pallas_call(
        _gelu_proj_kernel,
        grid=(m // tm, n // tn),
        in_specs=[
            pl.BlockSpec((tm, D_MODEL), lambda i, j: (i, 0)),
            pl.BlockSpec((1, D_MODEL), lambda i, j: (0, 0)),
            pl.BlockSpec((None, D_MODEL, tn), lambda i, j: (layer, 0, j)),
            pl.BlockSpec((1, tn), lambda i, j: (0, j)),
        ],
        out_specs=pl.BlockSpec((tm, tn), lambda i, j: (i, j)),
        out_shape=jax.ShapeDtypeStruct((m, n), F32),
        scratch_shapes=[pltpu.VMEM((tm, D_MODEL), BF16)],
        compiler_params=_params(("parallel", "arbitrary"), 48),
        name="gelu_proj",
    )(x, g, w, b)


def _ffn_kernel(x_ref, g_ref, w1_ref, w2_ref, o_ref, h_ref):
    @pl.when(pl.program_id(1) == 0)
    def _():
        _norm_to_scratch(x_ref, g_ref, h_ref)
        o_ref[...] = x_ref[...]

    hid = jnp.maximum(_dot(h_ref[...], w1_ref[...]), 0.0)
    o_ref[...] += _dot((hid * hid).astype(BF16), w2_ref[...])


def ffn(x, g, w1, w2, layer, tm, tf):
    m = x.shape[0]
    return pl.pallas_call(
        _ffn_kernel,
        grid=(m // tm, D_FF // tf),
        in_specs=[
            pl.BlockSpec((tm, D_MODEL), lambda i, f: (i, 0)),
            pl.BlockSpec((1, D_MODEL), lambda i, f: (0, 0)),
            pl.BlockSpec((None, D_MODEL, tf), lambda i, f: (layer, 0, f)),
            pl.BlockSpec((None, tf, D_MODEL), lambda i, f: (layer, f, 0)),
        ],
        out_specs=pl.BlockSpec((tm, D_MODEL), lambda i, f: (i, 0)),
        out_shape=jax.ShapeDtypeStruct((m, D_MODEL), F32),
        scratch_shapes=[pltpu.VMEM((tm, D_MODEL), BF16)],
        compiler_params=_params(("parallel", "arbitrary"), 60),
        name="ffn",
    )(x, g, w1, w2)


def _ln_silu(acc, lg_ref, lb_ref):
    mu = jnp.mean(acc, axis=-1, keepdims=True)
    xc = acc - mu
    y = xc * lax.rsqrt(jnp.mean(xc * xc, axis=-1, keepdims=True) + EPS)
    y = y * lg_ref[...] + lb_ref[...]
    return y * jax.nn.sigmoid(y)


def _conv_prompt_kernel(a_ref, halo_ref, w_ref, cb_ref, lg_ref, lb_ref, o_ref, sh_ref, y_ref):
    tt = a_ref.shape[0]
    first = pl.program_id(1) == 0
    sh_ref[0, 0:CONV_HALO, :] = jnp.where(first, 0.0, halo_ref[...])
    sh_ref[0, CONV_HALO:, :] = a_ref[...]
    span = tt + CONV_HALO - SUBLANES
    for s in range(1, SUBLANES):
        sh_ref[s, 0:span, :] = sh_ref[0, s:s + span, :]
    lead = CONV_HALO - (CONV_W - 1)
    groups = CONV_ROWS // SUBLANES

    def conv_rows(c, carry):
        r = pl.multiple_of(c * CONV_ROWS, CONV_ROWS)
        acc = jnp.zeros((groups, SUBLANES, D_CONV), F32) + cb_ref[...]
        for k in range(CONV_W):
            q, s = divmod(k + lead, SUBLANES)
            x = sh_ref[s, pl.ds(r + q * SUBLANES, CONV_ROWS), :]
            acc = acc + x.reshape(groups, SUBLANES, D_CONV) * w_ref[k]
        y_ref[pl.ds(r, CONV_ROWS), :] = acc.reshape(CONV_ROWS, D_CONV)
        return carry

    lax.fori_loop(0, tt // CONV_ROWS, conv_rows, 0)

    def norm_rows(c, carry):
        r = pl.multiple_of(c * LN_ROWS, LN_ROWS)
        o_ref[pl.ds(r, LN_ROWS), :] = _ln_silu(y_ref[pl.ds(r, LN_ROWS), :], lg_ref, lb_ref).astype(o_ref.dtype)
        return carry

    lax.fori_loop(0, tt // LN_ROWS, norm_rows, 0)


def conv_prompt(a, w, cb, lg, lb, tt):
    b, t, _ = a.shape
    hb = tt // CONV_HALO
    vec = pl.BlockSpec((1, D_CONV), lambda i, j: (0, 0))
    return pl.pallas_call(
        _conv_prompt_kernel,
        grid=(b, t // tt),
        in_specs=[
            pl.BlockSpec((None, tt, D_CONV), lambda i, j: (i, j, 0)),
            pl.BlockSpec((None, CONV_HALO, D_CONV), lambda i, j: (i, jnp.maximum(j * hb - 1, 0), 0)),
            pl.BlockSpec((CONV_W, SUBLANES, D_CONV), lambda i, j: (0, 0, 0)),
            vec, vec, vec,
        ],
        out_specs=pl.BlockSpec((None, tt, D_CONV), lambda i, j: (i, j, 0)),
        out_shape=jax.ShapeDtypeStruct((b, t, D_CONV), BF16),
        scratch_shapes=[pltpu.VMEM((SUBLANES, CONV_HALO + tt, D_CONV), F32), pltpu.VMEM((tt, D_CONV), F32)],
        compiler_params=_params(("parallel", "arbitrary"), 40),
        name="conv_prompt",
    )(a, a, w, cb, lg, lb)


def _conv_sample_kernel(apad_ref, w_ref, cb_ref, lg_ref, lb_ref, o_ref):
    rows = o_ref.shape[0]
    acc = jnp.zeros((rows, D_CONV), F32) + cb_ref[...]
    for k in range(CONV_W):
        acc = acc + apad_ref[k:k + rows, :] * w_ref[k]
    o_ref[...] = _ln_silu(acc, lg_ref, lb_ref).astype(o_ref.dtype)


def conv_sample(apad, w, cb, lg, lb):
    b, tp, _ = apad.shape
    t = tp - (CONV_W - 1)
    vec = pl.BlockSpec((1, D_CONV), lambda i: (0, 0))
    return pl.pallas_call(
        _conv_sample_kernel,
        grid=(b,),
        in_specs=[
            pl.BlockSpec((None, tp, D_CONV), lambda i: (i, 0, 0)),
            pl.BlockSpec((CONV_W, SUBLANES, D_CONV), lambda i: (0, 0, 0)),
            vec, vec, vec,
        ],
        out_specs=pl.BlockSpec((None, t, D_CONV), lambda i: (i, 0, 0)),
        out_shape=jax.ShapeDtypeStruct((b, t, D_CONV), BF16),
        compiler_params=_params(("parallel",), 32),
        name="conv_sample",
    )(apad, w, cb, lg, lb)


def _dil_prompt_kernel(*refs):
    qkv_refs = refs[:3 * N_DGROUPS]
    b_ref, qbuf, kbuf, vbuf, o_sc, l_sc = refs[3 * N_DGROUPS:]
    t = b_ref.shape[0]
    n = BAND
    qi = lax.broadcasted_iota(jnp.int32, (n, 2 * n), 0)
    kj = lax.broadcasted_iota(jnp.int32, (n, 2 * n), 1)
    band = (kj > qi) & (kj <= qi + n)

    for gi, (win, dil) in enumerate(DIL_GROUPS):
        assert win // dil == n
        q_ref, k_ref, v_ref = qkv_refs[3 * gi:3 * gi + 3]
        s_len = t // dil
        n_blk = s_len // n
        pitch = s_len + n

        def gather(r, carry, q_ref=q_ref, k_ref=k_ref, v_ref=v_ref, dil=dil, s_len=s_len, pitch=pitch):
            rows = pl.ds(r, s_len, stride=dil) if dil > 1 else pl.ds(0, s_len)
            k0 = pl.multiple_of(r * pitch, n)
            qbuf[pl.ds(pl.multiple_of(r * s_len, n), s_len), :] = q_ref[rows, :].astype(BF16)
            kbuf[pl.ds(k0, n), :] = jnp.zeros((n, HEAD_DIM), BF16)
            vbuf[pl.ds(k0, n), :] = jnp.zeros((n, HEAD_DIM), BF16)
            kbuf[pl.ds(k0 + n, s_len), :] = k_ref[rows, :].astype(BF16)
            vbuf[pl.ds(k0 + n, s_len), :] = v_ref[rows, :].astype(BF16)
            return carry

        lax.fori_loop(0, dil, gather, 0)

        def units(it, carry, dil=dil, n_blk=n_blk, pitch=pitch, gi=gi):
            for j in range(ATTN_UNROLL):
                u = it * ATTN_UNROLL + j
                if n_blk == 1:
                    r, blk = u, 0
                elif dil == 1:
                    r, blk = 0, u
                else:
                    r, blk = lax.div(u, jnp.int32(n_blk)), lax.rem(u, jnp.int32(n_blk))
                q = qbuf[pl.ds(pl.multiple_of(u * n, n), n), :]
                k0 = pl.multiple_of(r * pitch + blk * n, n)
                mask = band & (kj >= jnp.where(blk == 0, n, 0))
                s = jnp.where(mask, _dot_nt(q, kbuf[pl.ds(k0, 2 * n), :]), NEG)
                o, lse = _softmax_pv(s, vbuf[pl.ds(k0, 2 * n), :])
                start = r + blk * (n * dil)
                dst = pl.ds(start, n, stride=dil) if dil > 1 else pl.ds(pl.multiple_of(start, n), n)
                o_sc[gi, dst, :] = o
                l_sc[gi, dst, :] = jnp.broadcast_to(lse, (n, HEAD_DIM))
            return carry

        assert (dil * n_blk) % ATTN_UNROLL == 0
        lax.fori_loop(0, dil * n_blk // ATTN_UNROLL, units, 0)

    def merge(c, carry):
        rows = pl.ds(pl.multiple_of(c * MERGE_ROWS, MERGE_ROWS), MERGE_ROWS)
        outs = [o_sc[gi, rows, :] for gi in range(N_DGROUPS)]
        lses = [l_sc[gi, rows, :] for gi in range(N_DGROUPS)]
        b_ref[rows, :] = _merge3(outs, lses).astype(b_ref.dtype)
        return carry

    lax.fori_loop(0, t // MERGE_ROWS, merge, 0)


def dilated_attn_prompt(qkv, batch, seq):
    assert seq % (BAND * max(d for _, d in DIL_GROUPS)) == 0 and seq % MERGE_ROWS == 0
    kv_rows = max(seq + dil * BAND for _, dil in DIL_GROUPS)

    def spec(which, gi):
        return pl.BlockSpec((seq, HEAD_DIM), lambda b, h: (b, (which * N_DGROUPS + gi) * HEADS + h))

    in_specs = [spec(which, gi) for gi in range(N_DGROUPS) for which in range(3)]
    return pl.pallas_call(
        _dil_prompt_kernel,
        grid=(batch, HEADS),
        in_specs=in_specs,
        out_specs=pl.BlockSpec((seq, HEAD_DIM), lambda b, h: (b, h)),
        out_shape=jax.ShapeDtypeStruct((batch * seq, D_GRP), BF16),
        scratch_shapes=[pltpu.VMEM((seq, HEAD_DIM), BF16), pltpu.VMEM((kv_rows, HEAD_DIM), BF16),
                        pltpu.VMEM((kv_rows, HEAD_DIM), BF16),
                        pltpu.VMEM((N_DGROUPS, seq, HEAD_DIM), F32), pltpu.VMEM((N_DGROUPS, seq, HEAD_DIM), F32)],
        compiler_params=_params(("parallel", "parallel"), 40),
        name="dil_attn_prompt",
    )(*([qkv] * (3 * N_DGROUPS)))


def _dil_sample_kernel(q_ref, kn_ref, vn_ref, *rest):
    cache_refs, b_ref = rest[:2 * N_DGROUPS], rest[2 * N_DGROUPS]
    ds = q_ref.shape[0]
    qn = lax.broadcasted_iota(jnp.int32, (ds, ds), 0)
    pn = lax.broadcasted_iota(jnp.int32, (ds, ds), 1)
    for h in range(HEADS):
        outs, lses = [], []
        for gi, (win, dil) in enumerate(DIL_GROUPS):
            n = win // dil
            kc_ref, vc_ref = cache_refs[2 * gi], cache_refs[2 * gi + 1]
            cache_len = kc_ref.shape[0] // HEADS
            cols = slice(gi * D_GRP + h * HEAD_DIM, gi * D_GRP + (h + 1) * HEAD_DIM)
            qi = lax.broadcasted_iota(jnp.int32, (ds, cache_len), 0)
            pj = lax.broadcasted_iota(jnp.int32, (ds, cache_len), 1)
            dist = cache_len + qi - pj
            mask_c = (dist % dil == 0) & (dist <= dil * (n - 1))
            mask_n = (qn >= pn) & ((qn - pn) % dil == 0)
            q = q_ref[:, cols].astype(BF16)
            s_c = jnp.where(mask_c, _dot_nt(q, _head_rows(kc_ref, h, cache_len).astype(BF16)), NEG)
            s_n = jnp.where(mask_n, _dot_nt(q, kn_ref[:, cols].astype(BF16)), NEG)
            mx = jnp.maximum(jnp.max(s_c, axis=-1, keepdims=True), jnp.max(s_n, axis=-1, keepdims=True))
            p_c = jnp.exp(s_c - mx)
            p_n = jnp.exp(s_n - mx)
            l = jnp.sum(p_c, axis=-1, keepdims=True) + jnp.sum(p_n, axis=-1, keepdims=True)
            o = (_dot(p_c.astype(BF16), _head_rows(vc_ref, h, cache_len).astype(BF16))
                 + _dot(p_n.astype(BF16), vn_ref[:, cols].astype(BF16)))
            outs.append(o / l)
            lses.append(mx + jnp.log(l))
        b_ref[:, h * HEAD_DIM:(h + 1) * HEAD_DIM] = _merge3(outs, lses).astype(b_ref.dtype)


def dilated_attn_sample(qkv, caches, layer):
    b, ds, _ = qkv.shape
    cache_specs = []
    for gi, (win, dil) in enumerate(DIL_GROUPS):
        rows = caches[2 * gi].shape[2]
        assert rows // HEADS - dil * (win // dil - 1) >= 0
        cache_specs += [pl.BlockSpec((None, None, rows, HEAD_DIM), lambda i: (layer, i, 0, 0))] * 2

    new = lambda which: pl.BlockSpec((None, ds, D_ATT), lambda i: (i, 0, which))
    return pl.pallas_call(
        _dil_sample_kernel,
        grid=(b,),
        in_specs=[new(0), new(1), new(2)] + cache_specs,
        out_specs=pl.BlockSpec((None, ds, D_GRP), lambda i: (i, 0, 0)),
        out_shape=jax.ShapeDtypeStruct((b, ds, D_GRP), BF16),
        compiler_params=_params(("parallel",), 48),
        name="dil_attn_sample",
    )(qkv, qkv, qkv, *caches)


def _mix_out_kernel(x_ref, a_ref, b_ref, wa_ref, wb_ref, y_ref):
    y_ref[...] = x_ref[...] + _dot(a_ref[...], wa_ref[...]) + _dot(b_ref[...], wb_ref[...])


def mix_out(x, a, b, w_out, layer, tm):
    m = x.shape[0]
    row = lambda width: pl.BlockSpec((tm, width), lambda i: (i, 0))
    return pl.pallas_call(
        _mix_out_kernel,
        grid=(m // tm,),
        in_specs=[row(D_MODEL), row(D_CONV), row(D_GRP),
                  pl.BlockSpec((None, D_CONV, D_MODEL), lambda i: (layer, 0, 0)),
                  pl.BlockSpec((None, D_GRP, D_MODEL), lambda i: (layer, D_CONV // D_GRP, 0))],
        out_specs=row(D_MODEL),
        out_shape=jax.ShapeDtypeStruct((m, D_MODEL), F32),
        compiler_params=_params(("parallel",), 48),
        name="mix_out",
    )(x, a, b, w_out, w_out)


def _mix_mem_kernel(x_ref, a_ref, b_ref, wa_ref, wb_ref, g_ref, wq_ref, qg_ref, k_ref, v_ref, wo_ref, y_ref):
    x1 = x_ref[...] + _dot(a_ref[...], wa_ref[...]) + _dot(b_ref[...], wb_ref[...])
    y_ref[...] = _mem_attn_tail(x1, 1, g_ref, wq_ref, qg_ref, _kv_head_of(k_ref, v_ref), wo_ref)


def mix_out_mem(x, a, b, w_out, layer, mem_args, batch, tm):
    m = x.shape[0]
    nt = m // batch // tm
    row = lambda width: pl.BlockSpec((tm, width), lambda i, j: (i * nt + j, 0))
    mem_specs, mem_ops = _mem_operands(*mem_args)
    return pl.pallas_call(
        _mix_mem_kernel,
        grid=(batch, nt),
        in_specs=[row(D_MODEL), row(D_CONV), row(D_GRP),
                  _resident((None, D_CONV, D_MODEL), (layer, 0, 0)),
                  _resident((None, D_GRP, D_MODEL), (layer, D_CONV // D_GRP, 0))] + mem_specs,
        out_specs=row(D_MODEL),
        out_shape=jax.ShapeDtypeStruct((m, D_MODEL), F32),
        compiler_params=_params(("parallel", "parallel"), 56),
        name="mix_out_mem",
    )(x, a, b, w_out, w_out, *mem_ops)


def _sgu_gated(u_ref, gv_ref, lg_ref, lb_ref, ws_ref, bs_ref):
    gv = gv_ref[...]
    mu = jnp.mean(gv, axis=-1, keepdims=True)
    vc = gv - mu
    v = vc * lax.rsqrt(jnp.mean(vc * vc, axis=-1, keepdims=True) + EPS) * lg_ref[...] + lb_ref[...]
    vb = v.astype(BF16)
    rows, cm = gv.shape[0], ws_ref.shape[1]
    gated = []
    for g in range(N_SG):
        cols = slice(g * D_SG, (g + 1) * D_SG)
        sv = jnp.concatenate([_dot(ws_ref[g], vb[c * cm:(c + 1) * cm, cols]) + bs_ref[g]
                              for c in range(rows // cm)], axis=0)
        gated.append((u_ref[:, cols] * sv).astype(BF16))
    return jnp.concatenate(gated, axis=-1), v


def _sgu_kernel(x_ref, u_ref, gv_ref, lg_ref, lb_ref, ws_ref, bs_ref, w_ref, *out_refs, emit_v):
    gated, v = _sgu_gated(u_ref, gv_ref, lg_ref, lb_ref, ws_ref, bs_ref)
    if emit_v:
        out_refs[1][...] = v
    out_refs[0][...] = x_ref[...] + _dot(gated, w_ref[...])


def _sgu_mem_kernel(x_ref, u_ref, gv_ref, lg_ref, lb_ref, ws_ref, bs_ref, w_ref,
                    g_ref, wq_ref, qg_ref, k_ref, v_ref, wo_ref, y_ref):
    gated, _ = _sgu_gated(u_ref, gv_ref, lg_ref, lb_ref, ws_ref, bs_ref)
    x1 = x_ref[...] + _dot(gated, w_ref[...])
    y_ref[...] = _mem_attn_tail(x1, 1, g_ref, wq_ref, qg_ref, _kv_head_of(k_ref, v_ref), wo_ref)


def sgu_out_mem(x, z, lg, lb, ws, bs, w_out, layer, mem_args, batch, tm):
    m = x.shape[0]
    cm = ws.shape[1]
    nt = m // batch // tm
    assert tm % cm == 0
    row = lambda jblk: pl.BlockSpec((tm, D_GATE), lambda i, j: (i * nt + j, jblk))
    vec = pl.BlockSpec((1, D_GATE), lambda i, j: (0, 0))
    mem_specs, mem_ops = _mem_operands(*mem_args)
    return pl.pallas_call(
        _sgu_mem_kernel,
        grid=(batch, nt),
        in_specs=[row(0), row(0), row(1), vec, vec,
                  _resident((N_SG, cm, cm), (0, 0, 0)), _resident((N_SG, cm, D_SG), (0, 0, 0)),
                  _resident((None, D_GATE, D_MODEL), (layer, 0, 0))] + mem_specs,
        out_specs=row(0),
        out_shape=jax.ShapeDtypeStruct((m, D_MODEL), F32),
        compiler_params=_params(("parallel", "parallel"), 56),
        name="sgu_out_mem",
    )(x, z, z, lg, lb, ws, bs, w_out, *mem_ops)


def sgu_out(x, z, lg, lb, ws, bs, w_out, layer, tm, emit_v):
    m = x.shape[0]
    cm = ws.shape[1]
    assert tm % cm == 0
    row = lambda jblk: pl.BlockSpec((tm, D_GATE), lambda i: (i, jblk))
    vec = pl.BlockSpec((1, D_GATE), lambda i: (0, 0))
    out_specs = [row(0)]
    out_shape = [jax.ShapeDtypeStruct((m, D_MODEL), F32)]
    if emit_v:
        out_specs.append(row(0))
        out_shape.append(jax.ShapeDtypeStruct((m, D_GATE), F32))
    res = pl.pallas_call(
        functools.partial(_sgu_kernel, emit_v=emit_v),
        grid=(m // tm,),
        in_specs=[row(0), row(0), row(1), vec, vec,
                  pl.BlockSpec((N_SG, cm, cm), lambda i: (0, 0, 0)),
                  pl.BlockSpec((N_SG, cm, D_SG), lambda i: (0, 0, 0)),
                  pl.BlockSpec((None, D_GATE, D_MODEL), lambda i: (layer, 0, 0))],
        out_specs=out_specs,
        out_shape=out_shape,
        compiler_params=_params(("parallel",), 52),
        name="sgu_out",
    )(x, z, z, lg, lb, ws, bs, w_out)
    return res if emit_v else (res[0], None)


def _mem_attn_tail(x, n_seq, g_ref, wq_ref, qg_ref, kv_head, wo_ref):
    tm = x.shape[0] // n_seq
    h = (_rms(x) * g_ref[...]).astype(BF16)
    q = _head_norm(_dot(h, wq_ref[...]), qg_ref[...]).astype(BF16)
    per_seq = []
    for b in range(n_seq):
        outs = []
        for hd in range(HEADS):
            k, v = kv_head(b, hd)
            o, _ = _softmax_pv(_dot_nt(q[b * tm:(b + 1) * tm, hd * HEAD_DIM:(hd + 1) * HEAD_DIM], k), v)
            outs.append(o.astype(BF16))
        per_seq.append(jnp.concatenate(outs, axis=-1))
    o = jnp.concatenate(per_seq, axis=0) if n_seq > 1 else per_seq[0]
    return x + _dot(o, wo_ref[...])


def _kv_head_of(k_ref, v_ref):
    n_mem = k_ref.shape[0] // HEADS
    return lambda b, hd: (_head_rows(k_ref, hd, n_mem).astype(BF16), _head_rows(v_ref, hd, n_mem).astype(BF16))


def _resident(shape, index):
    return pl.BlockSpec(shape, lambda *_: index, pipeline_mode=pl.Buffered(1))


def _mem_operands(g, wq, q_gain, k, v, kv_layer, wo, layer):
    specs = [_resident((1, D_MODEL), (0, 0)),
             _resident((None, D_MODEL, D_MEMATT), (layer, 0, 0)),
             _resident((1, D_MEMATT), (0, 0)),
             pl.BlockSpec((None, None, k.shape[2], HEAD_DIM), lambda i, j: (kv_layer, i, 0, 0)),
             pl.BlockSpec((None, None, k.shape[2], HEAD_DIM), lambda i, j: (kv_layer, i, 0, 0)),
             _resident((None, D_MEMATT, D_MODEL), (layer, 0, 0))]
    return specs, (g, wq, q_gain, k, v, wo)


def _mem_attn_kernel(x_ref, g_ref, wq_ref, qg_ref, k_ref, v_ref, wo_ref, y_ref):
    bb, tm, _ = x_ref.shape
    n_mem = k_ref.shape[1] // HEADS

    def kv_head(b, hd):
        rows = pl.ds(hd, n_mem, stride=HEADS)
        return k_ref[b, rows, :].astype(BF16), v_ref[b, rows, :].astype(BF16)

    x = x_ref[...].reshape(bb * tm, D_MODEL)
    y_ref[...] = _mem_attn_tail(x, bb, g_ref, wq_ref, qg_ref, kv_head, wo_ref).reshape(bb, tm, D_MODEL)


def mem_attn(x, g, wq, q_gain, k, v, kv_layer, wo, layer, bb, tm):
    b, t, _ = x.shape
    full = lambda shape: pl.BlockSpec(shape, lambda i, j: (0,) * len(shape))
    kv = pl.BlockSpec((None, bb, k.shape[2], HEAD_DIM), lambda i, j: (kv_layer, i, 0, 0))
    xs = pl.BlockSpec((bb, tm, D_MODEL), lambda i, j: (i, j, 0))
    return pl.pallas_call(
        _mem_attn_kernel,
        grid=(b // bb, t // tm),
        in_specs=[xs, full((1, D_MODEL)),
                  pl.BlockSpec((None, D_MODEL, D_MEMATT), lambda i, j: (layer, 0, 0)),
                  full((1, D_MEMATT)), kv, kv,
                  pl.BlockSpec((None, D_MEMATT, D_MODEL), lambda i, j: (layer, 0, 0))],
        out_specs=xs,
        out_shape=jax.ShapeDtypeStruct((b, t, D_MODEL), F32),
        compiler_params=_params(("parallel", "parallel"), 40),
        name="mem_attn",
    )(x, g, wq, q_gain, k, v, wo)


def _tile_heads(g):
    return jnp.tile(g, HEADS)


def _row_tile(m):
    return min(m, 1024)


def kernel(x_prompt, x_sample, mem_prompt, state_conv, cache_k_w128, cache_v_w128, cache_k_w512, cache_v_w512,
           cache_k_w2048, cache_v_w2048, cache_mem_k, cache_mem_v, g_mix, w_in_e, conv_w, conv_b, conv_ln_g,
           conv_ln_b, q_norm_e, k_norm_e, w_out_e, w_in_o, b_in_o, v_ln_g, v_ln_b, w_s, b_s, w_out_o, g_xmem,
           g_mem, wq_mem, wk_mem, wv_mem, q_norm_mem, k_norm_mem, wo_mem, g_ffn, w_ffn1, w_ffn2):
    depth = g_mix.shape[0]
    bp, tp, _ = x_prompt.shape
    bs, ts, _ = x_sample.shape
    mp, ms = bp * tp, bs * ts
    scale = HEAD_DIM ** -0.5
    row = lambda v: v.reshape(1, -1)
    th_rows = lambda c: c.reshape(c.shape[0], c.shape[1], c.shape[2] * HEADS, HEAD_DIM)
    caches = [th_rows(c) for c in (cache_k_w128, cache_v_w128, cache_k_w512, cache_v_w512,
                                   cache_k_w2048, cache_v_w2048)]
    mem_k_s, mem_v_s = th_rows(cache_mem_k), th_rows(cache_mem_v)

    w_in_e, w_out_e, w_in_o, w_out_o = (w.astype(BF16) for w in (w_in_e, w_out_e, w_in_o, w_out_o))
    wq_mem, wo_mem, w_ffn1, w_ffn2 = (w.astype(BF16) for w in (wq_mem, wo_mem, w_ffn1, w_ffn2))
    wkv_mem = jnp.concatenate([wk_mem, wv_mem], axis=2).astype(BF16)

    xp = x_prompt.reshape(mp, D_MODEL)
    xs = x_sample.reshape(ms, D_MODEL)
    mem = mem_prompt.reshape(bp * N_MEM, D_MODEL)
    tm_p, tm_s, tm_mem = _row_tile(mp), _row_tile(ms), _row_tile(bp * N_MEM)
    assert tp % tm_p == 0 or tm_p % tp == 0

    conv_pl, conv_sl, kv_pl, kv_sl, memk_pl, memv_pl, chunk_sl = [], [], [], [], [], [], []
    for i in range(depth):
        j = i // 2
        kgain = jnp.stack([_tile_heads(k_norm_mem[i]), jnp.ones((D_MEMATT,), F32)])[:, None, :]
        mk, mv = headnorm_proj(mem, row(g_mem[i]), wkv_mem, i, 0, kgain, 1, tm_mem,
                               [(0, tm_mem, 1), (1, tm_mem, 1)], False)
        memk_pl.append(mk.reshape(bp, N_MEM, HEADS, HEAD_DIM))
        memv_pl.append(mv.reshape(bp, N_MEM, HEADS, HEAD_DIM))
        qgain = row(_tile_heads(q_norm_mem[i]) * scale)
        mem_p = (row(g_xmem[i]), wq_mem, qgain, mk.reshape(1, bp, N_MEM * HEADS, HEAD_DIM),
                 mv.reshape(1, bp, N_MEM * HEADS, HEAD_DIM), 0, wo_mem, i)
        mem_s = (row(g_xmem[i]), wq_mem, qgain, mem_k_s, mem_v_s, i, wo_mem, i)

        if i % 2 == 0:
            gains = jnp.concatenate([
                jnp.stack([_tile_heads(q_norm_e[j, gi]) * scale for gi in range(N_DGROUPS)]),
                jnp.stack([_tile_heads(k_norm_e[j, gi]) for gi in range(N_DGROUPS)]),
                jnp.ones((N_DGROUPS, D_GRP), F32)])[:, None, :]
            taps = jnp.broadcast_to(conv_w[j][:, None, :], (CONV_W, SUBLANES, D_CONV))
            conv_args = (taps, row(conv_b[j]), row(conv_ln_g[j]), row(conv_ln_b[j]))
            qkv_col0 = 2 * D_CONV // D_GRP

            def kv_tiles(seq, tm):
                tiles = []
                for gi, (win, _) in enumerate(DIL_GROUPS):
                    keep = min(win, seq)
                    assert keep == seq or (keep <= tm and seq % tm == 0)
                    spec = (tm, 1) if keep == seq else (keep, seq // tm)
                    tiles += [((1 + which) * N_DGROUPS + gi,) + spec for which in range(2)]
                return tiles

            a = glu_proj(xp, row(g_mix[i]), w_in_e, j, tm_p)
            qkv, *new_kv = headnorm_proj(xp, row(g_mix[i]), w_in_e, j, qkv_col0, gains, 2 * N_DGROUPS, tm_p,
                                         kv_tiles(tp, tm_p), True)
            kv_pl.append([kv.reshape(bp, -1, HEADS, HEAD_DIM) for kv in new_kv])
            a3 = a.reshape(bp, tp, D_CONV)
            conv_pl.append(a3[:, tp - (CONV_W - 1):])
            a_out = conv_prompt(a3, *conv_args, tt=512).reshape(mp, D_CONV)
            b_out = dilated_attn_prompt(qkv, bp, tp)
            xp = mix_out_mem(xp, a_out, b_out, w_out_e, j, mem_p, bp, 512)

            a = glu_proj(xs, row(g_mix[i]), w_in_e, j, tm_s)
            qkv, *new_kv = headnorm_proj(xs, row(g_mix[i]), w_in_e, j, qkv_col0, gains, 2 * N_DGROUPS, tm_s,
                                         kv_tiles(ts, tm_s), True)
            kv_sl.append([kv.reshape(bs, -1, HEADS, HEAD_DIM) for kv in new_kv])
            apad = jnp.concatenate([state_conv[j], a.reshape(bs, ts, D_CONV)], axis=1)
            conv_sl.append(apad[:, apad.shape[1] - (CONV_W - 1):])
            a_out = conv_sample(apad, *conv_args).reshape(ms, D_CONV)
            b_out = dilated_attn_sample(qkv.reshape(bs, ts, 3 * D_ATT), caches, j).reshape(ms, D_GRP)
            xs = mix_out(xs, a_out, b_out, w_out_e, j, tm_s)
        else:
            tm = 256
            sgu_vecs = (row(v_ln_g[j]), row(v_ln_b[j]))
            tril = jnp.tril(jnp.ones((CHUNK, CHUNK), F32))

            def spatial(c, rows):
                ws, bias = w_s[j][:, :c, :c] * tril[:c, :c], b_s[j][:, :c]
                if c < rows:
                    seq_of = jnp.arange(rows) // c
                    ws = jnp.tile(ws, (1, rows // c, rows // c)) * (seq_of[:, None] == seq_of[None, :])
                    bias = jnp.tile(bias, (1, rows // c))
                return ws.astype(BF16), jnp.broadcast_to(bias[:, :, None], bias.shape + (D_SG,))

            z = gelu_proj(xp, row(g_mix[i]), w_in_o, j, row(b_in_o[j]), tm_p)
            ws, bias = spatial(min(CHUNK, tp), min(CHUNK, tp))
            xp = sgu_out_mem(xp, z, *sgu_vecs, ws, bias, w_out_o, j, mem_p, bp, tm)

            z = gelu_proj(xs, row(g_mix[i]), w_in_o, j, row(b_in_o[j]), tm_s)
            ws, bias = spatial(min(CHUNK, ts), CHUNK if ts >= CHUNK else tm)
            xs, v = sgu_out(xs, z, *sgu_vecs, ws, bias, w_out_o, j, tm, emit_v=True)
            chunk_sl.append(v.reshape(bs, ts, D_GATE))

        xs = mem_attn(xs.reshape(bs, ts, D_MODEL), *mem_s, 8, ts).reshape(ms, D_MODEL)
        xp = ffn(xp, row(g_ffn[i]), w_ffn1, w_ffn2, i, tm_p, 1024)
        xs = ffn(xs, row(g_ffn[i]), w_ffn1, w_ffn2, i, tm_s, 1024)

    stack = lambda items: jnp.stack(items)
    kv_p = [stack([kv[n] for kv in kv_pl]) for n in range(2 * N_DGROUPS)]
    kv_s = [stack([kv[n] for kv in kv_sl]) for n in range(2 * N_DGROUPS)]
    return (xp.reshape(bp, tp, D_MODEL), xs.reshape(bs, ts, D_MODEL), stack(conv_pl), stack(conv_sl),
            *kv_p, *kv_s, stack(memk_pl), stack(memv_pl), stack(chunk_sl))
```

```python
import functools
import math

import numpy as np
import jax
import jax.numpy as jnp
from jax import lax
from jax.experimental import pallas as pl
from jax.experimental.pallas import tpu as pltpu

D_MODEL = 2048
EPS = 1e-6
NEG = -1e30
D_CONV = D_MODEL // 2
CONV_W = 31
HEAD_DIM = 128
DIL_GROUPS = ((128, 1), (512, 4), (2048, 16))
N_DGROUPS = len(DIL_GROUPS)
HEADS = 4
D_GRP = HEADS * HEAD_DIM
D_ATT = N_DGROUPS * D_GRP
CHUNK = 128
D_GATE = D_MODEL
N_SG = 8
D_SG = D_GATE // N_SG
N_MEM = 256
D_MEMATT = HEADS * HEAD_DIM
D_FF = 4 * D_MODEL

F32 = jnp.float32
BF16 = jnp.bfloat16
MIB = 1024 * 1024
NORM_ROWS = 256
CONV_HALO = 32
SUBLANES = 8
MERGE_ROWS = 256
BAND = DIL_GROUPS[0][0] // DIL_GROUPS[0][1]
GATHER_STRIDE = 4
ATTN_UNROLL = 16
CONV_ROWS = 16
LN_ROWS = 128
LANES = 128


def _params(sem, vmem_mib):
    return pltpu.CompilerParams(dimension_semantics=sem, vmem_limit_bytes=vmem_mib * MIB)


def _dot(a, b):
    return jnp.dot(a, b, preferred_element_type=F32)


def _dot_nt(a, b):
    return lax.dot_general(a, b, (((1,), (1,)), ((), ())), preferred_element_type=F32)


def _rms(x):
    return x * lax.rsqrt(jnp.mean(x * x, axis=-1, keepdims=True) + EPS)


def _norm_to_scratch(x_ref, g_ref, h_ref):
    rows = x_ref.shape[0]
    step = min(NORM_ROWS, rows)

    def body(c, carry):
        r = pl.multiple_of(c * step, step)
        x = x_ref[pl.ds(r, step), :]
        h_ref[pl.ds(r, step), :] = (_rms(x) * g_ref[...]).astype(BF16)
        return carry

    lax.fori_loop(0, rows // step, body, 0)


def _head_norm(acc, gain):
    parts = [_rms(acc[:, h * HEAD_DIM:(h + 1) * HEAD_DIM]) for h in range(acc.shape[1] // HEAD_DIM)]
    return jnp.concatenate(parts, axis=-1) * gain


def _head_rows(ref, h, rows):
    return ref[pl.ds(h, rows, stride=HEADS), :]


def _softmax_pv(s, v):
    mx = jnp.max(s, axis=-1, keepdims=True)
    p = jnp.exp(s - mx)
    l = jnp.sum(p, axis=-1, keepdims=True)
    return _dot(p.astype(BF16), v) / l, mx + jnp.log(l)


def _merge3(outs, lses):
    mx = jnp.maximum(jnp.maximum(lses[0], lses[1]), lses[2])
    es = [jnp.exp(l - mx) for l in lses]
    return (es[0] * outs[0] + es[1] * outs[1] + es[2] * outs[2]) / (es[0] + es[1] + es[2])


def _glu_kernel(x_ref, g_ref, wv_ref, wg_ref, o_ref, h_ref):
    @pl.when(pl.program_id(1) == 0)
    def _():
        _norm_to_scratch(x_ref, g_ref, h_ref)

    h = h_ref[...]
    val = _dot(h, wv_ref[...])
    gate = _dot(h, wg_ref[...])
    o_ref[...] = val * jax.nn.sigmoid(gate)


def glu_proj(x, g, w_in, layer, tm):
    m = x.shape[0]
    tn = 512
    nj = D_CONV // tn
    return pl.pallas_call(
        _glu_kernel,
        grid=(m // tm, nj),
        in_specs=[
            pl.BlockSpec((tm, D_MODEL), lambda i, j: (i, 0)),
            pl.BlockSpec((1, D_MODEL), lambda i, j: (0, 0)),
            pl.BlockSpec((None, D_MODEL, tn), lambda i, j: (layer, 0, j)),
            pl.BlockSpec((None, D_MODEL, tn), lambda i, j: (layer, 0, j + nj)),
        ],
        out_specs=pl.BlockSpec((tm, tn), lambda i, j: (i, j)),
        out_shape=jax.ShapeDtypeStruct((m, D_CONV), F32),
        scratch_shapes=[pltpu.VMEM((tm, D_MODEL), BF16)],
        compiler_params=_params(("parallel", "arbitrary"), 48),
        name="glu_proj",
    )(x, g, w_in, w_in)


def _headnorm_proj_kernel(x_ref, g_ref, w_ref, gain_ref, *rest, n_normed, emit_main, th_tiles):
    n_th = len(th_tiles)
    if emit_main:
        o_ref, th_refs, h_ref = rest[0], rest[1:1 + n_th], rest[1 + n_th]
    else:
        th_refs, h_ref, o_ref = rest[:n_th], rest[n_th], rest[n_th + 1]
    i, j = pl.program_id(0), pl.program_id(1)
    tm = x_ref.shape[0]

    @pl.when(j == 0)
    def _():
        _norm_to_scratch(x_ref, g_ref, h_ref)

    acc = _dot(h_ref[...], w_ref[...])
    o_ref[...] = jnp.where(j < n_normed, _head_norm(acc, gain_ref[0]), acc)

    for th_ref, (col, period) in zip(th_refs, th_tiles):
        rows = th_ref.shape[0] // HEADS

        @pl.when((j == col) & (i % period == period - 1))
        def _():
            for h in range(HEADS):
                th_ref[pl.ds(h, rows, stride=HEADS), :] = o_ref[tm - rows:, h * HEAD_DIM:(h + 1) * HEAD_DIM]


def _th_index(period, i, j):
    return (i // period, 0)


def headnorm_proj(x, g, w, layer, col_block0, gains, n_normed, tm, th_tiles, emit_main):
    m = x.shape[0]
    nj = gains.shape[0]
    out_specs, out_shape = [], []
    if emit_main:
        out_specs.append(pl.BlockSpec((tm, D_GRP), lambda i, j: (i, j)))
        out_shape.append(jax.ShapeDtypeStruct((m, nj * D_GRP), F32))
    for _, keep, period in th_tiles:
        assert keep <= tm and (m // tm) % period == 0
        out_specs.append(pl.BlockSpec((keep * HEADS, HEAD_DIM), functools.partial(_th_index, period)))
        out_shape.append(jax.ShapeDtypeStruct((m // tm // period * keep * HEADS, HEAD_DIM), F32))
    scratch = [pltpu.VMEM((tm, D_MODEL), BF16)]
    if not emit_main:
        scratch.append(pltpu.VMEM((tm, D_GRP), F32))
    return pl.pallas_call(
        functools.partial(_headnorm_proj_kernel, n_normed=n_normed, emit_main=emit_main,
                          th_tiles=tuple((col, period) for col, _, period in th_tiles)),
        grid=(m // tm, nj),
        in_specs=[
            pl.BlockSpec((tm, D_MODEL), lambda i, j: (i, 0)),
            pl.BlockSpec((1, D_MODEL), lambda i, j: (0, 0)),
            pl.BlockSpec((None, D_MODEL, D_GRP), lambda i, j: (layer, 0, j + col_block0)),
            pl.BlockSpec((1, 1, D_GRP), lambda i, j: (j, 0, 0)),
        ],
        out_specs=out_specs,
        out_shape=out_shape,
        scratch_shapes=scratch,
        compiler_params=_params(("arbitrary", "arbitrary"), 52),
        name="headnorm_proj",
    )(x, g, w, gains)


def _gelu_proj_kernel(x_ref, g_ref, w_ref, b_ref, o_ref, h_ref):
    @pl.when(pl.program_id(1) == 0)
    def _():
        _norm_to_scratch(x_ref, g_ref, h_ref)

    z = _dot(h_ref[...], w_ref[...]) + b_ref[...]
    o_ref[...] = 0.5 * z * (1.0 + lax.erf(z * np.float32(math.sqrt(0.5))))


def gelu_proj(x, g, w, layer, b, tm):
    m = x.shape[0]
    n = w.shape[2]
    tn = 1024
    return pl.pallas_call(
        _gelu_proj_kernel,
        grid=(m // tm, n // tn),
        in_specs=[
            pl.BlockSpec((tm, D_MODEL), lambda i, j: (i, 0)),
            pl.BlockSpec((1, D_MODEL), lambda i, j: (0, 0)),
            pl.BlockSpec((None, D_MODEL, tn), lambda i, j: (layer, 0, j)),
            pl.BlockSpec((1, tn), lambda i, j: (0, j)),
        ],
        out_specs=pl.BlockSpec((tm, tn), lambda i, j: (i, j)),
        out_shape=jax.ShapeDtypeStruct((m, n), F32),
        scratch_shapes=[pltpu.VMEM((tm, D_MODEL), BF16)],
        compiler_params=_params(("parallel", "arbitrary"), 48),
        name="gelu_proj",
    )(x, g, w, b)


def _ffn_kernel(x_ref, g_ref, w1_ref, w2_ref, o_ref, h_ref):
    @pl.when(pl.program_id(1) == 0)
    def _():
        _norm_to_scratch(x_ref, g_ref, h_ref)
        o_ref[...] = x_ref[...]

    hid = jnp.maximum(_dot(h_ref[...], w1_ref[...]), 0.0)
    o_ref[...] += _dot((hid * hid).astype(BF16), w2_ref[...])


def ffn(x, g, w1, w2, layer, tm, tf):
    m = x.shape[0]
    return pl.pallas_call(
        _ffn_kernel,
        grid=(m // tm, D_FF // tf),
        in_specs=[
            pl.BlockSpec((tm, D_MODEL), lambda i, f: (i, 0)),
            pl.BlockSpec((1, D_MODEL), lambda i, f: (0, 0)),
            pl.BlockSpec((None, D_MODEL, tf), lambda i, f: (layer, 0, f)),
            pl.BlockSpec((None, tf, D_MODEL), lambda i, f: (layer, f, 0)),
        ],
        out_specs=pl.BlockSpec((tm, D_MODEL), lambda i, f: (i, 0)),
        out_shape=jax.ShapeDtypeStruct((m, D_MODEL), F32),
        scratch_shapes=[pltpu.VMEM((tm, D_MODEL), BF16)],
        compiler_params=_params(("parallel", "arbitrary"), 60),
        name="ffn",
    )(x, g, w1, w2)


def _ffn_cast_kernel(x_ref, g_ref, w1_ref, w2_ref, o_ref, w1b_ref, w2b_ref, h_ref):
    @pl.when(pl.program_id(0) == 0)
    def _():
        _norm_to_scratch(x_ref, g_ref, h_ref)
        o_ref[...] = x_ref[...]

    w1 = w1_ref[...].astype(BF16)
    w2 = w2_ref[...].astype(BF16)
    w1b_ref[...] = w1
    w2b_ref[...] = w2
    hid = jnp.maximum(_dot(h_ref[...], w1), 0.0)
    o_ref[...] += _dot((hid * hid).astype(BF16), w2)


def ffn_cast(x, g, w1, w2, layer, tf):
    m = x.shape[0]
    const = lambda shape: pl.BlockSpec(shape, lambda f: (0,) * len(shape))
    return pl.pallas_call(
        _ffn_cast_kernel,
        grid=(D_FF // tf,),
        in_specs=[const((m, D_MODEL)), const((1, D_MODEL)),
                  pl.BlockSpec((None, D_MODEL, tf), lambda f: (layer, 0, f)),
                  pl.BlockSpec((None, tf, D_MODEL), lambda f: (layer, f, 0))],
        out_specs=[const((m, D_MODEL)),
                   pl.BlockSpec((None, D_MODEL, tf), lambda f: (0, 0, f)),
                   pl.BlockSpec((None, tf, D_MODEL), lambda f: (0, f, 0))],
        out_shape=[jax.ShapeDtypeStruct((m, D_MODEL), F32),
                   jax.ShapeDtypeStruct((1, D_MODEL, D_FF), BF16),
                   jax.ShapeDtypeStruct((1, D_FF, D_MODEL), BF16)],
        scratch_shapes=[pltpu.VMEM((m, D_MODEL), BF16)],
        compiler_params=_params(("arbitrary",), 48),
        name="ffn_cast",
    )(x, g, w1, w2)


def _ln_silu(acc, lg_ref, lb_ref):
    mu = jnp.mean(acc, axis=-1, keepdims=True)
    xc = acc - mu
    y = xc * lax.rsqrt(jnp.mean(xc * xc, axis=-1, keepdims=True) + EPS)
    y = y * lg_ref[...] + lb_ref[...]
    return y * jax.nn.sigmoid(y)


def _conv_prompt_kernel(a_ref, halo_ref, w_ref, cb_ref, lg_ref, lb_ref, o_ref, sh_ref, y_ref, wrep_ref):
    tt = a_ref.shape[0]
    first = pl.program_id(1) == 0
    for k in range(CONV_W):
        wrep_ref[k] = jnp.broadcast_to(w_ref[k:k + 1, :], (SUBLANES, D_CONV))
    sh_ref[0, 0:CONV_HALO, :] = jnp.where(first, 0.0, halo_ref[...])
    sh_ref[0, CONV_HALO:, :] = a_ref[...]
    span = tt + CONV_HALO - SUBLANES
    for s in range(1, SUBLANES):
        sh_ref[s, 0:span, :] = sh_ref[0, s:s + span, :]
    lead = CONV_HALO - (CONV_W - 1)
    groups = CONV_ROWS // SUBLANES

    def conv_rows(c, carry):
        r = pl.multiple_of(c * CONV_ROWS, CONV_ROWS)
        acc = jnp.zeros((groups, SUBLANES, D_CONV), F32) + cb_ref[...]
        for k in range(CONV_W):
            q, s = divmod(k + lead, SUBLANES)
            x = sh_ref[s, pl.ds(r + q * SUBLANES, CONV_ROWS), :]
            acc = acc + x.reshape(groups, SUBLANES, D_CONV) * wrep_ref[k]
        y_ref[pl.ds(r, CONV_ROWS), :] = acc.reshape(CONV_ROWS, D_CONV)
        return carry

    lax.fori_loop(0, tt // CONV_ROWS, conv_rows, 0)

    def norm_rows(c, carry):
        r = pl.multiple_of(c * LN_ROWS, LN_ROWS)
        o_ref[pl.ds(r, LN_ROWS), :] = _ln_silu(y_ref[pl.ds(r, LN_ROWS), :], lg_ref, lb_ref).astype(o_ref.dtype)
        return carry

    lax.fori_loop(0, tt // LN_ROWS, norm_rows, 0)


def conv_prompt(a, w, cb, lg, lb, tt):
    b, t, _ = a.shape
    hb = tt // CONV_HALO
    vec = pl.BlockSpec((1, D_CONV), lambda i, j: (0, 0))
    return pl.pallas_call(
        _conv_prompt_kernel,
        grid=(b, t // tt),
        in_specs=[
            pl.BlockSpec((None, tt, D_CONV), lambda i, j: (i, j, 0)),
            pl.BlockSpec((None, CONV_HALO, D_CONV), lambda i, j: (i, jnp.maximum(j * hb - 1, 0), 0)),
            pl.BlockSpec((CONV_W, D_CONV), lambda i, j: (0, 0)),
            vec, vec, vec,
        ],
        out_specs=pl.BlockSpec((None, tt, D_CONV), lambda i, j: (i, j, 0)),
        out_shape=jax.ShapeDtypeStruct((b, t, D_CONV), BF16),
        scratch_shapes=[pltpu.VMEM((SUBLANES, CONV_HALO + tt, D_CONV), F32), pltpu.VMEM((tt, D_CONV), F32),
                        pltpu.VMEM((CONV_W, SUBLANES, D_CONV), F32)],
        compiler_params=_params(("parallel", "arbitrary"), 40),
        name="conv_prompt",
    )(a, a, w, cb, lg, lb)


def _conv_sample_kernel(apad_ref, w_ref, cb_ref, lg_ref, lb_ref, o_ref):
    rows = o_ref.shape[0]
    acc = jnp.zeros((rows, D_CONV), F32) + cb_ref[...]
    for k in range(CONV_W):
        acc = acc + apad_ref[k:k + rows, :] * w_ref[k:k + 1, :]
    o_ref[...] = _ln_silu(acc, lg_ref, lb_ref).astype(o_ref.dtype)


def conv_sample(apad, w, cb, lg, lb):
    b, tp, _ = apad.shape
    t = tp - (CONV_W - 1)
    vec = pl.BlockSpec((1, D_CONV), lambda i: (0, 0))
    return pl.pallas_call(
        _conv_sample_kernel,
        grid=(b,),
        in_specs=[
            pl.BlockSpec((None, tp, D_CONV), lambda i: (i, 0, 0)),
            pl.BlockSpec((CONV_W, D_CONV), lambda i: (0, 0)),
            vec, vec, vec,
        ],
        out_specs=pl.BlockSpec((None, t, D_CONV), lambda i: (i, 0, 0)),
        out_shape=jax.ShapeDtypeStruct((b, t, D_CONV), BF16),
        compiler_params=_params(("parallel",), 32),
        name="conv_sample",
    )(apad, w, cb, lg, lb)


def _dil_prompt_kernel(*refs):
    qkv_refs = refs[:3 * N_DGROUPS]
    b_ref, qbuf, kbuf, vbuf, o_sc, l_sc, stage = refs[3 * N_DGROUPS:]
    t = b_ref.shape[0]
    n = BAND
    qi = lax.broadcasted_iota(jnp.int32, (n, 2 * n), 0)
    kj = lax.broadcasted_iota(jnp.int32, (n, 2 * n), 1)
    band = (kj > qi) & (kj <= qi + n)

    for gi, (win, dil) in enumerate(DIL_GROUPS):
        assert win // dil == n
        q_ref, k_ref, v_ref = qkv_refs[3 * gi:3 * gi + 3]
        s_len = t // dil
        n_blk = s_len // n
        pitch = s_len + n

        def place(r, q_rows, k_rows, v_rows, s_len=s_len, pitch=pitch):
            k0 = pl.multiple_of(r * pitch, n)
            qbuf[pl.ds(pl.multiple_of(r * s_len, n), s_len), :] = q_rows.astype(BF16)
            kbuf[pl.ds(k0, n), :] = jnp.zeros((n, HEAD_DIM), BF16)
            vbuf[pl.ds(k0, n), :] = jnp.zeros((n, HEAD_DIM), BF16)
            kbuf[pl.ds(k0 + n, s_len), :] = k_rows.astype(BF16)
            vbuf[pl.ds(k0 + n, s_len), :] = v_rows.astype(BF16)

        if dil > GATHER_STRIDE and dil % GATHER_STRIDE == 0:
            outer = dil // GATHER_STRIDE
            mid = t // GATHER_STRIDE

            def gather(r_in, carry, refs=(q_ref, k_ref, v_ref), s_len=s_len, outer=outer, mid=mid, place=place):
                for w, ref in enumerate(refs):
                    stage[w, 0:mid, :] = ref[pl.ds(r_in, mid, stride=GATHER_STRIDE), :]
                for m in range(outer):
                    place(r_in + m * GATHER_STRIDE,
                          *[stage[w, pl.ds(m, s_len, stride=outer), :] for w in range(3)])
                return carry

            lax.fori_loop(0, GATHER_STRIDE, gather, 0)
        else:
            def gather(r, carry, refs=(q_ref, k_ref, v_ref), dil=dil, s_len=s_len, place=place):
                rows = pl.ds(r, s_len, stride=dil) if dil > 1 else pl.ds(0, s_len)
                place(r, *[ref[rows, :] for ref in refs])
                return carry

            lax.fori_loop(0, dil, gather, 0)

        def units(it, carry, dil=dil, n_blk=n_blk, pitch=pitch, gi=gi):
            for j in range(ATTN_UNROLL):
                u = it * ATTN_UNROLL + j
                if n_blk == 1:
                    r, blk = u, 0
                elif dil == 1:
                    r, blk = 0, u
                else:
                    r, blk = lax.div(u, jnp.int32(n_blk)), lax.rem(u, jnp.int32(n_blk))
                q = qbuf[pl.ds(pl.multiple_of(u * n, n), n), :]
                k0 = pl.multiple_of(r * pitch + blk * n, n)
                mask = band & (kj >= jnp.where(blk == 0, n, 0))
                s = jnp.where(mask, _dot_nt(q, kbuf[pl.ds(k0, 2 * n), :]), NEG)
                o, lse = _softmax_pv(s, vbuf[pl.ds(k0, 2 * n), :])
                start = r + blk * (n * dil)
                dst = pl.ds(start, n, stride=dil) if dil > 1 else pl.ds(pl.multiple_of(start, n), n)
                o_sc[gi, dst, :] = o
                l_sc[gi, dst, :] = jnp.broadcast_to(lse, (n, HEAD_DIM))
            return carry

        assert (dil * n_blk) % ATTN_UNROLL == 0
        lax.fori_loop(0, dil * n_blk // ATTN_UNROLL, units, 0)

    def merge(c, carry):
        rows = pl.ds(pl.multiple_of(c * MERGE_ROWS, MERGE_ROWS), MERGE_ROWS)
        outs = [o_sc[gi, rows, :] for gi in range(N_DGROUPS)]
        lses = [l_sc[gi, rows, :] for gi in range(N_DGROUPS)]
        b_ref[rows, :] = _merge3(outs, lses).astype(b_ref.dtype)
        return carry

    lax.fori_loop(0, t // MERGE_ROWS, merge, 0)


def dilated_attn_prompt(qkv, batch, seq):
    assert seq % (BAND * max(d for _, d in DIL_GROUPS)) == 0 and seq % MERGE_ROWS == 0
    kv_rows = max(seq + dil * BAND for _, dil in DIL_GROUPS)

    def spec(which, gi):
        return pl.BlockSpec((seq, HEAD_DIM), lambda b, h: (b, (which * N_DGROUPS + gi) * HEADS + h))

    in_specs = [spec(which, gi) for gi in range(N_DGROUPS) for which in range(3)]
    return pl.pallas_call(
        _dil_prompt_kernel,
        grid=(batch, HEADS),
        in_specs=in_specs,
        out_specs=pl.BlockSpec((seq, HEAD_DIM), lambda b, h: (b, h)),
        out_shape=jax.ShapeDtypeStruct((batch * seq, D_GRP), BF16),
        scratch_shapes=[pltpu.VMEM((seq, HEAD_DIM), BF16), pltpu.VMEM((kv_rows, HEAD_DIM), BF16),
                        pltpu.VMEM((kv_rows, HEAD_DIM), BF16),
                        pltpu.VMEM((N_DGROUPS, seq, HEAD_DIM), F32), pltpu.VMEM((N_DGROUPS, seq, HEAD_DIM), F32),
                        pltpu.VMEM((3, seq // GATHER_STRIDE, HEAD_DIM), F32)],
        compiler_params=_params(("parallel", "parallel"), 40),
        name="dil_attn_prompt",
    )(*([qkv] * (3 * N_DGROUPS)))


def _dil_sample_kernel(q_ref, kn_ref, vn_ref, *rest):
    cache_refs, b_ref = rest[:2 * N_DGROUPS], rest[2 * N_DGROUPS]
    ds = q_ref.shape[0]
    qn = lax.broadcasted_iota(jnp.int32, (ds, ds), 0)
    pn = lax.broadcasted_iota(jnp.int32, (ds, ds), 1)
    for h in range(HEADS):
        outs, lses = [], []
        for gi, (win, dil) in enumerate(DIL_GROUPS):
            n = win // dil
            kc_ref, vc_ref = cache_refs[2 * gi], cache_refs[2 * gi + 1]
            cache_len = kc_ref.shape[0] // HEADS
            cols = slice(gi * D_GRP + h * HEAD_DIM, gi * D_GRP + (h + 1) * HEAD_DIM)
            qi = lax.broadcasted_iota(jnp.int32, (ds, cache_len), 0)
            pj = lax.broadcasted_iota(jnp.int32, (ds, cache_len), 1)
            dist = cache_len + qi - pj
            mask_c = (dist % dil == 0) & (dist <= dil * (n - 1))
            mask_n = (qn >= pn) & ((qn - pn) % dil == 0)
            q = q_ref[:, cols].astype(BF16)
            s_c = jnp.where(mask_c, _dot_nt(q, _head_rows(kc_ref, h, cache_len).astype(BF16)), NEG)
            s_n = jnp.where(mask_n, _dot_nt(q, kn_ref[:, cols].astype(BF16)), NEG)
            mx = jnp.maximum(jnp.max(s_c, axis=-1, keepdims=True), jnp.max(s_n, axis=-1, keepdims=True))
            p_c = jnp.exp(s_c - mx)
            p_n = jnp.exp(s_n - mx)
            l = jnp.sum(p_c, axis=-1, keepdims=True) + jnp.sum(p_n, axis=-1, keepdims=True)
            o = (_dot(p_c.astype(BF16), _head_rows(vc_ref, h, cache_len).astype(BF16))
                 + _dot(p_n.astype(BF16), vn_ref[:, cols].astype(BF16)))
            outs.append(o / l)
            lses.append(mx + jnp.log(l))
        b_ref[:, h * HEAD_DIM:(h + 1) * HEAD_DIM] = _merge3(outs, lses).astype(b_ref.dtype)


def dilated_attn_sample(qkv, caches, layer):
    b, ds, _ = qkv.shape
    cache_specs = []
    for gi, (win, dil) in enumerate(DIL_GROUPS):
        rows = caches[2 * gi].shape[2]
        assert rows // HEADS - dil * (win // dil - 1) >= 0
        cache_specs += [pl.BlockSpec((None, None, rows, HEAD_DIM), lambda i: (layer, i, 0, 0))] * 2

    new = lambda which: pl.BlockSpec((None, ds, D_ATT), lambda i: (i, 0, which))
    return pl.pallas_call(
        _dil_sample_kernel,
        grid=(b,),
        in_specs=[new(0), new(1), new(2)] + cache_specs,
        out_specs=pl.BlockSpec((None, ds, D_GRP), lambda i: (i, 0, 0)),
        out_shape=jax.ShapeDtypeStruct((b, ds, D_GRP), BF16),
        compiler_params=_params(("parallel",), 48),
        name="dil_attn_sample",
    )(qkv, qkv, qkv, *caches)


def _mix_out_kernel(x_ref, a_ref, b_ref, wa_ref, wb_ref, y_ref):
    y_ref[...] = x_ref[...] + _dot(a_ref[...], wa_ref[...]) + _dot(b_ref[...], wb_ref[...])


def mix_out(x, a, b, w_out, layer, tm):
    m = x.shape[0]
    row = lambda width: pl.BlockSpec((tm, width), lambda i: (i, 0))
    return pl.pallas_call(
        _mix_out_kernel,
        grid=(m // tm,),
        in_specs=[row(D_MODEL), row(D_CONV), row(D_GRP),
                  pl.BlockSpec((None, D_CONV, D_MODEL), lambda i: (layer, 0, 0)),
                  pl.BlockSpec((None, D_GRP, D_MODEL), lambda i: (layer, D_CONV // D_GRP, 0))],
        out_specs=row(D_MODEL),
        out_shape=jax.ShapeDtypeStruct((m, D_MODEL), F32),
        compiler_params=_params(("parallel",), 48),
        name="mix_out",
    )(x, a, b, w_out, w_out)


def _mix_mem_kernel(x_ref, a_ref, b_ref, wa_ref, wb_ref, g_ref, wq_ref, qg_ref, k_ref, v_ref, wo_ref, y_ref):
    x1 = x_ref[...] + _dot(a_ref[...], wa_ref[...]) + _dot(b_ref[...], wb_ref[...])
    y_ref[...] = _mem_attn_tail(x1, 1, g_ref, wq_ref, qg_ref, _kv_head_of(k_ref, v_ref), wo_ref)


def mix_out_mem(x, a, b, w_out, layer, mem_args, batch, tm):
    m = x.shape[0]
    nt = m // batch // tm
    row = lambda width: pl.BlockSpec((tm, width), lambda i, j: (i * nt + j, 0))
    mem_specs, mem_ops = _mem_operands(*mem_args)
    return pl.pallas_call(
        _mix_mem_kernel,
        grid=(batch, nt),
        in_specs=[row(D_MODEL), row(D_CONV), row(D_GRP),
                  _resident((None, D_CONV, D_MODEL), (layer, 0, 0)),
                  _resident((None, D_GRP, D_MODEL), (layer, D_CONV // D_GRP, 0))] + mem_specs,
        out_specs=row(D_MODEL),
        out_shape=jax.ShapeDtypeStruct((m, D_MODEL), F32),
        compiler_params=_params(("parallel", "parallel"), 56),
        name="mix_out_mem",
    )(x, a, b, w_out, w_out, *mem_ops)


def _sgu_gated(u_ref, gv_ref, lg_ref, lb_ref, ws_ref, bs_ref):
    gv = gv_ref[...]
    mu = jnp.mean(gv, axis=-1, keepdims=True)
    vc = gv - mu
    v = vc * lax.rsqrt(jnp.mean(vc * vc, axis=-1, keepdims=True) + EPS) * lg_ref[...] + lb_ref[...]
    vb = v.astype(BF16)
    rows, cm = gv.shape[0], ws_ref.shape[1]
    gated = []
    for g in range(N_SG):
        cols = slice(g * D_SG, (g + 1) * D_SG)
        sv = jnp.concatenate([_dot(ws_ref[g], vb[c * cm:(c + 1) * cm, cols]) + bs_ref[g]
                              for c in range(rows // cm)], axis=0)
        gated.append((u_ref[:, cols] * sv).astype(BF16))
    return jnp.concatenate(gated, axis=-1), v


def _sgu_kernel(x_ref, u_ref, gv_ref, lg_ref, lb_ref, ws_ref, bs_ref, w_ref, *out_refs, emit_v):
    gated, v = _sgu_gated(u_ref, gv_ref, lg_ref, lb_ref, ws_ref, bs_ref)
    if emit_v:
        out_refs[1][...] = v
    out_refs[0][...] = x_ref[...] + _dot(gated, w_ref[...])


def _sgu_mem_kernel(x_ref, u_ref, gv_ref, lg_ref, lb_ref, ws_ref, bs_ref, w_ref,
                    g_ref, wq_ref, qg_ref, k_ref, v_ref, wo_ref, y_ref):
    gated, _ = _sgu_gated(u_ref, gv_ref, lg_ref, lb_ref, ws_ref, bs_ref)
    x1 = x_ref[...] + _dot(gated, w_ref[...])
    y_ref[...] = _mem_attn_tail(x1, 1, g_ref, wq_ref, qg_ref, _kv_head_of(k_ref, v_ref), wo_ref)


def sgu_out_mem(x, z, lg, lb, ws, bs, w_out, layer, mem_args, batch, tm):
    m = x.shape[0]
    cm = ws.shape[1]
    nt = m // batch // tm
    assert tm % cm == 0
    row = lambda jblk: pl.BlockSpec((tm, D_GATE), lambda i, j: (i * nt + j, jblk))
    vec = pl.BlockSpec((1, D_GATE), lambda i, j: (0, 0))
    mem_specs, mem_ops = _mem_operands(*mem_args)
    return pl.pallas_call(
        _sgu_mem_kernel,
        grid=(batch, nt),
        in_specs=[row(0), row(0), row(1), vec, vec,
                  _resident((N_SG, cm, cm), (0, 0, 0)), _resident((N_SG, cm, D_SG), (0, 0, 0)),
                  _resident((None, D_GATE, D_MODEL), (layer, 0, 0))] + mem_specs,
        out_specs=row(0),
        out_shape=jax.ShapeDtypeStruct((m, D_MODEL), F32),
        compiler_params=_params(("parallel", "parallel"), 56),
        name="sgu_out_mem",
    )(x, z, z, lg, lb, ws, bs, w_out, *mem_ops)


def sgu_out(x, z, lg, lb, ws, bs, w_out, layer, tm, emit_v):
    m = x.shape[0]
    cm = ws.shape[1]
    assert tm % cm == 0
    row = lambda jblk: pl.BlockSpec((tm, D_GATE), lambda i: (i, jblk))
    vec = pl.BlockSpec((1, D_GATE), lambda i: (0, 0))
    out_specs = [row(0)]
    out_shape = [jax.ShapeDtypeStruct((m, D_MODEL), F32)]
    if emit_v:
        out_specs.append(row(0))
        out_shape.append(jax.ShapeDtypeStruct((m, D_GATE), F32))
    res = pl.pallas_call(
        functools.partial(_sgu_kernel, emit_v=emit_v),
        grid=(m // tm,),
        in_specs=[row(0), row(0), row(1), vec, vec,
                  pl.BlockSpec((N_SG, cm, cm), lambda i: (0, 0, 0)),
                  pl.BlockSpec((N_SG, cm, D_SG), lambda i: (0, 0, 0)),
                  pl.BlockSpec((None, D_GATE, D_MODEL), lambda i: (layer, 0, 0))],
        out_specs=out_specs,
        out_shape=out_shape,
        compiler_params=_params(("parallel",), 52),
        name="sgu_out",
    )(x, z, z, lg, lb, ws, bs, w_out)
    return res if emit_v else (res[0], None)


def _mem_attn_tail(x, n_seq, g_ref, wq_ref, qg_ref, kv_head, wo_ref):
    tm = x.shape[0] // n_seq
    h = (_rms(x) * g_ref[...]).astype(BF16)
    q = _head_norm(_dot(h, wq_ref[...]), qg_ref[...]).astype(BF16)
    per_seq = []
    for b in range(n_seq):
        outs = []
        for hd in range(HEADS):
            k, v = kv_head(b, hd)
            o, _ = _softmax_pv(_dot_nt(q[b * tm:(b + 1) * tm, hd * HEAD_DIM:(hd + 1) * HEAD_DIM], k), v)
            outs.append(o.astype(BF16))
        per_seq.append(jnp.concatenate(outs, axis=-1))
    o = jnp.concatenate(per_seq, axis=0) if n_seq > 1 else per_seq[0]
    return x + _dot(o, wo_ref[...])


def _kv_head_of(k_ref, v_ref):
    n_mem = k_ref.shape[0] // HEADS
    return lambda b, hd: (_head_rows(k_ref, hd, n_mem).astype(BF16), _head_rows(v_ref, hd, n_mem).astype(BF16))


def _resident(shape, index):
    return pl.BlockSpec(shape, lambda *_: index, pipeline_mode=pl.Buffered(1))


def _mem_operands(g, wq, q_gain, k, v, kv_layer, wo, layer):
    specs = [_resident((1, D_MODEL), (0, 0)),
             _resident((None, D_MODEL, D_MEMATT), (layer, 0, 0)),
             _resident((1, D_MEMATT), (0, 0)),
             pl.BlockSpec((None, None, k.shape[2], HEAD_DIM), lambda i, j: (kv_layer, i, 0, 0)),
             pl.BlockSpec((None, None, k.shape[2], HEAD_DIM), lambda i, j: (kv_layer, i, 0, 0)),
             _resident((None, D_MEMATT, D_MODEL), (layer, 0, 0))]
    return specs, (g, wq, q_gain, k, v, wo)


def _mem_attn_kernel(x_ref, g_ref, wq_ref, qg_ref, k_ref, v_ref, wo_ref, y_ref):
    bb, tm, _ = x_ref.shape
    n_mem = k_ref.shape[1] // HEADS

    def kv_head(b, hd):
        rows = pl.ds(hd, n_mem, stride=HEADS)
        return k_ref[b, rows, :].astype(BF16), v_ref[b, rows, :].astype(BF16)

    x = x_ref[...].reshape(bb * tm, D_MODEL)
    y_ref[...] = _mem_attn_tail(x, bb, g_ref, wq_ref, qg_ref, kv_head, wo_ref).reshape(bb, tm, D_MODEL)


def mem_attn(x, g, wq, q_gain, k, v, kv_layer, wo, layer, bb, tm):
    b, t, _ = x.shape
    full = lambda shape: pl.BlockSpec(shape, lambda i, j: (0,) * len(shape))
    kv = pl.BlockSpec((None, bb, k.shape[2], HEAD_DIM), lambda i, j: (kv_layer, i, 0, 0))
    xs = pl.BlockSpec((bb, tm, D_MODEL), lambda i, j: (i, j, 0))
    return pl.pallas_call(
        _mem_attn_kernel,
        grid=(b // bb, t // tm),
        in_specs=[xs, full((1, D_MODEL)),
                  pl.BlockSpec((None, D_MODEL, D_MEMATT), lambda i, j: (layer, 0, 0)),
                  full((1, D_MEMATT)), kv, kv,
                  pl.BlockSpec((None, D_MEMATT, D_MODEL), lambda i, j: (layer, 0, 0))],
        out_specs=xs,
        out_shape=jax.ShapeDtypeStruct((b, t, D_MODEL), F32),
        compiler_params=_params(("parallel", "parallel"), 40),
        name="mem_attn",
    )(x, g, wq, q_gain, k, v, wo)


def _tile_heads(g):
    return jnp.tile(g, HEADS)


def _row_tile(m):
    return min(m, 1024)


def kernel(x_prompt, x_sample, mem_prompt, state_conv, cache_k_w128, cache_v_w128, cache_k_w512, cache_v_w512,
           cache_k_w2048, cache_v_w2048, cache_mem_k, cache_mem_v, g_mix, w_in_e, conv_w, conv_b, conv_ln_g,
           conv_ln_b, q_norm_e, k_norm_e, w_out_e, w_in_o, b_in_o, v_ln_g, v_ln_b, w_s, b_s, w_out_o, g_xmem,
           g_mem, wq_mem, wk_mem, wv_mem, q_norm_mem, k_norm_mem, wo_mem, g_ffn, w_ffn1, w_ffn2):
    depth = g_mix.shape[0]
    bp, tp, _ = x_prompt.shape
    bs, ts, _ = x_sample.shape
    mp, ms = bp * tp, bs * ts
    scale = HEAD_DIM ** -0.5
    row = lambda v: v.reshape(1, -1)
    th_rows = lambda c: c.reshape(c.shape[0], c.shape[1], c.shape[2] * HEADS, HEAD_DIM)
    caches = [th_rows(c) for c in (cache_k_w128, cache_v_w128, cache_k_w512, cache_v_w512,
                                   cache_k_w2048, cache_v_w2048)]
    mem_k_s, mem_v_s = th_rows(cache_mem_k), th_rows(cache_mem_v)

    w_in_e, w_out_e, w_in_o, w_out_o = (w.astype(BF16) for w in (w_in_e, w_out_e, w_in_o, w_out_o))
    wq_mem, wo_mem = wq_mem.astype(BF16), wo_mem.astype(BF16)
    wkv_mem = jnp.concatenate([wk_mem, wv_mem], axis=2).astype(BF16)

    xp = x_prompt.reshape(mp, D_MODEL)
    xs = x_sample.reshape(ms, D_MODEL)
    mem = mem_prompt.reshape(bp * N_MEM, D_MODEL)
    tm_p, tm_s, tm_mem = _row_tile(mp), _row_tile(ms), _row_tile(bp * N_MEM)
    assert tp % tm_p == 0 or tm_p % tp == 0
    assert ms == tm_s

    conv_pl, conv_sl, kv_pl, kv_sl, memk_pl, memv_pl, chunk_sl = [], [], [], [], [], [], []
    for i in range(depth):
        j = i // 2
        kgain = jnp.stack([_tile_heads(k_norm_mem[i]), jnp.ones((D_MEMATT,), F32)])[:, None, :]
        mk, mv = headnorm_proj(mem, row(g_mem[i]), wkv_mem, i, 0, kgain, 1, tm_mem,
                               [(0, tm_mem, 1), (1, tm_mem, 1)], False)
        memk_pl.append(mk.reshape(bp, N_MEM, HEADS, HEAD_DIM))
        memv_pl.append(mv.reshape(bp, N_MEM, HEADS, HEAD_DIM))
        qgain = row(_tile_heads(q_norm_mem[i]) * scale)
        mem_p = (row(g_xmem[i]), wq_mem, qgain, mk.reshape(1, bp, N_MEM * HEADS, HEAD_DIM),
                 mv.reshape(1, bp, N_MEM * HEADS, HEAD_DIM), 0, wo_mem, i)
        mem_s = (row(g_xmem[i]), wq_mem, qgain, mem_k_s, mem_v_s, i, wo_mem, i)

        if i % 2 == 0:
            gains = jnp.concatenate([
                jnp.stack([_tile_heads(q_norm_e[j, gi]) * scale for gi in range(N_DGROUPS)]),
                jnp.stack([_tile_heads(k_norm_e[j, gi]) for gi in range(N_DGROUPS)]),
                jnp.ones((N_DGROUPS, D_GRP), F32)])[:, None, :]
            conv_args = (conv_w[j], row(conv_b[j]), row(conv_ln_g[j]), row(conv_ln_b[j]))
            qkv_col0 = 2 * D_CONV // D_GRP

            def kv_tiles(seq, tm):
                tiles = []
                for gi, (win, _) in enumerate(DIL_GROUPS):
                    keep = min(win, seq)
                    assert keep == seq or (keep <= tm and seq % tm == 0)
                    spec = (tm, 1) if keep == seq else (keep, seq // tm)
                    tiles += [((1 + which) * N_DGROUPS + gi,) + spec for which in range(2)]
                return tiles

            a = glu_proj(xp, row(g_mix[i]), w_in_e, j, tm_p)
            qkv, *new_kv = headnorm_proj(xp, row(g_mix[i]), w_in_e, j, qkv_col0, gains, 2 * N_DGROUPS, tm_p,
                                         kv_tiles(tp, tm_p), True)
            kv_pl.append([kv.reshape(bp, -1, HEADS, HEAD_DIM) for kv in new_kv])
            a3 = a.reshape(bp, tp, D_CONV)
            conv_pl.append(a3[:, tp - (CONV_W - 1):])
            a_out = conv_prompt(a3, *conv_args, tt=512).reshape(mp, D_CONV)
            b_out = dilated_attn_prompt(qkv, bp, tp)
            xp = mix_out_mem(xp, a_out, b_out, w_out_e, j, mem_p, bp, 512)

            a = glu_proj(xs, row(g_mix[i]), w_in_e, j, tm_s)
            qkv, *new_kv = headnorm_proj(xs, row(g_mix[i]), w_in_e, j, qkv_col0, gains, 2 * N_DGROUPS, tm_s,
                                         kv_tiles(ts, tm_s), True)
            kv_sl.append([kv.reshape(bs, -1, HEADS, HEAD_DIM) for kv in new_kv])
            apad = jnp.concatenate([state_conv[j], a.reshape(bs, ts, D_CONV)], axis=1)
            conv_sl.append(apad[:, apad.shape[1] - (CONV_W - 1):])
            a_out = conv_sample(apad, *conv_args).reshape(ms, D_CONV)
            b_out = dilated_attn_sample(qkv.reshape(bs, ts, 3 * D_ATT), caches, j).reshape(ms, D_GRP)
            xs = mix_out(xs, a_out, b_out, w_out_e, j, tm_s)
        else:
            tm = 256
            sgu_vecs = (row(v_ln_g[j]), row(v_ln_b[j]))
            tril = jnp.tril(jnp.ones((CHUNK, CHUNK), F32))

            def spatial(c, rows):
                ws, bias = w_s[j][:, :c, :c] * tril[:c, :c], b_s[j][:, :c]
                if c < rows:
                    seq_of = jnp.arange(rows) // c
                    ws = jnp.tile(ws, (1, rows // c, rows // c)) * (seq_of[:, None] == seq_of[None, :])
                    bias = jnp.tile(bias, (1, rows // c))
                return ws.astype(BF16), jnp.broadcast_to(bias[:, :, None], bias.shape + (D_SG,))

            z = gelu_proj(xp, row(g_mix[i]), w_in_o, j, row(b_in_o[j]), tm_p)
            ws, bias = spatial(min(CHUNK, tp), min(CHUNK, tp))
            xp = sgu_out_mem(xp, z, *sgu_vecs, ws, bias, w_out_o, j, mem_p, bp, tm)

            z = gelu_proj(xs, row(g_mix[i]), w_in_o, j, row(b_in_o[j]), tm_s)
            ws, bias = spatial(min(CHUNK, ts), CHUNK if ts >= CHUNK else tm)
            xs, v = sgu_out(xs, z, *sgu_vecs, ws, bias, w_out_o, j, tm, emit_v=True)
            chunk_sl.append(v.reshape(bs, ts, D_GATE))

        xs = mem_attn(xs.reshape(bs, ts, D_MODEL), *mem_s, 8, ts).reshape(ms, D_MODEL)
        xs, w1, w2 = ffn_cast(xs, row(g_ffn[i]), w_ffn1, w_ffn2, i, 512)
        xp = ffn(xp, row(g_ffn[i]), w1, w2, 0, tm_p, 1024)

    stack = lambda items: jnp.stack(items)
    kv_p = [stack([kv[n] for kv in kv_pl]) for n in range(2 * N_DGROUPS)]
    kv_s = [stack([kv[n] for kv in kv_sl]) for n in range(2 * N_DGROUPS)]
    return (xp.reshape(bp, tp, D_MODEL), xs.reshape(bs, ts, D_MODEL), stack(conv_pl), stack(conv_sl),
            *kv_p, *kv_s, stack(memk_pl), stack(memv_pl), stack(chunk_sl))
```

```python
import functools
import math

import numpy as np
import jax
import jax.numpy as jnp
from jax import lax
from jax.experimental import pallas as pl
from jax.experimental.pallas import tpu as pltpu

D_MODEL = 2048
EPS = 1e-6
NEG = -1e30
D_CONV = D_MODEL // 2
CONV_W = 31
HEAD_DIM = 128
DIL_GROUPS = ((128, 1), (512, 4), (2048, 16))
N_DGROUPS = len(DIL_GROUPS)
HEADS = 4
D_GRP = HEADS * HEAD_DIM
D_ATT = N_DGROUPS * D_GRP
CHUNK = 128
D_GATE = D_MODEL
N_SG = 8
D_SG = D_GATE // N_SG
N_MEM = 256
D_MEMATT = HEADS * HEAD_DIM
D_FF = 4 * D_MODEL

F32 = jnp.float32
BF16 = jnp.bfloat16
MIB = 1024 * 1024
NORM_ROWS = 256
CONV_HALO = 32
SUBLANES = 8
MERGE_ROWS = 256
BAND = DIL_GROUPS[0][0] // DIL_GROUPS[0][1]
GATHER_STRIDE = 4
ATTN_UNROLL = 16
CONV_ROWS = 16
LANES = 128


def _params(sem, vmem_mib):
    return pltpu.CompilerParams(dimension_semantics=sem, vmem_limit_bytes=vmem_mib * MIB)


def _dot(a, b):
    return jnp.dot(a, b, preferred_element_type=F32)


def _dot_nt(a, b):
    return lax.dot_general(a, b, (((1,), (1,)), ((), ())), preferred_element_type=F32)


def _rms(x):
    return x * lax.rsqrt(jnp.mean(x * x, axis=-1, keepdims=True) + EPS)


def _norm_to_scratch(x_ref, g_ref, h_ref):
    rows = x_ref.shape[0]
    step = min(NORM_ROWS, rows)

    def body(c, carry):
        r = pl.multiple_of(c * step, step)
        x = x_ref[pl.ds(r, step), :]
        h_ref[pl.ds(r, step), :] = (_rms(x) * g_ref[...]).astype(BF16)
        return carry

    lax.fori_loop(0, rows // step, body, 0)


def _head_norm(acc, gain):
    parts = [_rms(acc[:, h * HEAD_DIM:(h + 1) * HEAD_DIM]) for h in range(acc.shape[1] // HEAD_DIM)]
    return jnp.concatenate(parts, axis=-1) * gain


def _head_rows(ref, h, rows):
    return ref[pl.ds(h, rows, stride=HEADS), :]


def _softmax_pv(s, v):
    mx = jnp.max(s, axis=-1, keepdims=True)
    p = jnp.exp(s - mx)
    l = jnp.sum(p, axis=-1, keepdims=True)
    return _dot(p.astype(BF16), v) / l, mx + jnp.log(l)


def _merge3(outs, lses):
    mx = jnp.maximum(jnp.maximum(lses[0], lses[1]), lses[2])
    es = [jnp.exp(l - mx) for l in lses]
    return (es[0] * outs[0] + es[1] * outs[1] + es[2] * outs[2]) / (es[0] + es[1] + es[2])


def _glu_kernel(x_ref, g_ref, wv_ref, wg_ref, o_ref, h_ref):
    @pl.when(pl.program_id(1) == 0)
    def _():
        _norm_to_scratch(x_ref, g_ref, h_ref)

    h = h_ref[...]
    val = _dot(h, wv_ref[...])
    gate = _dot(h, wg_ref[...])
    o_ref[...] = val * jax.nn.sigmoid(gate)


def glu_proj(x, g, w_in, layer, tm):
    m = x.shape[0]
    tn = D_CONV
    nj = D_CONV // tn
    return pl.pallas_call(
        _glu_kernel,
        grid=(m // tm, nj),
        in_specs=[
            pl.BlockSpec((tm, D_MODEL), lambda i, j: (i, 0)),
            pl.BlockSpec((1, D_MODEL), lambda i, j: (0, 0)),
            pl.BlockSpec((None, D_MODEL, tn), lambda i, j: (layer, 0, j)),
            pl.BlockSpec((None, D_MODEL, tn), lambda i, j: (layer, 0, j + nj)),
        ],
        out_specs=pl.BlockSpec((tm, tn), lambda i, j: (i, j)),
        out_shape=jax.ShapeDtypeStruct((m, D_CONV), F32),
        scratch_shapes=[pltpu.VMEM((tm, D_MODEL), BF16)],
        compiler_params=_params(("parallel", "arbitrary"), 48),
        name="glu_proj",
    )(x, g, w_in, w_in)


def _headnorm_proj_kernel(x_ref, g_ref, w_ref, gain_ref, *rest, n_normed, emit_main, th_tiles):
    n_th = len(th_tiles)
    if emit_main:
        o_ref, th_refs, h_ref = rest[0], rest[1:1 + n_th], rest[1 + n_th]
    else:
        th_refs, h_ref, o_ref = rest[:n_th], rest[n_th], rest[n_th + 1]
    i, j = pl.program_id(0), pl.program_id(1)
    tm = x_ref.shape[0]

    @pl.when(j == 0)
    def _():
        _norm_to_scratch(x_ref, g_ref, h_ref)

    acc = _dot(h_ref[...], w_ref[...])
    o_ref[...] = jnp.where(j < n_normed, _head_norm(acc, gain_ref[0]), acc)

    for th_ref, (col, period) in zip(th_refs, th_tiles):
        rows = th_ref.shape[0] // HEADS

        @pl.when((j == col) & (i % period == period - 1))
        def _():
            for h in range(HEADS):
                th_ref[pl.ds(h, rows, stride=HEADS), :] = o_ref[tm - rows:, h * HEAD_DIM:(h + 1) * HEAD_DIM]


def _th_index(period, i, j):
    return (i // period, 0)


def headnorm_proj(x, g, w, layer, col_block0, gains, n_normed, tm, th_tiles, emit_main):
    m = x.shape[0]
    nj = gains.shape[0]
    out_specs, out_shape = [], []
    if emit_main:
        out_specs.append(pl.BlockSpec((tm, D_GRP), lambda i, j: (i, j)))
        out_shape.append(jax.ShapeDtypeStruct((m, nj * D_GRP), F32))
    for _, keep, period in th_tiles:
        assert keep <= tm and (m // tm) % period == 0
        out_specs.append(pl.BlockSpec((keep * HEADS, HEAD_DIM), functools.partial(_th_index, period)))
        out_shape.append(jax.ShapeDtypeStruct((m // tm // period * keep * HEADS, HEAD_DIM), F32))
    scratch = [pltpu.VMEM((tm, D_MODEL), BF16)]
    if not emit_main:
        scratch.append(pltpu.VMEM((tm, D_GRP), F32))
    return pl.pallas_call(
        functools.partial(_headnorm_proj_kernel, n_normed=n_normed, emit_main=emit_main,
                          th_tiles=tuple((col, period) for col, _, period in th_tiles)),
        grid=(m // tm, nj),
        in_specs=[
            pl.BlockSpec((tm, D_MODEL), lambda i, j: (i, 0)),
            pl.BlockSpec((1, D_MODEL), lambda i, j: (0, 0)),
            pl.BlockSpec((None, D_MODEL, D_GRP), lambda i, j: (layer, 0, j + col_block0)),
            pl.BlockSpec((1, 1, D_GRP), lambda i, j: (j, 0, 0)),
        ],
        out_specs=out_specs,
        out_shape=out_shape,
        scratch_shapes=scratch,
        compiler_params=_params(("arbitrary", "arbitrary"), 52),
        name="headnorm_proj",
    )(x, g, w, gains)


def _gelu_proj_kernel(x_ref, g_ref, w_ref, b_ref, o_ref, h_ref):
    @pl.when(pl.program_id(1) == 0)
    def _():
        _norm_to_scratch(x_ref, g_ref, h_ref)

    z = _dot(h_ref[...], w_ref[...]) + b_ref[...]
    o_ref[...] = 0.5 * z * (1.0 + lax.erf(z * np.float32(math.sqrt(0.5))))


def gelu_proj(x, g, w, layer, b, tm):
    m = x.shape[0]
    n = w.shape[2]
    tn = 1024
    return pl.pallas_call(
        _gelu_proj_kernel,
        grid=(m // tm, n // tn),
        in_specs=[
            pl.BlockSpec((tm, D_MODEL), lambda i, j: (i, 0)),
            pl.BlockSpec((1, D_MODEL), lambda i, j: (0, 0)),
            pl.BlockSpec((None, D_MODEL, tn), lambda i, j: (layer, 0, j)),
            pl.BlockSpec((1, tn), lambda i, j: (0, j)),
        ],
        out_specs=pl.BlockSpec((tm, tn), lambda i, j: (i, j)),
        out_shape=jax.ShapeDtypeStruct((m, n), F32),
        scratch_shapes=[pltpu.VMEM((tm, D_MODEL), BF16)],
        compiler_params=_params(("parallel", "arbitrary"), 48),
        name="gelu_proj",
    )(x, g, w, b)


def _ffn_kernel(x_ref, g_ref, w1_ref, w2_ref, o_ref, h_ref):
    @pl.when(pl.program_id(1) == 0)
    def _():
        _norm_to_scratch(x_ref, g_ref, h_ref)
        o_ref[...] = x_ref[...]

    hid = jnp.maximum(_dot(h_ref[...], w1_ref[...]), 0.0)
    o_ref[...] += _dot((hid * hid).astype(BF16), w2_ref[...])


def ffn(x, g, w1, w2, layer, tm, tf):
    m = x.shape[0]
    return pl.pallas_call(
        _ffn_kernel,
        grid=(m // tm, D_FF // tf),
        in_specs=[
            pl.BlockSpec((tm, D_MODEL), lambda i, f: (i, 0)),
            pl.BlockSpec((1, D_MODEL), lambda i, f: (0, 0)),
            pl.BlockSpec((None, D_MODEL, tf), lambda i, f: (layer, 0, f)),
            pl.BlockSpec((None, tf, D_MODEL), lambda i, f: (layer, f, 0)),
        ],
        out_specs=pl.BlockSpec((tm, D_MODEL), lambda i, f: (i, 0)),
        out_shape=jax.ShapeDtypeStruct((m, D_MODEL), F32),
        scratch_shapes=[pltpu.VMEM((tm, D_MODEL), BF16)],
        compiler_params=_params(("parallel", "arbitrary"), 60),
        name="ffn",
    )(x, g, w1, w2)


def _ffn_cast_kernel(x_ref, g_ref, w1_ref, w2_ref, o_ref, w1b_ref, w2b_ref, h_ref):
    @pl.when(pl.program_id(0) == 0)
    def _():
        _norm_to_scratch(x_ref, g_ref, h_ref)
        o_ref[...] = x_ref[...]

    w1 = w1_ref[...].astype(BF16)
    w2 = w2_ref[...].astype(BF16)
    w1b_ref[...] = w1
    w2b_ref[...] = w2
    hid = jnp.maximum(_dot(h_ref[...], w1), 0.0)
    o_ref[...] += _dot((hid * hid).astype(BF16), w2)


def ffn_cast(x, g, w1, w2, layer, tf):
    m = x.shape[0]
    const = lambda shape: pl.BlockSpec(shape, lambda f: (0,) * len(shape))
    return pl.pallas_call(
        _ffn_cast_kernel,
        grid=(D_FF // tf,),
        in_specs=[const((m, D_MODEL)), const((1, D_MODEL)),
                  pl.BlockSpec((None, D_MODEL, tf), lambda f: (layer, 0, f)),
                  pl.BlockSpec((None, tf, D_MODEL), lambda f: (layer, f, 0))],
        out_specs=[const((m, D_MODEL)),
                   pl.BlockSpec((None, D_MODEL, tf), lambda f: (0, 0, f)),
                   pl.BlockSpec((None, tf, D_MODEL), lambda f: (0, f, 0))],
        out_shape=[jax.ShapeDtypeStruct((m, D_MODEL), F32),
                   jax.ShapeDtypeStruct((1, D_MODEL, D_FF), BF16),
                   jax.ShapeDtypeStruct((1, D_FF, D_MODEL), BF16)],
        scratch_shapes=[pltpu.VMEM((m, D_MODEL), BF16)],
        compiler_params=_params(("arbitrary",), 48),
        name="ffn_cast",
    )(x, g, w1, w2)


def _ln_silu(acc, lg_ref, lb_ref):
    mu = jnp.mean(acc, axis=-1, keepdims=True)
    xc = acc - mu
    y = xc * lax.rsqrt(jnp.mean(xc * xc, axis=-1, keepdims=True) + EPS)
    y = y * lg_ref[...] + lb_ref[...]
    return y * jax.nn.sigmoid(y)


def _conv_prompt_kernel(a_ref, halo_ref, w_ref, cb_ref, o_ref, sh_ref, wrep_ref):
    tt = a_ref.shape[0]
    first = pl.program_id(1) == 0
    for k in range(CONV_W):
        wrep_ref[k] = jnp.broadcast_to(w_ref[k:k + 1, :], (SUBLANES, D_CONV))
    sh_ref[0, 0:CONV_HALO, :] = jnp.where(first, 0.0, halo_ref[...])
    sh_ref[0, CONV_HALO:, :] = a_ref[...]
    span = tt + CONV_HALO - SUBLANES
    for s in range(1, SUBLANES):
        sh_ref[s, 0:span, :] = sh_ref[0, s:s + span, :]
    lead = CONV_HALO - (CONV_W - 1)
    groups = CONV_ROWS // SUBLANES

    def conv_rows(c, carry):
        r = pl.multiple_of(c * CONV_ROWS, CONV_ROWS)
        acc = jnp.zeros((groups, SUBLANES, D_CONV), F32) + cb_ref[...]
        for k in range(CONV_W):
            q, s = divmod(k + lead, SUBLANES)
            x = sh_ref[s, pl.ds(r + q * SUBLANES, CONV_ROWS), :]
            acc = acc + x.reshape(groups, SUBLANES, D_CONV) * wrep_ref[k]
        o_ref[pl.ds(r, CONV_ROWS), :] = acc.reshape(CONV_ROWS, D_CONV)
        return carry

    lax.fori_loop(0, tt // CONV_ROWS, conv_rows, 0)


def conv_prompt(a, w, cb, tt):
    b, t, _ = a.shape
    hb = tt // CONV_HALO
    vec = pl.BlockSpec((1, D_CONV), lambda i, j: (0, 0))
    return pl.pallas_call(
        _conv_prompt_kernel,
        grid=(b, t // tt),
        in_specs=[
            pl.BlockSpec((None, tt, D_CONV), lambda i, j: (i, j, 0)),
            pl.BlockSpec((None, CONV_HALO, D_CONV), lambda i, j: (i, jnp.maximum(j * hb - 1, 0), 0)),
            pl.BlockSpec((CONV_W, D_CONV), lambda i, j: (0, 0)),
            vec,
        ],
        out_specs=pl.BlockSpec((None, tt, D_CONV), lambda i, j: (i, j, 0)),
        out_shape=jax.ShapeDtypeStruct((b, t, D_CONV), F32),
        scratch_shapes=[pltpu.VMEM((SUBLANES, CONV_HALO + tt, D_CONV), F32),
                        pltpu.VMEM((CONV_W, SUBLANES, D_CONV), F32)],
        compiler_params=_params(("parallel", "arbitrary"), 40),
        name="conv_prompt",
    )(a, a, w, cb)


def _conv_sample_kernel(apad_ref, w_ref, cb_ref, lg_ref, lb_ref, o_ref):
    rows = o_ref.shape[0]
    acc = jnp.zeros((rows, D_CONV), F32) + cb_ref[...]
    for k in range(CONV_W):
        acc = acc + apad_ref[k:k + rows, :] * w_ref[k:k + 1, :]
    o_ref[...] = _ln_silu(acc, lg_ref, lb_ref).astype(o_ref.dtype)


def conv_sample(apad, w, cb, lg, lb):
    b, tp, _ = apad.shape
    t = tp - (CONV_W - 1)
    vec = pl.BlockSpec((1, D_CONV), lambda i: (0, 0))
    return pl.pallas_call(
        _conv_sample_kernel,
        grid=(b,),
        in_specs=[
            pl.BlockSpec((None, tp, D_CONV), lambda i: (i, 0, 0)),
            pl.BlockSpec((CONV_W, D_CONV), lambda i: (0, 0)),
            vec, vec, vec,
        ],
        out_specs=pl.BlockSpec((None, t, D_CONV), lambda i: (i, 0, 0)),
        out_shape=jax.ShapeDtypeStruct((b, t, D_CONV), BF16),
        compiler_params=_params(("parallel",), 32),
        name="conv_sample",
    )(apad, w, cb, lg, lb)


def _dil_prompt_kernel(*refs):
    qkv_refs = refs[:3 * N_DGROUPS]
    b_ref, qbuf, kbuf, vbuf, o_sc, l_sc, stage = refs[3 * N_DGROUPS:]
    t = b_ref.shape[0]
    n = BAND
    qi = lax.broadcasted_iota(jnp.int32, (n, 2 * n), 0)
    kj = lax.broadcasted_iota(jnp.int32, (n, 2 * n), 1)
    band = (kj > qi) & (kj <= qi + n)

    for gi, (win, dil) in enumerate(DIL_GROUPS):
        assert win // dil == n
        q_ref, k_ref, v_ref = qkv_refs[3 * gi:3 * gi + 3]
        s_len = t // dil
        n_blk = s_len // n
        pitch = s_len + n

        def place(r, q_rows, k_rows, v_rows, s_len=s_len, pitch=pitch):
            k0 = pl.multiple_of(r * pitch, n)
            qbuf[pl.ds(pl.multiple_of(r * s_len, n), s_len), :] = q_rows.astype(BF16)
            kbuf[pl.ds(k0, n), :] = jnp.zeros((n, HEAD_DIM), BF16)
            vbuf[pl.ds(k0, n), :] = jnp.zeros((n, HEAD_DIM), BF16)
            kbuf[pl.ds(k0 + n, s_len), :] = k_rows.astype(BF16)
            vbuf[pl.ds(k0 + n, s_len), :] = v_rows.astype(BF16)

        if dil > GATHER_STRIDE and dil % GATHER_STRIDE == 0:
            outer = dil // GATHER_STRIDE
            mid = t // GATHER_STRIDE

            def gather(r_in, carry, refs=(q_ref, k_ref, v_ref), s_len=s_len, outer=outer, mid=mid, place=place):
                for w, ref in enumerate(refs):
                    stage[w, 0:mid, :] = ref[pl.ds(r_in, mid, stride=GATHER_STRIDE), :]
                for m in range(outer):
                    place(r_in + m * GATHER_STRIDE,
                          *[stage[w, pl.ds(m, s_len, stride=outer), :] for w in range(3)])
                return carry

            lax.fori_loop(0, GATHER_STRIDE, gather, 0)
        else:
            def gather(r, carry, refs=(q_ref, k_ref, v_ref), dil=dil, s_len=s_len, place=place):
                rows = pl.ds(r, s_len, stride=dil) if dil > 1 else pl.ds(0, s_len)
                place(r, *[ref[rows, :] for ref in refs])
                return carry

            lax.fori_loop(0, dil, gather, 0)

        def units(it, carry, dil=dil, n_blk=n_blk, pitch=pitch, gi=gi):
            for j in range(ATTN_UNROLL):
                u = it * ATTN_UNROLL + j
                if n_blk == 1:
                    r, blk = u, 0
                elif dil == 1:
                    r, blk = 0, u
                else:
                    r, blk = lax.div(u, jnp.int32(n_blk)), lax.rem(u, jnp.int32(n_blk))
                q = qbuf[pl.ds(pl.multiple_of(u * n, n), n), :]
                k0 = pl.multiple_of(r * pitch + blk * n, n)
                mask = band & (kj >= jnp.where(blk == 0, n, 0))
                s = jnp.where(mask, _dot_nt(q, kbuf[pl.ds(k0, 2 * n), :]), NEG)
                o, lse = _softmax_pv(s, vbuf[pl.ds(k0, 2 * n), :])
                start = r + blk * (n * dil)
                dst = pl.ds(start, n, stride=dil) if dil > 1 else pl.ds(pl.multiple_of(start, n), n)
                o_sc[gi, dst, :] = o
                l_sc[gi, dst, :] = jnp.broadcast_to(lse, (n, HEAD_DIM))
            return carry

        assert (dil * n_blk) % ATTN_UNROLL == 0
        lax.fori_loop(0, dil * n_blk // ATTN_UNROLL, units, 0)

    def merge(c, carry):
        rows = pl.ds(pl.multiple_of(c * MERGE_ROWS, MERGE_ROWS), MERGE_ROWS)
        outs = [o_sc[gi, rows, :] for gi in range(N_DGROUPS)]
        lses = [l_sc[gi, rows, :] for gi in range(N_DGROUPS)]
        b_ref[rows, :] = _merge3(outs, lses).astype(b_ref.dtype)
        return carry

    lax.fori_loop(0, t // MERGE_ROWS, merge, 0)


def dilated_attn_prompt(qkv, batch, seq):
    assert seq % (BAND * max(d for _, d in DIL_GROUPS)) == 0 and seq % MERGE_ROWS == 0
    kv_rows = max(seq + dil * BAND for _, dil in DIL_GROUPS)

    def spec(which, gi):
        return pl.BlockSpec((seq, HEAD_DIM), lambda b, h: (b, (which * N_DGROUPS + gi) * HEADS + h))

    in_specs = [spec(which, gi) for gi in range(N_DGROUPS) for which in range(3)]
    return pl.pallas_call(
        _dil_prompt_kernel,
        grid=(batch, HEADS),
        in_specs=in_specs,
        out_specs=pl.BlockSpec((seq, HEAD_DIM), lambda b, h: (b, h)),
        out_shape=jax.ShapeDtypeStruct((batch * seq, D_GRP), BF16),
        scratch_shapes=[pltpu.VMEM((seq, HEAD_DIM), BF16), pltpu.VMEM((kv_rows, HEAD_DIM), BF16),
                        pltpu.VMEM((kv_rows, HEAD_DIM), BF16),
                        pltpu.VMEM((N_DGROUPS, seq, HEAD_DIM), F32), pltpu.VMEM((N_DGROUPS, seq, HEAD_DIM), F32),
                        pltpu.VMEM((3, seq // GATHER_STRIDE, HEAD_DIM), F32)],
        compiler_params=_params(("parallel", "parallel"), 40),
        name="dil_attn_prompt",
    )(*([qkv] * (3 * N_DGROUPS)))


def _dil_sample_kernel(q_ref, kn_ref, vn_ref, *rest):
    cache_refs, b_ref = rest[:2 * N_DGROUPS], rest[2 * N_DGROUPS]
    ds = q_ref.shape[0]
    qn = lax.broadcasted_iota(jnp.int32, (ds, ds), 0)
    pn = lax.broadcasted_iota(jnp.int32, (ds, ds), 1)
    for h in range(HEADS):
        outs, lses = [], []
        for gi, (win, dil) in enumerate(DIL_GROUPS):
            n = win // dil
            kc_ref, vc_ref = cache_refs[2 * gi], cache_refs[2 * gi + 1]
            cache_len = kc_ref.shape[0] // HEADS
            cols = slice(gi * D_GRP + h * HEAD_DIM, gi * D_GRP + (h + 1) * HEAD_DIM)
            qi = lax.broadcasted_iota(jnp.int32, (ds, cache_len), 0)
            pj = lax.broadcasted_iota(jnp.int32, (ds, cache_len), 1)
            dist = cache_len + qi - pj
            mask_c = (dist % dil == 0) & (dist <= dil * (n - 1))
            mask_n = (qn >= pn) & ((qn - pn) % dil == 0)
            q = q_ref[:, cols].astype(BF16)
            s_c = jnp.where(mask_c, _dot_nt(q, _head_rows(kc_ref, h, cache_len).astype(BF16)), NEG)
            s_n = jnp.where(mask_n, _dot_nt(q, kn_ref[:, cols].astype(BF16)), NEG)
            mx = jnp.maximum(jnp.max(s_c, axis=-1, keepdims=True), jnp.max(s_n, axis=-1, keepdims=True))
            p_c = jnp.exp(s_c - mx)
            p_n = jnp.exp(s_n - mx)
            l = jnp.sum(p_c, axis=-1, keepdims=True) + jnp.sum(p_n, axis=-1, keepdims=True)
            o = (_dot(p_c.astype(BF16), _head_rows(vc_ref, h, cache_len).astype(BF16))
                 + _dot(p_n.astype(BF16), vn_ref[:, cols].astype(BF16)))
            outs.append(o / l)
            lses.append(mx + jnp.log(l))
        b_ref[:, h * HEAD_DIM:(h + 1) * HEAD_DIM] = _merge3(outs, lses).astype(b_ref.dtype)


def dilated_attn_sample(qkv, caches, layer):
    b, ds, _ = qkv.shape
    cache_specs = []
    for gi, (win, dil) in enumerate(DIL_GROUPS):
        rows = caches[2 * gi].shape[2]
        assert rows // HEADS - dil * (win // dil - 1) >= 0
        cache_specs += [pl.BlockSpec((None, None, rows, HEAD_DIM), lambda i: (layer, i, 0, 0))] * 2

    new = lambda which: pl.BlockSpec((None, ds, D_ATT), lambda i: (i, 0, which))
    return pl.pallas_call(
        _dil_sample_kernel,
        grid=(b,),
        in_specs=[new(0), new(1), new(2)] + cache_specs,
        out_specs=pl.BlockSpec((None, ds, D_GRP), lambda i: (i, 0, 0)),
        out_shape=jax.ShapeDtypeStruct((b, ds, D_GRP), BF16),
        compiler_params=_params(("parallel",), 48),
        name="dil_attn_sample",
    )(qkv, qkv, qkv, *caches)


def _mix_out_kernel(x_ref, a_ref, b_ref, wa_ref, wb_ref, y_ref):
    y_ref[...] = x_ref[...] + _dot(a_ref[...], wa_ref[...]) + _dot(b_ref[...], wb_ref[...])


def mix_out(x, a, b, w_out, layer, tm):
    m = x.shape[0]
    row = lambda width: pl.BlockSpec((tm, width), lambda i: (i, 0))
    return pl.pallas_call(
        _mix_out_kernel,
        grid=(m // tm,),
        in_specs=[row(D_MODEL), row(D_CONV), row(D_GRP),
                  pl.BlockSpec((None, D_CONV, D_MODEL), lambda i: (layer, 0, 0)),
                  pl.BlockSpec((None, D_GRP, D_MODEL), lambda i: (layer, D_CONV // D_GRP, 0))],
        out_specs=row(D_MODEL),
        out_shape=jax.ShapeDtypeStruct((m, D_MODEL), F32),
        compiler_params=_params(("parallel",), 48),
        name="mix_out",
    )(x, a, b, w_out, w_out)


def _mix_mem_kernel(x_ref, c_ref, lg_ref, lb_ref, b_ref, wa_ref, wb_ref,
                    g_ref, wq_ref, qg_ref, k_ref, v_ref, wo_ref, y_ref):
    a = _ln_silu(c_ref[...], lg_ref, lb_ref).astype(BF16)
    x1 = x_ref[...] + _dot(a, wa_ref[...]) + _dot(b_ref[...], wb_ref[...])
    y_ref[...] = _mem_attn_tail(x1, 1, g_ref, wq_ref, qg_ref, _kv_head_of(k_ref, v_ref), wo_ref)


def mix_out_mem(x, c, lg, lb, b, w_out, layer, mem_args, batch, tm):
    m = x.shape[0]
    nt = m // batch // tm
    row = lambda width: pl.BlockSpec((tm, width), lambda i, j: (i * nt + j, 0))
    vec = _resident((1, D_CONV), (0, 0))
    mem_specs, mem_ops = _mem_operands(*mem_args)
    return pl.pallas_call(
        _mix_mem_kernel,
        grid=(batch, nt),
        in_specs=[row(D_MODEL), row(D_CONV), vec, vec, row(D_GRP),
                  _resident((None, D_CONV, D_MODEL), (layer, 0, 0)),
                  _resident((None, D_GRP, D_MODEL), (layer, D_CONV // D_GRP, 0))] + mem_specs,
        out_specs=row(D_MODEL),
        out_shape=jax.ShapeDtypeStruct((m, D_MODEL), F32),
        compiler_params=_params(("parallel", "parallel"), 56),
        name="mix_out_mem",
    )(x, c, lg, lb, b, w_out, w_out, *mem_ops)


def _sgu_gated(u_ref, gv_ref, lg_ref, lb_ref, ws_ref, bs_ref):
    gv = gv_ref[...]
    mu = jnp.mean(gv, axis=-1, keepdims=True)
    vc = gv - mu
    v = vc * lax.rsqrt(jnp.mean(vc * vc, axis=-1, keepdims=True) + EPS) * lg_ref[...] + lb_ref[...]
    vb = v.astype(BF16)
    rows, cm = gv.shape[0], ws_ref.shape[1]
    gated = []
    for g in range(N_SG):
        cols = slice(g * D_SG, (g + 1) * D_SG)
        bias = bs_ref[:, g:g + 1]
        if cm == SUBLANES:
            v3 = vb[:, cols].astype(F32).reshape(rows // cm, cm, D_SG)
            wsg = ws_ref[g].astype(F32)
            sv = bias[None] + sum(v3[:, s:s + 1, :] * wsg[:, s:s + 1][None] for s in range(cm))
            sv = sv.reshape(rows, D_SG)
        else:
            sv = jnp.concatenate([_dot(ws_ref[g], vb[c * cm:(c + 1) * cm, cols]) + bias
                                  for c in range(rows // cm)], axis=0)
        gated.append((u_ref[:, cols] * sv).astype(BF16))
    return jnp.concatenate(gated, axis=-1), v


def _sgu_kernel(x_ref, u_ref, gv_ref, lg_ref, lb_ref, ws_ref, bs_ref, w_ref, *out_refs, emit_v):
    gated, v = _sgu_gated(u_ref, gv_ref, lg_ref, lb_ref, ws_ref, bs_ref)
    if emit_v:
        out_refs[1][...] = v
    out_refs[0][...] = x_ref[...] + _dot(gated, w_ref[...])


def _sgu_mem_kernel(x_ref, u_ref, gv_ref, lg_ref, lb_ref, ws_ref, bs_ref, w_ref,
                    g_ref, wq_ref, qg_ref, k_ref, v_ref, wo_ref, y_ref):
    gated, _ = _sgu_gated(u_ref, gv_ref, lg_ref, lb_ref, ws_ref, bs_ref)
    x1 = x_ref[...] + _dot(gated, w_ref[...])
    y_ref[...] = _mem_attn_tail(x1, 1, g_ref, wq_ref, qg_ref, _kv_head_of(k_ref, v_ref), wo_ref)


def sgu_out_mem(x, z, lg, lb, ws, bs, w_out, layer, mem_args, batch, tm):
    m = x.shape[0]
    cm = ws.shape[1]
    nt = m // batch // tm
    assert tm % cm == 0
    row = lambda jblk: pl.BlockSpec((tm, D_GATE), lambda i, j: (i * nt + j, jblk))
    vec = pl.BlockSpec((1, D_GATE), lambda i, j: (0, 0))
    mem_specs, mem_ops = _mem_operands(*mem_args)
    return pl.pallas_call(
        _sgu_mem_kernel,
        grid=(batch, nt),
        in_specs=[row(0), row(0), row(1), vec, vec,
                  _resident((N_SG, cm, cm), (0, 0, 0)), _resident((cm, N_SG), (0, 0)),
                  _resident((None, D_GATE, D_MODEL), (layer, 0, 0))] + mem_specs,
        out_specs=row(0),
        out_shape=jax.ShapeDtypeStruct((m, D_MODEL), F32),
        compiler_params=_params(("parallel", "parallel"), 56),
        name="sgu_out_mem",
    )(x, z, z, lg, lb, ws, bs, w_out, *mem_ops)


def sgu_out(x, z, lg, lb, ws, bs, w_out, layer, tm, emit_v):
    m = x.shape[0]
    cm = ws.shape[1]
    assert tm % cm == 0
    row = lambda jblk: pl.BlockSpec((tm, D_GATE), lambda i: (i, jblk))
    vec = pl.BlockSpec((1, D_GATE), lambda i: (0, 0))
    out_specs = [row(0)]
    out_shape = [jax.ShapeDtypeStruct((m, D_MODEL), F32)]
    if emit_v:
        out_specs.append(row(0))
        out_shape.append(jax.ShapeDtypeStruct((m, D_GATE), F32))
    res = pl.pallas_call(
        functools.partial(_sgu_kernel, emit_v=emit_v),
        grid=(m // tm,),
        in_specs=[row(0), row(0), row(1), vec, vec,
                  pl.BlockSpec((N_SG, cm, cm), lambda i: (0, 0, 0)),
                  pl.BlockSpec((cm, N_SG), lambda i: (0, 0)),
                  pl.BlockSpec((None, D_GATE, D_MODEL), lambda i: (layer, 0, 0))],
        out_specs=out_specs,
        out_shape=out_shape,
        compiler_params=_params(("parallel",), 52),
        name="sgu_out",
    )(x, z, z, lg, lb, ws, bs, w_out)
    return res if emit_v else (res[0], None)


def _mem_attn_tail(x, n_seq, g_ref, wq_ref, qg_ref, kv_head, wo_ref):
    tm = x.shape[0] // n_seq
    h = (_rms(x) * g_ref[...]).astype(BF16)
    q = _head_norm(_dot(h, wq_ref[...]), qg_ref[...]).astype(BF16)
    per_seq = []
    for b in range(n_seq):
        outs = []
        for hd in range(HEADS):
            k, v = kv_head(b, hd)
            o, _ = _softmax_pv(_dot_nt(q[b * tm:(b + 1) * tm, hd * HEAD_DIM:(hd + 1) * HEAD_DIM], k), v)
            outs.append(o.astype(BF16))
        per_seq.append(jnp.concatenate(outs, axis=-1))
    o = jnp.concatenate(per_seq, axis=0) if n_seq > 1 else per_seq[0]
    return x + _dot(o, wo_ref[...])


def _kv_head_of(k_ref, v_ref):
    n_mem = k_ref.shape[0] // HEADS
    return lambda b, hd: (_head_rows(k_ref, hd, n_mem).astype(BF16), _head_rows(v_ref, hd, n_mem).astype(BF16))


def _resident(shape, index):
    return pl.BlockSpec(shape, lambda *_: index, pipeline_mode=pl.Buffered(1))


def _mem_operands(g, wq, q_gain, k, v, kv_layer, wo, layer):
    specs = [_resident((1, D_MODEL), (0, 0)),
             _resident((None, D_MODEL, D_MEMATT), (layer, 0, 0)),
             _resident((1, D_MEMATT), (0, 0)),
             pl.BlockSpec((None, None, k.shape[2], HEAD_DIM), lambda i, j: (kv_layer, i, 0, 0)),
             pl.BlockSpec((None, None, k.shape[2], HEAD_DIM), lambda i, j: (kv_layer, i, 0, 0)),
             _resident((None, D_MEMATT, D_MODEL), (layer, 0, 0))]
    return specs, (g, wq, q_gain, k, v, wo)


def _mem_attn_kernel(x_ref, g_ref, wq_ref, qg_ref, k_ref, v_ref, wo_ref, y_ref):
    bb, tm, _ = x_ref.shape
    n_mem = k_ref.shape[1] // HEADS

    def kv_head(b, hd):
        rows = pl.ds(hd, n_mem, stride=HEADS)
        return k_ref[b, rows, :].astype(BF16), v_ref[b, rows, :].astype(BF16)

    x = x_ref[...].reshape(bb * tm, D_MODEL)
    y_ref[...] = _mem_attn_tail(x, bb, g_ref, wq_ref, qg_ref, kv_head, wo_ref).reshape(bb, tm, D_MODEL)


def mem_attn(x, g, wq, q_gain, k, v, kv_layer, wo, layer, bb, tm):
    b, t, _ = x.shape
    full = lambda shape: pl.BlockSpec(shape, lambda i, j: (0,) * len(shape))
    kv = pl.BlockSpec((None, bb, k.shape[2], HEAD_DIM), lambda i, j: (kv_layer, i, 0, 0))
    xs = pl.BlockSpec((bb, tm, D_MODEL), lambda i, j: (i, j, 0))
    return pl.pallas_call(
        _mem_attn_kernel,
        grid=(b // bb, t // tm),
        in_specs=[xs, full((1, D_MODEL)),
                  pl.BlockSpec((None, D_MODEL, D_MEMATT), lambda i, j: (layer, 0, 0)),
                  full((1, D_MEMATT)), kv, kv,
                  pl.BlockSpec((None, D_MEMATT, D_MODEL), lambda i, j: (layer, 0, 0))],
        out_specs=xs,
        out_shape=jax.ShapeDtypeStruct((b, t, D_MODEL), F32),
        compiler_params=_params(("parallel", "parallel"), 40),
        name="mem_attn",
    )(x, g, wq, q_gain, k, v, wo)


def _tile_heads(g):
    return jnp.tile(g, HEADS)


def _row_tile(m):
    return min(m, 1024)


def kernel(x_prompt, x_sample, mem_prompt, state_conv, cache_k_w128, cache_v_w128, cache_k_w512, cache_v_w512,
           cache_k_w2048, cache_v_w2048, cache_mem_k, cache_mem_v, g_mix, w_in_e, conv_w, conv_b, conv_ln_g,
           conv_ln_b, q_norm_e, k_norm_e, w_out_e, w_in_o, b_in_o, v_ln_g, v_ln_b, w_s, b_s, w_out_o, g_xmem,
           g_mem, wq_mem, wk_mem, wv_mem, q_norm_mem, k_norm_mem, wo_mem, g_ffn, w_ffn1, w_ffn2):
    depth = g_mix.shape[0]
    bp, tp, _ = x_prompt.shape
    bs, ts, _ = x_sample.shape
    mp, ms = bp * tp, bs * ts
    scale = HEAD_DIM ** -0.5
    row = lambda v: v.reshape(1, -1)
    th_rows = lambda c: c.reshape(c.shape[0], c.shape[1], c.shape[2] * HEADS, HEAD_DIM)
    caches = [th_rows(c) for c in (cache_k_w128, cache_v_w128, cache_k_w512, cache_v_w512,
                                   cache_k_w2048, cache_v_w2048)]
    mem_k_s, mem_v_s = th_rows(cache_mem_k), th_rows(cache_mem_v)

    w_in_e, w_out_e, w_in_o, w_out_o = (w.astype(BF16) for w in (w_in_e, w_out_e, w_in_o, w_out_o))
    wq_mem, wo_mem = wq_mem.astype(BF16), wo_mem.astype(BF16)
    wkv_mem = jnp.concatenate([wk_mem, wv_mem], axis=2).astype(BF16)

    xp = x_prompt.reshape(mp, D_MODEL)
    xs = x_sample.reshape(ms, D_MODEL)
    mem = mem_prompt.reshape(bp * N_MEM, D_MODEL)
    tm_p, tm_s, tm_mem = _row_tile(mp), _row_tile(ms), _row_tile(bp * N_MEM)
    assert tp % tm_p == 0 or tm_p % tp == 0
    assert ms == tm_s

    conv_pl, conv_sl, kv_pl, kv_sl, memk_pl, memv_pl, chunk_sl = [], [], [], [], [], [], []
    for i in range(depth):
        j = i // 2
        kgain = jnp.stack([_tile_heads(k_norm_mem[i]), jnp.ones((D_MEMATT,), F32)])[:, None, :]
        mk, mv = headnorm_proj(mem, row(g_mem[i]), wkv_mem, i, 0, kgain, 1, tm_mem,
                               [(0, tm_mem, 1), (1, tm_mem, 1)], False)
        memk_pl.append(mk.reshape(bp, N_MEM, HEADS, HEAD_DIM))
        memv_pl.append(mv.reshape(bp, N_MEM, HEADS, HEAD_DIM))
        qgain = row(_tile_heads(q_norm_mem[i]) * scale)
        mem_p = (row(g_xmem[i]), wq_mem, qgain, mk.reshape(1, bp, N_MEM * HEADS, HEAD_DIM),
                 mv.reshape(1, bp, N_MEM * HEADS, HEAD_DIM), 0, wo_mem, i)
        mem_s = (row(g_xmem[i]), wq_mem, qgain, mem_k_s, mem_v_s, i, wo_mem, i)

        if i % 2 == 0:
            gains = jnp.concatenate([
                jnp.stack([_tile_heads(q_norm_e[j, gi]) * scale for gi in range(N_DGROUPS)]),
                jnp.stack([_tile_heads(k_norm_e[j, gi]) for gi in range(N_DGROUPS)]),
                jnp.ones((N_DGROUPS, D_GRP), F32)])[:, None, :]
            conv_args = (conv_w[j], row(conv_b[j]), row(conv_ln_g[j]), row(conv_ln_b[j]))
            qkv_col0 = 2 * D_CONV // D_GRP

            def kv_tiles(seq, tm):
                tiles = []
                for gi, (win, _) in enumerate(DIL_GROUPS):
                    keep = min(win, seq)
                    assert keep == seq or (keep <= tm and seq % tm == 0)
                    spec = (tm, 1) if keep == seq else (keep, seq // tm)
                    tiles += [((1 + which) * N_DGROUPS + gi,) + spec for which in range(2)]
                return tiles

            a = glu_proj(xp, row(g_mix[i]), w_in_e, j, tm_p)
            qkv, *new_kv = headnorm_proj(xp, row(g_mix[i]), w_in_e, j, qkv_col0, gains, 2 * N_DGROUPS, tm_p,
                                         kv_tiles(tp, tm_p), True)
            kv_pl.append([kv.reshape(bp, -1, HEADS, HEAD_DIM) for kv in new_kv])
            a3 = a.reshape(bp, tp, D_CONV)
            conv_pl.append(a3[:, tp - (CONV_W - 1):])
            c_out = conv_prompt(a3, *conv_args[:2], tt=512).reshape(mp, D_CONV)
            b_out = dilated_attn_prompt(qkv, bp, tp)
            xp = mix_out_mem(xp, c_out, *conv_args[2:], b_out, w_out_e, j, mem_p, bp, 512)

            a = glu_proj(xs, row(g_mix[i]), w_in_e, j, tm_s)
            qkv, *new_kv = headnorm_proj(xs, row(g_mix[i]), w_in_e, j, qkv_col0, gains, 2 * N_DGROUPS, tm_s,
                                         kv_tiles(ts, tm_s), True)
            kv_sl.append([kv.reshape(bs, -1, HEADS, HEAD_DIM) for kv in new_kv])
            apad = jnp.concatenate([state_conv[j], a.reshape(bs, ts, D_CONV)], axis=1)
            conv_sl.append(apad[:, apad.shape[1] - (CONV_W - 1):])
            a_out = conv_sample(apad, *conv_args).reshape(ms, D_CONV)
            b_out = dilated_attn_sample(qkv.reshape(bs, ts, 3 * D_ATT), caches, j).reshape(ms, D_GRP)
            xs = mix_out(xs, a_out, b_out, w_out_e, j, tm_s)
        else:
            sgu_vecs = (row(v_ln_g[j]), row(v_ln_b[j]))
            tril = jnp.tril(jnp.ones((CHUNK, CHUNK), F32))

            def spatial(t):
                c = min(CHUNK, t)
                assert c in (SUBLANES, CHUNK) and t % c == 0
                return (w_s[j][:, :c, :c] * tril[:c, :c]).astype(BF16), b_s[j][:, :c].T

            z = gelu_proj(xp, row(g_mix[i]), w_in_o, j, row(b_in_o[j]), tm_p)
            xp = sgu_out_mem(xp, z, *sgu_vecs, *spatial(tp), w_out_o, j, mem_p, bp, 512)

            z = gelu_proj(xs, row(g_mix[i]), w_in_o, j, row(b_in_o[j]), tm_s)
            xs, v = sgu_out(xs, z, *sgu_vecs, *spatial(ts), w_out_o, j, 256, emit_v=True)
            chunk_sl.append(v.reshape(bs, ts, D_GATE))

        xs = mem_attn(xs.reshape(bs, ts, D_MODEL), *mem_s, 8, ts).reshape(ms, D_MODEL)
        xs, w1, w2 = ffn_cast(xs, row(g_ffn[i]), w_ffn1, w_ffn2, i, 512)
        xp = ffn(xp, row(g_ffn[i]), w1, w2, 0, tm_p, 1024)

    stack = lambda items: jnp.stack(items)
    kv_p = [stack([kv[n] for kv in kv_pl]) for n in range(2 * N_DGROUPS)]
    kv_s = [stack([kv[n] for kv in kv_sl]) for n in range(2 * N_DGROUPS)]
    return (xp.reshape(bp, tp, D_MODEL), xs.reshape(bs, ts, D_MODEL), stack(conv_pl), stack(conv_sl),
            *kv_p, *kv_s, stack(memk_pl), stack(memv_pl), stack(chunk_sl))
```

```python
import functools
import math

import numpy as np
import jax
import jax.numpy as jnp
from jax import lax
from jax.experimental import pallas as pl
from jax.experimental.pallas import tpu as pltpu

D_MODEL = 2048
EPS = 1e-6
NEG = -1e30
D_CONV = D_MODEL // 2
CONV_W = 31
HEAD_DIM = 128
DIL_GROUPS = ((128, 1), (512, 4), (2048, 16))
N_DGROUPS = len(DIL_GROUPS)
HEADS = 4
D_GRP = HEADS * HEAD_DIM
D_ATT = N_DGROUPS * D_GRP
CHUNK = 128
D_GATE = D_MODEL
N_SG = 8
D_SG = D_GATE // N_SG
N_MEM = 256
D_MEMATT = HEADS * HEAD_DIM
D_FF = 4 * D_MODEL

F32 = jnp.float32
BF16 = jnp.bfloat16
MIB = 1024 * 1024
NORM_ROWS = 256
CONV_HALO = 32
SUBLANES = 8
MERGE_ROWS = 256
BAND = DIL_GROUPS[0][0] // DIL_GROUPS[0][1]
GATHER_STRIDE = 4
ATTN_UNROLL = 16
CONV_ROWS = 16
LANES = 128


def _params(sem, vmem_mib):
    return pltpu.CompilerParams(dimension_semantics=sem, vmem_limit_bytes=vmem_mib * MIB)


def _dot(a, b):
    return jnp.dot(a, b, preferred_element_type=F32)


def _dot_nt(a, b):
    return lax.dot_general(a, b, (((1,), (1,)), ((), ())), preferred_element_type=F32)


def _rms(x):
    return x * lax.rsqrt(jnp.mean(x * x, axis=-1, keepdims=True) + EPS)


def _norm_to_scratch(x_ref, g_ref, h_ref):
    rows = x_ref.shape[0]
    step = min(NORM_ROWS, rows)

    def body(c, carry):
        r = pl.multiple_of(c * step, step)
        x = x_ref[pl.ds(r, step), :]
        h_ref[pl.ds(r, step), :] = (_rms(x) * g_ref[...]).astype(BF16)
        return carry

    lax.fori_loop(0, rows // step, body, 0)


def _head_norm(acc, gain):
    parts = [_rms(acc[:, h * HEAD_DIM:(h + 1) * HEAD_DIM]) for h in range(acc.shape[1] // HEAD_DIM)]
    return jnp.concatenate(parts, axis=-1) * gain


def _head_rows(ref, h, rows):
    return ref[pl.ds(h, rows, stride=HEADS), :]


def _softmax_pv(s, v):
    mx = jnp.max(s, axis=-1, keepdims=True)
    p = jnp.exp(s - mx)
    l = jnp.sum(p, axis=-1, keepdims=True)
    return _dot(p.astype(BF16), v) / l, mx + jnp.log(l)


def _merge3(outs, lses):
    mx = jnp.maximum(jnp.maximum(lses[0], lses[1]), lses[2])
    es = [jnp.exp(l - mx) for l in lses]
    return (es[0] * outs[0] + es[1] * outs[1] + es[2] * outs[2]) / (es[0] + es[1] + es[2])


def _glu_kernel(x_ref, g_ref, wv_ref, wg_ref, o_ref, h_ref):
    @pl.when(pl.program_id(1) == 0)
    def _():
        _norm_to_scratch(x_ref, g_ref, h_ref)

    h = h_ref[...]
    val = _dot(h, wv_ref[...])
    gate = _dot(h, wg_ref[...])
    o_ref[...] = val * jax.nn.sigmoid(gate)


def glu_proj(x, g, w_in, layer, tm):
    m = x.shape[0]
    tn = D_CONV
    nj = D_CONV // tn
    return pl.pallas_call(
        _glu_kernel,
        grid=(m // tm, nj),
        in_specs=[
            pl.BlockSpec((tm, D_MODEL), lambda i, j: (i, 0)),
            pl.BlockSpec((1, D_MODEL), lambda i, j: (0, 0)),
            pl.BlockSpec((None, D_MODEL, tn), lambda i, j: (layer, 0, j)),
            pl.BlockSpec((None, D_MODEL, tn), lambda i, j: (layer, 0, j + nj)),
        ],
        out_specs=pl.BlockSpec((tm, tn), lambda i, j: (i, j)),
        out_shape=jax.ShapeDtypeStruct((m, D_CONV), F32),
        scratch_shapes=[pltpu.VMEM((tm, D_MODEL), BF16)],
        compiler_params=_params(("parallel", "arbitrary"), 48),
        name="glu_proj",
    )(x, g, w_in, w_in)


def _headnorm_proj_kernel(x_ref, g_ref, w_ref, gain_ref, *rest, n_normed, emit_main, th_tiles):
    n_th = len(th_tiles)
    if emit_main:
        o_ref, th_refs, h_ref = rest[0], rest[1:1 + n_th], rest[1 + n_th]
    else:
        th_refs, h_ref, o_ref = rest[:n_th], rest[n_th], rest[n_th + 1]
    i, j = pl.program_id(0), pl.program_id(1)
    tm = x_ref.shape[0]

    @pl.when(j == 0)
    def _():
        _norm_to_scratch(x_ref, g_ref, h_ref)

    acc = _dot(h_ref[...], w_ref[...])
    per_tile = acc.shape[1] // D_GRP
    for s in range(per_tile):
        cols = slice(s * D_GRP, (s + 1) * D_GRP)
        normed = _head_norm(acc[:, cols], gain_ref[0, :, cols])
        o_ref[:, cols] = jnp.where(j * per_tile + s < n_normed, normed, acc[:, cols])

    for th_ref, (col, period) in zip(th_refs, th_tiles):
        rows = th_ref.shape[0] // HEADS
        c0 = col % per_tile * D_GRP

        @pl.when((j == col // per_tile) & (i % period == period - 1))
        def _():
            for h in range(HEADS):
                th_ref[pl.ds(h, rows, stride=HEADS), :] = o_ref[tm - rows:, c0 + h * HEAD_DIM:c0 + (h + 1) * HEAD_DIM]


def _th_index(period, i, j):
    return (i // period, 0)


def headnorm_proj(x, g, w, layer, gains, n_normed, tm, tn, th_tiles, emit_main):
    m = x.shape[0]
    n = gains.shape[0] * D_GRP
    nj = n // tn
    assert n % tn == 0 and tn % D_GRP == 0
    out_specs, out_shape = [], []
    if emit_main:
        out_specs.append(pl.BlockSpec((tm, tn), lambda i, j: (i, j)))
        out_shape.append(jax.ShapeDtypeStruct((m, n), F32))
    for _, keep, period in th_tiles:
        assert keep <= tm and (m // tm) % period == 0
        out_specs.append(pl.BlockSpec((keep * HEADS, HEAD_DIM), functools.partial(_th_index, period),
                                      pipeline_mode=pl.Buffered(1)))
        out_shape.append(jax.ShapeDtypeStruct((m // tm // period * keep * HEADS, HEAD_DIM), F32))
    scratch = [pltpu.VMEM((tm, D_MODEL), BF16)]
    if not emit_main:
        scratch.append(pltpu.VMEM((tm, tn), F32))
    return pl.pallas_call(
        functools.partial(_headnorm_proj_kernel, n_normed=n_normed, emit_main=emit_main,
                          th_tiles=tuple((col, period) for col, _, period in th_tiles)),
        grid=(m // tm, nj),
        in_specs=[
            pl.BlockSpec((tm, D_MODEL), lambda i, j: (i, 0)),
            pl.BlockSpec((1, D_MODEL), lambda i, j: (0, 0)),
            pl.BlockSpec((None, D_MODEL, tn), lambda i, j: (layer, 0, j)),
            pl.BlockSpec((1, 1, tn), lambda i, j: (j, 0, 0)),
        ],
        out_specs=out_specs,
        out_shape=out_shape,
        scratch_shapes=scratch,
        compiler_params=_params(("arbitrary", "arbitrary"), 58),
        name="headnorm_proj",
    )(x, g, w, gains.reshape(nj, 1, tn))


def _gelu_proj_kernel(x_ref, g_ref, w_ref, b_ref, o_ref, h_ref):
    @pl.when(pl.program_id(1) == 0)
    def _():
        _norm_to_scratch(x_ref, g_ref, h_ref)

    z = _dot(h_ref[...], w_ref[...]) + b_ref[...]
    o_ref[...] = 0.5 * z * (1.0 + lax.erf(z * np.float32(math.sqrt(0.5))))


def gelu_proj(x, g, w, layer, b, tm):
    m = x.shape[0]
    n = w.shape[2]
    tn = 1024
    return pl.pallas_call(
        _gelu_proj_kernel,
        grid=(m // tm, n // tn),
        in_specs=[
            pl.BlockSpec((tm, D_MODEL), lambda i, j: (i, 0)),
            pl.BlockSpec((1, D_MODEL), lambda i, j: (0, 0)),
            pl.BlockSpec((None, D_MODEL, tn), lambda i, j: (layer, 0, j)),
            pl.BlockSpec((1, tn), lambda i, j: (0, j)),
        ],
        out_specs=pl.BlockSpec((tm, tn), lambda i, j: (i, j)),
        out_shape=jax.ShapeDtypeStruct((m, n), F32),
        scratch_shapes=[pltpu.VMEM((tm, D_MODEL), BF16)],
        compiler_params=_params(("parallel", "arbitrary"), 48),
        name="gelu_proj",
    )(x, g, w, b)


def _ffn_kernel(x_ref, g_ref, w1_ref, w2_ref, o_ref, h_ref):
    @pl.when(pl.program_id(1) == 0)
    def _():
        _norm_to_scratch(x_ref, g_ref, h_ref)
        o_ref[...] = x_ref[...]

    hid = jnp.maximum(_dot(h_ref[...], w1_ref[...]), 0.0)
    o_ref[...] += _dot((hid * hid).astype(BF16), w2_ref[...])


def ffn(x, g, w1, w2, layer, tm, tf):
    m = x.shape[0]
    return pl.pallas_call(
        _ffn_kernel,
        grid=(m // tm, D_FF // tf),
        in_specs=[
            pl.BlockSpec((tm, D_MODEL), lambda i, f: (i, 0)),
            pl.BlockSpec((1, D_MODEL), lambda i, f: (0, 0)),
            pl.BlockSpec((None, D_MODEL, tf), lambda i, f: (layer, 0, f)),
            pl.BlockSpec((None, tf, D_MODEL), lambda i, f: (layer, f, 0)),
        ],
        out_specs=pl.BlockSpec((tm, D_MODEL), lambda i, f: (i, 0)),
        out_shape=jax.ShapeDtypeStruct((m, D_MODEL), F32),
        scratch_shapes=[pltpu.VMEM((tm, D_MODEL), BF16)],
        compiler_params=_params(("parallel", "arbitrary"), 60),
        name="ffn",
    )(x, g, w1, w2)


def _ffn_cast_kernel(x_ref, g_ref, w1_ref, w2_ref, o_ref, w1b_ref, w2b_ref, h_ref):
    @pl.when(pl.program_id(0) == 0)
    def _():
        _norm_to_scratch(x_ref, g_ref, h_ref)
        o_ref[...] = x_ref[...]

    w1 = w1_ref[...].astype(BF16)
    w2 = w2_ref[...].astype(BF16)
    w1b_ref[...] = w1
    w2b_ref[...] = w2
    hid = jnp.maximum(_dot(h_ref[...], w1), 0.0)
    o_ref[...] += _dot((hid * hid).astype(BF16), w2)


def ffn_cast(x, g, w1, w2, layer, tf):
    m = x.shape[0]
    const = lambda shape: pl.BlockSpec(shape, lambda f: (0,) * len(shape))
    return pl.pallas_call(
        _ffn_cast_kernel,
        grid=(D_FF // tf,),
        in_specs=[const((m, D_MODEL)), const((1, D_MODEL)),
                  pl.BlockSpec((None, D_MODEL, tf), lambda f: (layer, 0, f)),
                  pl.BlockSpec((None, tf, D_MODEL), lambda f: (layer, f, 0))],
        out_specs=[const((m, D_MODEL)),
                   pl.BlockSpec((None, D_MODEL, tf), lambda f: (0, 0, f)),
                   pl.BlockSpec((None, tf, D_MODEL), lambda f: (0, f, 0))],
        out_shape=[jax.ShapeDtypeStruct((m, D_MODEL), F32),
                   jax.ShapeDtypeStruct((1, D_MODEL, D_FF), BF16),
                   jax.ShapeDtypeStruct((1, D_FF, D_MODEL), BF16)],
        scratch_shapes=[pltpu.VMEM((m, D_MODEL), BF16)],
        compiler_params=_params(("arbitrary",), 48),
        name="ffn_cast",
    )(x, g, w1, w2)


def _ln_silu(acc, lg_ref, lb_ref):
    mu = jnp.mean(acc, axis=-1, keepdims=True)
    xc = acc - mu
    y = xc * lax.rsqrt(jnp.mean(xc * xc, axis=-1, keepdims=True) + EPS)
    y = y * lg_ref[...] + lb_ref[...]
    return y * jax.nn.sigmoid(y)


def _conv_prompt_kernel(a_ref, halo_ref, w_ref, cb_ref, o_ref, sh_ref, wrep_ref):
    tt = a_ref.shape[0]
    first = pl.program_id(1) == 0
    for k in range(CONV_W):
        wrep_ref[k] = jnp.broadcast_to(w_ref[k:k + 1, :], (SUBLANES, D_CONV))
    sh_ref[0, 0:CONV_HALO, :] = jnp.where(first, 0.0, halo_ref[...])
    sh_ref[0, CONV_HALO:, :] = a_ref[...]
    span = tt + CONV_HALO - SUBLANES
    for s in range(1, SUBLANES):
        sh_ref[s, 0:span, :] = sh_ref[0, s:s + span, :]
    lead = CONV_HALO - (CONV_W - 1)
    groups = CONV_ROWS // SUBLANES

    def conv_rows(c, carry):
        r = pl.multiple_of(c * CONV_ROWS, CONV_ROWS)
        acc = jnp.zeros((groups, SUBLANES, D_CONV), F32) + cb_ref[...]
        for k in range(CONV_W):
            q, s = divmod(k + lead, SUBLANES)
            x = sh_ref[s, pl.ds(r + q * SUBLANES, CONV_ROWS), :]
            acc = acc + x.reshape(groups, SUBLANES, D_CONV) * wrep_ref[k]
        o_ref[pl.ds(r, CONV_ROWS), :] = acc.reshape(CONV_ROWS, D_CONV)
        return carry

    lax.fori_loop(0, tt // CONV_ROWS, conv_rows, 0)


def conv_prompt(a, w, cb, tt):
    b, t, _ = a.shape
    hb = tt // CONV_HALO
    vec = pl.BlockSpec((1, D_CONV), lambda i, j: (0, 0))
    return pl.pallas_call(
        _conv_prompt_kernel,
        grid=(b, t // tt),
        in_specs=[
            pl.BlockSpec((None, tt, D_CONV), lambda i, j: (i, j, 0)),
            pl.BlockSpec((None, CONV_HALO, D_CONV), lambda i, j: (i, jnp.maximum(j * hb - 1, 0), 0)),
            pl.BlockSpec((CONV_W, D_CONV), lambda i, j: (0, 0)),
            vec,
        ],
        out_specs=pl.BlockSpec((None, tt, D_CONV), lambda i, j: (i, j, 0)),
        out_shape=jax.ShapeDtypeStruct((b, t, D_CONV), F32),
        scratch_shapes=[pltpu.VMEM((SUBLANES, CONV_HALO + tt, D_CONV), F32),
                        pltpu.VMEM((CONV_W, SUBLANES, D_CONV), F32)],
        compiler_params=_params(("parallel", "arbitrary"), 40),
        name="conv_prompt",
    )(a, a, w, cb)


def _conv_sample_kernel(apad_ref, w_ref, cb_ref, lg_ref, lb_ref, o_ref):
    rows = o_ref.shape[0]
    acc = jnp.zeros((rows, D_CONV), F32) + cb_ref[...]
    for k in range(CONV_W):
        acc = acc + apad_ref[k:k + rows, :] * w_ref[k:k + 1, :]
    o_ref[...] = _ln_silu(acc, lg_ref, lb_ref).astype(o_ref.dtype)


def conv_sample(apad, w, cb, lg, lb):
    b, tp, _ = apad.shape
    t = tp - (CONV_W - 1)
    vec = pl.BlockSpec((1, D_CONV), lambda i: (0, 0))
    return pl.pallas_call(
        _conv_sample_kernel,
        grid=(b,),
        in_specs=[
            pl.BlockSpec((None, tp, D_CONV), lambda i: (i, 0, 0)),
            pl.BlockSpec((CONV_W, D_CONV), lambda i: (0, 0)),
            vec, vec, vec,
        ],
        out_specs=pl.BlockSpec((None, t, D_CONV), lambda i: (i, 0, 0)),
        out_shape=jax.ShapeDtypeStruct((b, t, D_CONV), BF16),
        compiler_params=_params(("parallel",), 32),
        name="conv_sample",
    )(apad, w, cb, lg, lb)


def _dil_prompt_kernel(*refs):
    qkv_refs = refs[:3 * N_DGROUPS]
    b_ref, qbuf, kbuf, vbuf, o_sc, l_sc, stage = refs[3 * N_DGROUPS:]
    t = b_ref.shape[0]
    n = BAND
    qi = lax.broadcasted_iota(jnp.int32, (n, 2 * n), 0)
    kj = lax.broadcasted_iota(jnp.int32, (n, 2 * n), 1)
    band = (kj > qi) & (kj <= qi + n)

    for gi, (win, dil) in enumerate(DIL_GROUPS):
        assert win // dil == n
        q_ref, k_ref, v_ref = qkv_refs[3 * gi:3 * gi + 3]
        s_len = t // dil
        n_blk = s_len // n
        pitch = s_len + n

        def place(r, q_rows, k_rows, v_rows, s_len=s_len, pitch=pitch):
            k0 = pl.multiple_of(r * pitch, n)
            qbuf[pl.ds(pl.multiple_of(r * s_len, n), s_len), :] = q_rows.astype(BF16)
            kbuf[pl.ds(k0, n), :] = jnp.zeros((n, HEAD_DIM), BF16)
            vbuf[pl.ds(k0, n), :] = jnp.zeros((n, HEAD_DIM), BF16)
            kbuf[pl.ds(k0 + n, s_len), :] = k_rows.astype(BF16)
            vbuf[pl.ds(k0 + n, s_len), :] = v_rows.astype(BF16)

        if dil > GATHER_STRIDE and dil % GATHER_STRIDE == 0:
            outer = dil // GATHER_STRIDE
            mid = t // GATHER_STRIDE

            def gather(r_in, carry, refs=(q_ref, k_ref, v_ref), s_len=s_len, outer=outer, mid=mid, place=place):
                for w, ref in enumerate(refs):
                    stage[w, 0:mid, :] = ref[pl.ds(r_in, mid, stride=GATHER_STRIDE), :]
                for m in range(outer):
                    place(r_in + m * GATHER_STRIDE,
                          *[stage[w, pl.ds(m, s_len, stride=outer), :] for w in range(3)])
                return carry

            lax.fori_loop(0, GATHER_STRIDE, gather, 0)
        else:
            def gather(r, carry, refs=(q_ref, k_ref, v_ref), dil=dil, s_len=s_len, place=place):
                rows = pl.ds(r, s_len, stride=dil) if dil > 1 else pl.ds(0, s_len)
                place(r, *[ref[rows, :] for ref in refs])
                return carry

            lax.fori_loop(0, dil, gather, 0)

        def units(it, carry, dil=dil, n_blk=n_blk, pitch=pitch, gi=gi):
            for j in range(ATTN_UNROLL):
                u = it * ATTN_UNROLL + j
                if n_blk == 1:
                    r, blk = u, 0
                elif dil == 1:
                    r, blk = 0, u
                else:
                    r, blk = lax.div(u, jnp.int32(n_blk)), lax.rem(u, jnp.int32(n_blk))
                q = qbuf[pl.ds(pl.multiple_of(u * n, n), n), :]
                k0 = pl.multiple_of(r * pitch + blk * n, n)
                mask = band & (kj >= jnp.where(blk == 0, n, 0))
                s = jnp.where(mask, _dot_nt(q, kbuf[pl.ds(k0, 2 * n), :]), NEG)
                o, lse = _softmax_pv(s, vbuf[pl.ds(k0, 2 * n), :])
                start = r + blk * (n * dil)
                dst = pl.ds(start, n, stride=dil) if dil > 1 else pl.ds(pl.multiple_of(start, n), n)
                o_sc[gi, dst, :] = o
                l_sc[gi, dst, :] = jnp.broadcast_to(lse, (n, HEAD_DIM))
            return carry

        assert (dil * n_blk) % ATTN_UNROLL == 0
        lax.fori_loop(0, dil * n_blk // ATTN_UNROLL, units, 0)

    def merge(c, carry):
        rows = pl.ds(pl.multiple_of(c * MERGE_ROWS, MERGE_ROWS), MERGE_ROWS)
        outs = [o_sc[gi, rows, :] for gi in range(N_DGROUPS)]
        lses = [l_sc[gi, rows, :] for gi in range(N_DGROUPS)]
        b_ref[rows, :] = _merge3(outs, lses).astype(b_ref.dtype)
        return carry

    lax.fori_loop(0, t // MERGE_ROWS, merge, 0)


def dilated_attn_prompt(qkv, batch, seq):
    assert seq % (BAND * max(d for _, d in DIL_GROUPS)) == 0 and seq % MERGE_ROWS == 0
    kv_rows = max(seq + dil * BAND for _, dil in DIL_GROUPS)

    def spec(which, gi):
        return pl.BlockSpec((seq, HEAD_DIM), lambda b, h: (b, (which * N_DGROUPS + gi) * HEADS + h))

    in_specs = [spec(which, gi) for gi in range(N_DGROUPS) for which in range(3)]
    return pl.pallas_call(
        _dil_prompt_kernel,
        grid=(batch, HEADS),
        in_specs=in_specs,
        out_specs=pl.BlockSpec((seq, HEAD_DIM), lambda b, h: (b, h)),
        out_shape=jax.ShapeDtypeStruct((batch * seq, D_GRP), BF16),
        scratch_shapes=[pltpu.VMEM((seq, HEAD_DIM), BF16), pltpu.VMEM((kv_rows, HEAD_DIM), BF16),
                        pltpu.VMEM((kv_rows, HEAD_DIM), BF16),
                        pltpu.VMEM((N_DGROUPS, seq, HEAD_DIM), F32), pltpu.VMEM((N_DGROUPS, seq, HEAD_DIM), F32),
                        pltpu.VMEM((3, seq // GATHER_STRIDE, HEAD_DIM), F32)],
        compiler_params=_params(("parallel", "parallel"), 40),
        name="dil_attn_prompt",
    )(*([qkv] * (3 * N_DGROUPS)))


def _dil_sample_kernel(q_ref, kn_ref, vn_ref, *rest):
    cache_refs, b_ref = rest[:2 * N_DGROUPS], rest[2 * N_DGROUPS]
    ds = q_ref.shape[0]
    qn = lax.broadcasted_iota(jnp.int32, (ds, ds), 0)
    pn = lax.broadcasted_iota(jnp.int32, (ds, ds), 1)
    for h in range(HEADS):
        outs, lses = [], []
        for gi, (win, dil) in enumerate(DIL_GROUPS):
            n = win // dil
            kc_ref, vc_ref = cache_refs[2 * gi], cache_refs[2 * gi + 1]
            cache_len = kc_ref.shape[0] // HEADS
            cols = slice(gi * D_GRP + h * HEAD_DIM, gi * D_GRP + (h + 1) * HEAD_DIM)
            qi = lax.broadcasted_iota(jnp.int32, (ds, cache_len), 0)
            pj = lax.broadcasted_iota(jnp.int32, (ds, cache_len), 1)
            dist = cache_len + qi - pj
            mask_c = (dist % dil == 0) & (dist <= dil * (n - 1))
            mask_n = (qn >= pn) & ((qn - pn) % dil == 0)
            q = q_ref[:, cols].astype(BF16)
            s_c = jnp.where(mask_c, _dot_nt(q, _head_rows(kc_ref, h, cache_len).astype(BF16)), NEG)
            s_n = jnp.where(mask_n, _dot_nt(q, kn_ref[:, cols].astype(BF16)), NEG)
            mx = jnp.maximum(jnp.max(s_c, axis=-1, keepdims=True), jnp.max(s_n, axis=-1, keepdims=True))
            p_c = jnp.exp(s_c - mx)
            p_n = jnp.exp(s_n - mx)
            l = jnp.sum(p_c, axis=-1, keepdims=True) + jnp.sum(p_n, axis=-1, keepdims=True)
            o = (_dot(p_c.astype(BF16), _head_rows(vc_ref, h, cache_len).astype(BF16))
                 + _dot(p_n.astype(BF16), vn_ref[:, cols].astype(BF16)))
            outs.append(o / l)
            lses.append(mx + jnp.log(l))
        b_ref[:, h * HEAD_DIM:(h + 1) * HEAD_DIM] = _merge3(outs, lses).astype(b_ref.dtype)


def dilated_attn_sample(qkv, caches, layer):
    b, ds, _ = qkv.shape
    cache_specs = []
    for gi, (win, dil) in enumerate(DIL_GROUPS):
        rows = caches[2 * gi].shape[2]
        assert rows // HEADS - dil * (win // dil - 1) >= 0
        cache_specs += [pl.BlockSpec((None, None, rows, HEAD_DIM), lambda i: (layer, i, 0, 0))] * 2

    new = lambda which: pl.BlockSpec((None, ds, D_ATT), lambda i: (i, 0, which))
    return pl.pallas_call(
        _dil_sample_kernel,
        grid=(b,),
        in_specs=[new(0), new(1), new(2)] + cache_specs,
        out_specs=pl.BlockSpec((None, ds, D_GRP), lambda i: (i, 0, 0)),
        out_shape=jax.ShapeDtypeStruct((b, ds, D_GRP), BF16),
        compiler_params=_params(("parallel",), 48),
        name="dil_attn_sample",
    )(qkv, qkv, qkv, *caches)


def _mix_out_kernel(x_ref, a_ref, b_ref, wa_ref, wb_ref, y_ref):
    y_ref[...] = x_ref[...] + _dot(a_ref[...], wa_ref[...]) + _dot(b_ref[...], wb_ref[...])


def mix_out(x, a, b, w_out, layer, tm):
    m = x.shape[0]
    row = lambda width: pl.BlockSpec((tm, width), lambda i: (i, 0))
    return pl.pallas_call(
        _mix_out_kernel,
        grid=(m // tm,),
        in_specs=[row(D_MODEL), row(D_CONV), row(D_GRP),
                  pl.BlockSpec((None, D_CONV, D_MODEL), lambda i: (layer, 0, 0)),
                  pl.BlockSpec((None, D_GRP, D_MODEL), lambda i: (layer, D_CONV // D_GRP, 0))],
        out_specs=row(D_MODEL),
        out_shape=jax.ShapeDtypeStruct((m, D_MODEL), F32),
        compiler_params=_params(("parallel",), 48),
        name="mix_out",
    )(x, a, b, w_out, w_out)


def _mix_mem_kernel(x_ref, c_ref, lg_ref, lb_ref, b_ref, wa_ref, wb_ref,
                    g_ref, wq_ref, qg_ref, k_ref, v_ref, wo_ref, y_ref):
    a = _ln_silu(c_ref[...], lg_ref, lb_ref).astype(BF16)
    x1 = x_ref[...] + _dot(a, wa_ref[...]) + _dot(b_ref[...], wb_ref[...])
    y_ref[...] = _mem_attn_tail(x1, 1, g_ref, wq_ref, qg_ref, _kv_head_of(k_ref, v_ref), wo_ref)


def mix_out_mem(x, c, lg, lb, b, w_out, layer, mem_args, batch, tm):
    m = x.shape[0]
    nt = m // batch // tm
    row = lambda width: pl.BlockSpec((tm, width), lambda i, j: (i * nt + j, 0))
    vec = _resident((1, D_CONV), (0, 0))
    mem_specs, mem_ops = _mem_operands(*mem_args)
    return pl.pallas_call(
        _mix_mem_kernel,
        grid=(batch, nt),
        in_specs=[row(D_MODEL), row(D_CONV), vec, vec, row(D_GRP),
                  _resident((None, D_CONV, D_MODEL), (layer, 0, 0)),
                  _resident((None, D_GRP, D_MODEL), (layer, D_CONV // D_GRP, 0))] + mem_specs,
        out_specs=row(D_MODEL),
        out_shape=jax.ShapeDtypeStruct((m, D_MODEL), F32),
        compiler_params=_params(("parallel", "parallel"), 56),
        name="mix_out_mem",
    )(x, c, lg, lb, b, w_out, w_out, *mem_ops)


def _sgu_gated(u_ref, gv_ref, lg_ref, lb_ref, ws_ref, bs_ref):
    gv = gv_ref[...]
    mu = jnp.mean(gv, axis=-1, keepdims=True)
    vc = gv - mu
    v = vc * lax.rsqrt(jnp.mean(vc * vc, axis=-1, keepdims=True) + EPS) * lg_ref[...] + lb_ref[...]
    vb = v.astype(BF16)
    rows, cm = gv.shape[0], ws_ref.shape[1]
    gated = []
    for g in range(N_SG):
        cols = slice(g * D_SG, (g + 1) * D_SG)
        bias = bs_ref[:, g:g + 1]
        if cm == SUBLANES:
            v3 = vb[:, cols].astype(F32).reshape(rows // cm, cm, D_SG)
            wsg = ws_ref[g].astype(F32)
            sv = bias[None] + sum(v3[:, s:s + 1, :] * wsg[:, s:s + 1][None] for s in range(cm))
            sv = sv.reshape(rows, D_SG)
        else:
            sv = jnp.concatenate([_dot(ws_ref[g], vb[c * cm:(c + 1) * cm, cols]) + bias
                                  for c in range(rows // cm)], axis=0)
        gated.append((u_ref[:, cols] * sv).astype(BF16))
    return jnp.concatenate(gated, axis=-1), v


def _sgu_kernel(x_ref, u_ref, gv_ref, lg_ref, lb_ref, ws_ref, bs_ref, w_ref, *out_refs, emit_v):
    gated, v = _sgu_gated(u_ref, gv_ref, lg_ref, lb_ref, ws_ref, bs_ref)
    if emit_v:
        out_refs[1][...] = v
    out_refs[0][...] = x_ref[...] + _dot(gated, w_ref[...])


def _sgu_mem_kernel(x_ref, u_ref, gv_ref, lg_ref, lb_ref, ws_ref, bs_ref, w_ref,
                    g_ref, wq_ref, qg_ref, k_ref, v_ref, wo_ref, y_ref):
    gated, _ = _sgu_gated(u_ref, gv_ref, lg_ref, lb_ref, ws_ref, bs_ref)
    x1 = x_ref[...] + _dot(gated, w_ref[...])
    y_ref[...] = _mem_attn_tail(x1, 1, g_ref, wq_ref, qg_ref, _kv_head_of(k_ref, v_ref), wo_ref)


def sgu_out_mem(x, z, lg, lb, ws, bs, w_out, layer, mem_args, batch, tm):
    m = x.shape[0]
    cm = ws.shape[1]
    nt = m // batch // tm
    assert tm % cm == 0
    row = lambda jblk: pl.BlockSpec((tm, D_GATE), lambda i, j: (i * nt + j, jblk))
    vec = pl.BlockSpec((1, D_GATE), lambda i, j: (0, 0))
    mem_specs, mem_ops = _mem_operands(*mem_args)
    return pl.pallas_call(
        _sgu_mem_kernel,
        grid=(batch, nt),
        in_specs=[row(0), row(0), row(1), vec, vec,
                  _resident((N_SG, cm, cm), (0, 0, 0)), _resident((cm, N_SG), (0, 0)),
                  _resident((None, D_GATE, D_MODEL), (layer, 0, 0))] + mem_specs,
        out_specs=row(0),
        out_shape=jax.ShapeDtypeStruct((m, D_MODEL), F32),
        compiler_params=_params(("parallel", "parallel"), 56),
        name="sgu_out_mem",
    )(x, z, z, lg, lb, ws, bs, w_out, *mem_ops)


def sgu_out(x, z, lg, lb, ws, bs, w_out, layer, tm, emit_v):
    m = x.shape[0]
    cm = ws.shape[1]
    assert tm % cm == 0
    row = lambda jblk: pl.BlockSpec((tm, D_GATE), lambda i: (i, jblk))
    vec = pl.BlockSpec((1, D_GATE), lambda i: (0, 0))
    out_specs = [row(0)]
    out_shape = [jax.ShapeDtypeStruct((m, D_MODEL), F32)]
    if emit_v:
        out_specs.append(row(0))
        out_shape.append(jax.ShapeDtypeStruct((m, D_GATE), F32))
    res = pl.pallas_call(
        functools.partial(_sgu_kernel, emit_v=emit_v),
        grid=(m // tm,),
        in_specs=[row(0), row(0), row(1), vec, vec,
                  pl.BlockSpec((N_SG, cm, cm), lambda i: (0, 0, 0)),
                  pl.BlockSpec((cm, N_SG), lambda i: (0, 0)),
                  pl.BlockSpec((None, D_GATE, D_MODEL), lambda i: (layer, 0, 0))],
        out_specs=out_specs,
        out_shape=out_shape,
        compiler_params=_params(("parallel",), 52),
        name="sgu_out",
    )(x, z, z, lg, lb, ws, bs, w_out)
    return res if emit_v else (res[0], None)


def _mem_attn_tail(x, n_seq, g_ref, wq_ref, qg_ref, kv_head, wo_ref):
    tm = x.shape[0] // n_seq
    h = (_rms(x) * g_ref[...]).astype(BF16)
    q = _head_norm(_dot(h, wq_ref[...]), qg_ref[...]).astype(BF16)
    per_seq = []
    for b in range(n_seq):
        outs = []
        for hd in range(HEADS):
            k, v = kv_head(b, hd)
            o, _ = _softmax_pv(_dot_nt(q[b * tm:(b + 1) * tm, hd * HEAD_DIM:(hd + 1) * HEAD_DIM], k), v)
            outs.append(o.astype(BF16))
        per_seq.append(jnp.concatenate(outs, axis=-1))
    o = jnp.concatenate(per_seq, axis=0) if n_seq > 1 else per_seq[0]
    return x + _dot(o, wo_ref[...])


def _kv_head_of(k_ref, v_ref):
    n_mem = k_ref.shape[0] // HEADS
    return lambda b, hd: (_head_rows(k_ref, hd, n_mem).astype(BF16), _head_rows(v_ref, hd, n_mem).astype(BF16))


def _resident(shape, index):
    return pl.BlockSpec(shape, lambda *_: index, pipeline_mode=pl.Buffered(1))


def _mem_operands(g, wq, q_gain, k, v, kv_layer, wo, layer):
    specs = [_resident((1, D_MODEL), (0, 0)),
             _resident((None, D_MODEL, D_MEMATT), (layer, 0, 0)),
             _resident((1, D_MEMATT), (0, 0)),
             pl.BlockSpec((None, None, k.shape[2], HEAD_DIM), lambda i, j: (kv_layer, i, 0, 0)),
             pl.BlockSpec((None, None, k.shape[2], HEAD_DIM), lambda i, j: (kv_layer, i, 0, 0)),
             _resident((None, D_MEMATT, D_MODEL), (layer, 0, 0))]
    return specs, (g, wq, q_gain, k, v, wo)


def _mem_attn_kernel(x_ref, g_ref, wq_ref, qg_ref, k_ref, v_ref, wo_ref, y_ref):
    bb, tm, _ = x_ref.shape
    n_mem = k_ref.shape[1] // HEADS

    def kv_head(b, hd):
        rows = pl.ds(hd, n_mem, stride=HEADS)
        return k_ref[b, rows, :].astype(BF16), v_ref[b, rows, :].astype(BF16)

    x = x_ref[...].reshape(bb * tm, D_MODEL)
    y_ref[...] = _mem_attn_tail(x, bb, g_ref, wq_ref, qg_ref, kv_head, wo_ref).reshape(bb, tm, D_MODEL)


def mem_attn(x, g, wq, q_gain, k, v, kv_layer, wo, layer, bb, tm):
    b, t, _ = x.shape
    full = lambda shape: pl.BlockSpec(shape, lambda i, j: (0,) * len(shape))
    kv = pl.BlockSpec((None, bb, k.shape[2], HEAD_DIM), lambda i, j: (kv_layer, i, 0, 0))
    xs = pl.BlockSpec((bb, tm, D_MODEL), lambda i, j: (i, j, 0))
    return pl.pallas_call(
        _mem_attn_kernel,
        grid=(b // bb, t // tm),
        in_specs=[xs, full((1, D_MODEL)),
                  pl.BlockSpec((None, D_MODEL, D_MEMATT), lambda i, j: (layer, 0, 0)),
                  full((1, D_MEMATT)), kv, kv,
                  pl.BlockSpec((None, D_MEMATT, D_MODEL), lambda i, j: (layer, 0, 0))],
        out_specs=xs,
        out_shape=jax.ShapeDtypeStruct((b, t, D_MODEL), F32),
        compiler_params=_params(("parallel", "parallel"), 40),
        name="mem_attn",
    )(x, g, wq, q_gain, k, v, wo)


def _tile_heads(g):
    return jnp.tile(g, HEADS)


def _row_tile(m):
    return min(m, 1024)


def kernel(x_prompt, x_sample, mem_prompt, state_conv, cache_k_w128, cache_v_w128, cache_k_w512, cache_v_w512,
           cache_k_w2048, cache_v_w2048, cache_mem_k, cache_mem_v, g_mix, w_in_e, conv_w, conv_b, conv_ln_g,
           conv_ln_b, q_norm_e, k_norm_e, w_out_e, w_in_o, b_in_o, v_ln_g, v_ln_b, w_s, b_s, w_out_o, g_xmem,
           g_mem, wq_mem, wk_mem, wv_mem, q_norm_mem, k_norm_mem, wo_mem, g_ffn, w_ffn1, w_ffn2):
    depth = g_mix.shape[0]
    bp, tp, _ = x_prompt.shape
    bs, ts, _ = x_sample.shape
    mp, ms = bp * tp, bs * ts
    scale = HEAD_DIM ** -0.5
    row = lambda v: v.reshape(1, -1)
    th_rows = lambda c: c.reshape(c.shape[0], c.shape[1], c.shape[2] * HEADS, HEAD_DIM)
    caches = [th_rows(c) for c in (cache_k_w128, cache_v_w128, cache_k_w512, cache_v_w512,
                                   cache_k_w2048, cache_v_w2048)]
    mem_k_s, mem_v_s = th_rows(cache_mem_k), th_rows(cache_mem_v)

    w_glu, w_qkv = w_in_e[:, :, :2 * D_CONV].astype(BF16), w_in_e[:, :, 2 * D_CONV:].astype(BF16)
    w_out_e, w_in_o, w_out_o = (w.astype(BF16) for w in (w_out_e, w_in_o, w_out_o))
    wq_mem, wo_mem = wq_mem.astype(BF16), wo_mem.astype(BF16)
    wkv_mem = jnp.concatenate([wk_mem, wv_mem], axis=2).astype(BF16)

    xp = x_prompt.reshape(mp, D_MODEL)
    xs = x_sample.reshape(ms, D_MODEL)
    mem = mem_prompt.reshape(bp * N_MEM, D_MODEL)
    tm_p, tm_s, tm_mem = _row_tile(mp), _row_tile(ms), _row_tile(bp * N_MEM)
    assert tp % tm_p == 0 or tm_p % tp == 0
    assert ms == tm_s

    conv_pl, conv_sl, kv_pl, kv_sl, memk_pl, memv_pl, chunk_sl = [], [], [], [], [], [], []
    for i in range(depth):
        j = i // 2
        kgain = jnp.stack([_tile_heads(k_norm_mem[i]), jnp.ones((D_MEMATT,), F32)])
        mk, mv = headnorm_proj(mem, row(g_mem[i]), wkv_mem, i, kgain, 1, tm_mem, D_MEMATT,
                               [(0, tm_mem, 1), (1, tm_mem, 1)], False)
        memk_pl.append(mk.reshape(bp, N_MEM, HEADS, HEAD_DIM))
        memv_pl.append(mv.reshape(bp, N_MEM, HEADS, HEAD_DIM))
        qgain = row(_tile_heads(q_norm_mem[i]) * scale)
        mem_p = (row(g_xmem[i]), wq_mem, qgain, mk.reshape(1, bp, N_MEM * HEADS, HEAD_DIM),
                 mv.reshape(1, bp, N_MEM * HEADS, HEAD_DIM), 0, wo_mem, i)
        mem_s = (row(g_xmem[i]), wq_mem, qgain, mem_k_s, mem_v_s, i, wo_mem, i)

        if i % 2 == 0:
            gains = jnp.concatenate([
                jnp.stack([_tile_heads(q_norm_e[j, gi]) * scale for gi in range(N_DGROUPS)]),
                jnp.stack([_tile_heads(k_norm_e[j, gi]) for gi in range(N_DGROUPS)]),
                jnp.ones((N_DGROUPS, D_GRP), F32)])
            conv_args = (conv_w[j], row(conv_b[j]), row(conv_ln_g[j]), row(conv_ln_b[j]))

            def kv_tiles(seq, tm):
                tiles = []
                for gi, (win, _) in enumerate(DIL_GROUPS):
                    keep = min(win, seq)
                    assert keep == seq or (keep <= tm and seq % tm == 0)
                    spec = (tm, 1) if keep == seq else (keep, seq // tm)
                    tiles += [((1 + which) * N_DGROUPS + gi,) + spec for which in range(2)]
                return tiles

            a = glu_proj(xp, row(g_mix[i]), w_glu, j, tm_p)
            qkv, *new_kv = headnorm_proj(xp, row(g_mix[i]), w_qkv, j, gains, 2 * N_DGROUPS, tm_p, D_ATT,
                                         kv_tiles(tp, tm_p), True)
            kv_pl.append([kv.reshape(bp, -1, HEADS, HEAD_DIM) for kv in new_kv])
            a3 = a.reshape(bp, tp, D_CONV)
            conv_pl.append(a3[:, tp - (CONV_W - 1):])
            c_out = conv_prompt(a3, *conv_args[:2], tt=512).reshape(mp, D_CONV)
            b_out = dilated_attn_prompt(qkv, bp, tp)
            xp = mix_out_mem(xp, c_out, *conv_args[2:], b_out, w_out_e, j, mem_p, bp, 512)

            a = glu_proj(xs, row(g_mix[i]), w_glu, j, tm_s)
            qkv, *new_kv = headnorm_proj(xs, row(g_mix[i]), w_qkv, j, gains, 2 * N_DGROUPS, tm_s, D_ATT,
                                         kv_tiles(ts, tm_s), True)
            kv_sl.append([kv.reshape(bs, -1, HEADS, HEAD_DIM) for kv in new_kv])
            apad = jnp.concatenate([state_conv[j], a.reshape(bs, ts, D_CONV)], axis=1)
            conv_sl.append(apad[:, apad.shape[1] - (CONV_W - 1):])
            a_out = conv_sample(apad, *conv_args).reshape(ms, D_CONV)
            b_out = dilated_attn_sample(qkv.reshape(bs, ts, 3 * D_ATT), caches, j).reshape(ms, D_GRP)
            xs = mix_out(xs, a_out, b_out, w_out_e, j, tm_s)
        else:
            sgu_vecs = (row(v_ln_g[j]), row(v_ln_b[j]))
            tril = jnp.tril(jnp.ones((CHUNK, CHUNK), F32))

            def spatial(t):
                c = min(CHUNK, t)
                assert c in (SUBLANES, CHUNK) and t % c == 0
                return (w_s[j][:, :c, :c] * tril[:c, :c]).astype(BF16), b_s[j][:, :c].T

            z = gelu_proj(xp, row(g_mix[i]), w_in_o, j, row(b_in_o[j]), tm_p)
            xp = sgu_out_mem(xp, z, *sgu_vecs, *spatial(tp), w_out_o, j, mem_p, bp, 512)

            z = gelu_proj(xs, row(g_mix[i]), w_in_o, j, row(b_in_o[j]), tm_s)
            xs, v = sgu_out(xs, z, *sgu_vecs, *spatial(ts), w_out_o, j, 256, emit_v=True)
            chunk_sl.append(v.reshape(bs, ts, D_GATE))

        xs = mem_attn(xs.reshape(bs, ts, D_MODEL), *mem_s, 8, ts).reshape(ms, D_MODEL)
        xs, w1, w2 = ffn_cast(xs, row(g_ffn[i]), w_ffn1, w_ffn2, i, 512)
        xp = ffn(xp, row(g_ffn[i]), w1, w2, 0, tm_p, 1024)

    stack = lambda items: jnp.stack(items)
    kv_p = [stack([kv[n] for kv in kv_pl]) for n in range(2 * N_DGROUPS)]
    kv_s = [stack([kv[n] for kv in kv_sl]) for n in range(2 * N_DGROUPS)]
    return (xp.reshape(bp, tp, D_MODEL), xs.reshape(bs, ts, D_MODEL), stack(conv_pl), stack(conv_sl),
            *kv_p, *kv_s, stack(memk_pl), stack(memv_pl), stack(chunk_sl))
```

```python
import functools
import math

import numpy as np
import jax
import jax.numpy as jnp
from jax import lax
from jax.experimental import pallas as pl
from jax.experimental.pallas import tpu as pltpu

D_MODEL = 2048
EPS = 1e-6
NEG = -1e30
D_CONV = D_MODEL // 2
CONV_W = 31
HEAD_DIM = 128
DIL_GROUPS = ((128, 1), (512, 4), (2048, 16))
N_DGROUPS = len(DIL_GROUPS)
HEADS = 4
D_GRP = HEADS * HEAD_DIM
D_ATT = N_DGROUPS * D_GRP
CHUNK = 128
D_GATE = D_MODEL
N_SG = 8
D_SG = D_GATE // N_SG
N_MEM = 256
D_MEMATT = HEADS * HEAD_DIM
D_FF = 4 * D_MODEL

F32 = jnp.float32
BF16 = jnp.bfloat16
MIB = 1024 * 1024
NORM_ROWS = 256
CONV_HALO = 32
SUBLANES = 8
MERGE_ROWS = 256
BAND = DIL_GROUPS[0][0] // DIL_GROUPS[0][1]
GATHER_STRIDE = 4
ATTN_UNROLL = 16
CONV_ROWS = 16
LANES = 128


def _params(sem, vmem_mib):
    return pltpu.CompilerParams(dimension_semantics=sem, vmem_limit_bytes=vmem_mib * MIB)


def _dot(a, b):
    return jnp.dot(a, b, preferred_element_type=F32)


def _dot_nt(a, b):
    return lax.dot_general(a, b, (((1,), (1,)), ((), ())), preferred_element_type=F32)


def _rms(x):
    return x * lax.rsqrt(jnp.mean(x * x, axis=-1, keepdims=True) + EPS)


def _norm_to_scratch(x_ref, g_ref, h_ref):
    rows = x_ref.shape[0]
    step = min(NORM_ROWS, rows)

    def body(c, carry):
        r = pl.multiple_of(c * step, step)
        x = x_ref[pl.ds(r, step), :]
        h_ref[pl.ds(r, step), :] = (_rms(x) * g_ref[...]).astype(BF16)
        return carry

    lax.fori_loop(0, rows // step, body, 0)


def _head_norm(acc, gain):
    parts = [_rms(acc[:, h * HEAD_DIM:(h + 1) * HEAD_DIM]) for h in range(acc.shape[1] // HEAD_DIM)]
    return jnp.concatenate(parts, axis=-1) * gain


def _head_rows(ref, h, rows):
    return ref[pl.ds(h, rows, stride=HEADS), :]


def _softmax_pv(s, v):
    mx = jnp.max(s, axis=-1, keepdims=True)
    p = jnp.exp(s - mx)
    l = jnp.sum(p, axis=-1, keepdims=True)
    return _dot(p.astype(BF16), v) / l, mx + jnp.log(l)


def _merge3(outs, lses):
    mx = jnp.maximum(jnp.maximum(lses[0], lses[1]), lses[2])
    es = [jnp.exp(l - mx) for l in lses]
    return (es[0] * outs[0] + es[1] * outs[1] + es[2] * outs[2]) / (es[0] + es[1] + es[2])


def _glu_kernel(x_ref, g_ref, wv_ref, wg_ref, o_ref, h_ref):
    @pl.when(pl.program_id(1) == 0)
    def _():
        _norm_to_scratch(x_ref, g_ref, h_ref)

    h = h_ref[...]
    val = _dot(h, wv_ref[...])
    gate = _dot(h, wg_ref[...])
    o_ref[...] = val * jax.nn.sigmoid(gate)


def glu_proj(x, g, w_in, layer, tm):
    m = x.shape[0]
    tn = D_CONV
    nj = D_CONV // tn
    return pl.pallas_call(
        _glu_kernel,
        grid=(m // tm, nj),
        in_specs=[
            pl.BlockSpec((tm, D_MODEL), lambda i, j: (i, 0)),
            pl.BlockSpec((1, D_MODEL), lambda i, j: (0, 0)),
            pl.BlockSpec((None, D_MODEL, tn), lambda i, j: (layer, 0, j)),
            pl.BlockSpec((None, D_MODEL, tn), lambda i, j: (layer, 0, j + nj)),
        ],
        out_specs=pl.BlockSpec((tm, tn), lambda i, j: (i, j)),
        out_shape=jax.ShapeDtypeStruct((m, D_CONV), F32),
        scratch_shapes=[pltpu.VMEM((tm, D_MODEL), BF16)],
        compiler_params=_params(("parallel", "arbitrary"), 48),
        name="glu_proj",
    )(x, g, w_in, w_in)


def _headnorm_proj_kernel(x_ref, g_ref, *rest, per_tile, n_normed, emit_main, th_tiles):
    w_refs, gain_ref, rest = rest[:per_tile], rest[per_tile], rest[per_tile + 1:]
    n_th = len(th_tiles)
    if emit_main:
        o_ref, th_refs, h_ref = rest[0], rest[1:1 + n_th], rest[1 + n_th]
    else:
        th_refs, h_ref, o_ref = rest[:n_th], rest[n_th], rest[n_th + 1]
    i, j = pl.program_id(0), pl.program_id(1)
    tm = x_ref.shape[0]

    @pl.when(j == 0)
    def _():
        _norm_to_scratch(x_ref, g_ref, h_ref)

    h = h_ref[...]
    for s in range(per_tile):
        cols = slice(s * D_GRP, (s + 1) * D_GRP)
        acc = _dot(h, w_refs[s][...])
        o_ref[:, cols] = jnp.where(j * per_tile + s < n_normed, _head_norm(acc, gain_ref[0, :, cols]), acc)

    for th_ref, (col, period) in zip(th_refs, th_tiles):
        rows = th_ref.shape[0] // HEADS
        c0 = col % per_tile * D_GRP

        @pl.when((j == col // per_tile) & (i % period == period - 1))
        def _():
            for h in range(HEADS):
                th_ref[pl.ds(h, rows, stride=HEADS), :] = o_ref[tm - rows:, c0 + h * HEAD_DIM:c0 + (h + 1) * HEAD_DIM]


def _th_index(period, i, j):
    return (i // period, 0)


def _w_group_index(layer, first, per_tile, s, i, j):
    return (layer, 0, first + j * per_tile + s)


def headnorm_proj(x, g, w, layer, group0, gains, n_normed, tm, tn, th_tiles, emit_main):
    m = x.shape[0]
    n = gains.shape[0] * D_GRP
    nj = n // tn
    per_tile = tn // D_GRP
    assert n % tn == 0 and tn % D_GRP == 0
    out_specs, out_shape = [], []
    if emit_main:
        out_specs.append(pl.BlockSpec((tm, tn), lambda i, j: (i, j)))
        out_shape.append(jax.ShapeDtypeStruct((m, n), F32))
    for _, keep, period in th_tiles:
        assert keep <= tm and (m // tm) % period == 0
        out_specs.append(pl.BlockSpec((keep * HEADS, HEAD_DIM), functools.partial(_th_index, period),
                                      pipeline_mode=pl.Buffered(1)))
        out_shape.append(jax.ShapeDtypeStruct((m // tm // period * keep * HEADS, HEAD_DIM), F32))
    scratch = [pltpu.VMEM((tm, D_MODEL), BF16)]
    if not emit_main:
        scratch.append(pltpu.VMEM((tm, tn), F32))
    return pl.pallas_call(
        functools.partial(_headnorm_proj_kernel, per_tile=per_tile, n_normed=n_normed, emit_main=emit_main,
                          th_tiles=tuple((col, period) for col, _, period in th_tiles)),
        grid=(m // tm, nj),
        in_specs=[pl.BlockSpec((tm, D_MODEL), lambda i, j: (i, 0)),
                  pl.BlockSpec((1, D_MODEL), lambda i, j: (0, 0))]
        + [pl.BlockSpec((None, D_MODEL, D_GRP), functools.partial(_w_group_index, layer, group0, per_tile, s))
           for s in range(per_tile)]
        + [pl.BlockSpec((1, 1, tn), lambda i, j: (j, 0, 0))],
        out_specs=out_specs,
        out_shape=out_shape,
        scratch_shapes=scratch,
        compiler_params=_params(("arbitrary", "arbitrary"), 58),
        name="headnorm_proj",
    )(x, g, *([w] * per_tile), gains.reshape(nj, 1, tn))


def _gelu_proj_kernel(x_ref, g_ref, w_ref, b_ref, o_ref, h_ref):
    @pl.when(pl.program_id(1) == 0)
    def _():
        _norm_to_scratch(x_ref, g_ref, h_ref)

    z = _dot(h_ref[...], w_ref[...]) + b_ref[...]
    o_ref[...] = 0.5 * z * (1.0 + lax.erf(z * np.float32(math.sqrt(0.5))))


def gelu_proj(x, g, w, layer, b, tm):
    m = x.shape[0]
    n = w.shape[2]
    tn = 1024
    return pl.pallas_call(
        _gelu_proj_kernel,
        grid=(m // tm, n // tn),
        in_specs=[
            pl.BlockSpec((tm, D_MODEL), lambda i, j: (i, 0)),
            pl.BlockSpec((1, D_MODEL), lambda i, j: (0, 0)),
            pl.BlockSpec((None, D_MODEL, tn), lambda i, j: (layer, 0, j)),
            pl.BlockSpec((1, tn), lambda i, j: (0, j)),
        ],
        out_specs=pl.BlockSpec((tm, tn), lambda i, j: (i, j)),
        out_shape=jax.ShapeDtypeStruct((m, n), F32),
        scratch_shapes=[pltpu.VMEM((tm, D_MODEL), BF16)],
        compiler_params=_params(("parallel", "arbitrary"), 48),
        name="gelu_proj",
    )(x, g, w, b)


def _ffn_kernel(x_ref, g_ref, w1_ref, w2_ref, o_ref, h_ref):
    @pl.when(pl.program_id(1) == 0)
    def _():
        _norm_to_scratch(x_ref, g_ref, h_ref)
        o_ref[...] = x_ref[...]

    hid = jnp.maximum(_dot(h_ref[...], w1_ref[...]), 0.0)
    o_ref[...] += _dot((hid * hid).astype(BF16), w2_ref[...])


def ffn(x, g, w1, w2, layer, tm, tf):
    m = x.shape[0]
    return pl.pallas_call(
        _ffn_kernel,
        grid=(m // tm, D_FF // tf),
        in_specs=[
            pl.BlockSpec((tm, D_MODEL), lambda i, f: (i, 0)),
            pl.BlockSpec((1, D_MODEL), lambda i, f: (0, 0)),
            pl.BlockSpec((None, D_MODEL, tf), lambda i, f: (layer, 0, f)),
            pl.BlockSpec((None, tf, D_MODEL), lambda i, f: (layer, f, 0)),
        ],
        out_specs=pl.BlockSpec((tm, D_MODEL), lambda i, f: (i, 0)),
        out_shape=jax.ShapeDtypeStruct((m, D_MODEL), F32),
        scratch_shapes=[pltpu.VMEM((tm, D_MODEL), BF16)],
        compiler_params=_params(("parallel", "arbitrary"), 60),
        name="ffn",
    )(x, g, w1, w2)


def _ffn_cast_kernel(x_ref, g_ref, w1_ref, w2_ref, o_ref, w1b_ref, w2b_ref, h_ref):
    @pl.when(pl.program_id(0) == 0)
    def _():
        _norm_to_scratch(x_ref, g_ref, h_ref)
        o_ref[...] = x_ref[...]

    w1 = w1_ref[...].astype(BF16)
    w2 = w2_ref[...].astype(BF16)
    w1b_ref[...] = w1
    w2b_ref[...] = w2
    hid = jnp.maximum(_dot(h_ref[...], w1), 0.0)
    o_ref[...] += _dot((hid * hid).astype(BF16), w2)


def ffn_cast(x, g, w1, w2, layer, tf):
    m = x.shape[0]
    const = lambda shape: pl.BlockSpec(shape, lambda f: (0,) * len(shape))
    return pl.pallas_call(
        _ffn_cast_kernel,
        grid=(D_FF // tf,),
        in_specs=[const((m, D_MODEL)), const((1, D_MODEL)),
                  pl.BlockSpec((None, D_MODEL, tf), lambda f: (layer, 0, f)),
                  pl.BlockSpec((None, tf, D_MODEL), lambda f: (layer, f, 0))],
        out_specs=[const((m, D_MODEL)),
                   pl.BlockSpec((None, D_MODEL, tf), lambda f: (0, 0, f)),
                   pl.BlockSpec((None, tf, D_MODEL), lambda f: (0, f, 0))],
        out_shape=[jax.ShapeDtypeStruct((m, D_MODEL), F32),
                   jax.ShapeDtypeStruct((1, D_MODEL, D_FF), BF16),
                   jax.ShapeDtypeStruct((1, D_FF, D_MODEL), BF16)],
        scratch_shapes=[pltpu.VMEM((m, D_MODEL), BF16)],
        compiler_params=_params(("arbitrary",), 48),
        name="ffn_cast",
    )(x, g, w1, w2)


def _ln_silu(acc, lg_ref, lb_ref):
    mu = jnp.mean(acc, axis=-1, keepdims=True)
    xc = acc - mu
    y = xc * lax.rsqrt(jnp.mean(xc * xc, axis=-1, keepdims=True) + EPS)
    y = y * lg_ref[...] + lb_ref[...]
    return y * jax.nn.sigmoid(y)


def _conv_prompt_kernel(a_ref, halo_ref, w_ref, cb_ref, o_ref, sh_ref, wrep_ref):
    tt = a_ref.shape[0]
    first = pl.program_id(1) == 0
    for k in range(CONV_W):
        wrep_ref[k] = jnp.broadcast_to(w_ref[k:k + 1, :], (SUBLANES, D_CONV))
    sh_ref[0, 0:CONV_HALO, :] = jnp.where(first, 0.0, halo_ref[...])
    sh_ref[0, CONV_HALO:, :] = a_ref[...]
    span = tt + CONV_HALO - SUBLANES
    for s in range(1, SUBLANES):
        sh_ref[s, 0:span, :] = sh_ref[0, s:s + span, :]
    lead = CONV_HALO - (CONV_W - 1)
    groups = CONV_ROWS // SUBLANES

    def conv_rows(c, carry):
        r = pl.multiple_of(c * CONV_ROWS, CONV_ROWS)
        acc = jnp.zeros((groups, SUBLANES, D_CONV), F32) + cb_ref[...]
        for k in range(CONV_W):
            q, s = divmod(k + lead, SUBLANES)
            x = sh_ref[s, pl.ds(r + q * SUBLANES, CONV_ROWS), :]
            acc = acc + x.reshape(groups, SUBLANES, D_CONV) * wrep_ref[k]
        o_ref[pl.ds(r, CONV_ROWS), :] = acc.reshape(CONV_ROWS, D_CONV)
        return carry

    lax.fori_loop(0, tt // CONV_ROWS, conv_rows, 0)


def conv_prompt(a, w, cb, tt):
    b, t, _ = a.shape
    hb = tt // CONV_HALO
    vec = pl.BlockSpec((1, D_CONV), lambda i, j: (0, 0))
    return pl.pallas_call(
        _conv_prompt_kernel,
        grid=(b, t // tt),
        in_specs=[
            pl.BlockSpec((None, tt, D_CONV), lambda i, j: (i, j, 0)),
            pl.BlockSpec((None, CONV_HALO, D_CONV), lambda i, j: (i, jnp.maximum(j * hb - 1, 0), 0)),
            pl.BlockSpec((CONV_W, D_CONV), lambda i, j: (0, 0)),
            vec,
        ],
        out_specs=pl.BlockSpec((None, tt, D_CONV), lambda i, j: (i, j, 0)),
        out_shape=jax.ShapeDtypeStruct((b, t, D_CONV), F32),
        scratch_shapes=[pltpu.VMEM((SUBLANES, CONV_HALO + tt, D_CONV), F32),
                        pltpu.VMEM((CONV_W, SUBLANES, D_CONV), F32)],
        compiler_params=_params(("parallel", "arbitrary"), 40),
        name="conv_prompt",
    )(a, a, w, cb)


def _conv_sample_kernel(apad_ref, w_ref, cb_ref, lg_ref, lb_ref, o_ref):
    rows = o_ref.shape[0]
    acc = jnp.zeros((rows, D_CONV), F32) + cb_ref[...]
    for k in range(CONV_W):
        acc = acc + apad_ref[k:k + rows, :] * w_ref[k:k + 1, :]
    o_ref[...] = _ln_silu(acc, lg_ref, lb_ref).astype(o_ref.dtype)


def conv_sample(apad, w, cb, lg, lb):
    b, tp, _ = apad.shape
    t = tp - (CONV_W - 1)
    vec = pl.BlockSpec((1, D_CONV), lambda i: (0, 0))
    return pl.pallas_call(
        _conv_sample_kernel,
        grid=(b,),
        in_specs=[
            pl.BlockSpec((None, tp, D_CONV), lambda i: (i, 0, 0)),
            pl.BlockSpec((CONV_W, D_CONV), lambda i: (0, 0)),
            vec, vec, vec,
        ],
        out_specs=pl.BlockSpec((None, t, D_CONV), lambda i: (i, 0, 0)),
        out_shape=jax.ShapeDtypeStruct((b, t, D_CONV), BF16),
        compiler_params=_params(("parallel",), 32),
        name="conv_sample",
    )(apad, w, cb, lg, lb)


def _dil_prompt_kernel(*refs):
    qkv_refs = refs[:3 * N_DGROUPS]
    b_ref, qbuf, kbuf, vbuf, o_sc, l_sc, stage = refs[3 * N_DGROUPS:]
    t = b_ref.shape[0]
    n = BAND
    qi = lax.broadcasted_iota(jnp.int32, (n, 2 * n), 0)
    kj = lax.broadcasted_iota(jnp.int32, (n, 2 * n), 1)
    band = (kj > qi) & (kj <= qi + n)

    for gi, (win, dil) in enumerate(DIL_GROUPS):
        assert win // dil == n
        q_ref, k_ref, v_ref = qkv_refs[3 * gi:3 * gi + 3]
        s_len = t // dil
        n_blk = s_len // n
        pitch = s_len + n

        def place(r, q_rows, k_rows, v_rows, s_len=s_len, pitch=pitch):
            k0 = pl.multiple_of(r * pitch, n)
            qbuf[pl.ds(pl.multiple_of(r * s_len, n), s_len), :] = q_rows.astype(BF16)
            kbuf[pl.ds(k0, n), :] = jnp.zeros((n, HEAD_DIM), BF16)
            vbuf[pl.ds(k0, n), :] = jnp.zeros((n, HEAD_DIM), BF16)
            kbuf[pl.ds(k0 + n, s_len), :] = k_rows.astype(BF16)
            vbuf[pl.ds(k0 + n, s_len), :] = v_rows.astype(BF16)

        if dil > GATHER_STRIDE and dil % GATHER_STRIDE == 0:
            outer = dil // GATHER_STRIDE
            mid = t // GATHER_STRIDE

            def gather(r_in, carry, refs=(q_ref, k_ref, v_ref), s_len=s_len, outer=outer, mid=mid, place=place):
                for w, ref in enumerate(refs):
                    stage[w, 0:mid, :] = ref[pl.ds(r_in, mid, stride=GATHER_STRIDE), :]
                for m in range(outer):
                    place(r_in + m * GATHER_STRIDE,
                          *[stage[w, pl.ds(m, s_len, stride=outer), :] for w in range(3)])
                return carry

            lax.fori_loop(0, GATHER_STRIDE, gather, 0)
        else:
            def gather(r, carry, refs=(q_ref, k_ref, v_ref), dil=dil, s_len=s_len, place=place):
                rows = pl.ds(r, s_len, stride=dil) if dil > 1 else pl.ds(0, s_len)
                place(r, *[ref[rows, :] for ref in refs])
                return carry

            lax.fori_loop(0, dil, gather, 0)

        def units(it, carry, dil=dil, n_blk=n_blk, pitch=pitch, gi=gi):
            for j in range(ATTN_UNROLL):
                u = it * ATTN_UNROLL + j
                if n_blk == 1:
                    r, blk = u, 0
                elif dil == 1:
                    r, blk = 0, u
                else:
                    r, blk = lax.div(u, jnp.int32(n_blk)), lax.rem(u, jnp.int32(n_blk))
                q = qbuf[pl.ds(pl.multiple_of(u * n, n), n), :]
                k0 = pl.multiple_of(r * pitch + blk * n, n)
                mask = band & (kj >= jnp.where(blk == 0, n, 0))
                s = jnp.where(mask, _dot_nt(q, kbuf[pl.ds(k0, 2 * n), :]), NEG)
                o, lse = _softmax_pv(s, vbuf[pl.ds(k0, 2 * n), :])
                start = r + blk * (n * dil)
                dst = pl.ds(start, n, stride=dil) if dil > 1 else pl.ds(pl.multiple_of(start, n), n)
                o_sc[gi, dst, :] = o
                l_sc[gi, dst, :] = jnp.broadcast_to(lse, (n, HEAD_DIM))
            return carry

        assert (dil * n_blk) % ATTN_UNROLL == 0
        lax.fori_loop(0, dil * n_blk // ATTN_UNROLL, units, 0)

    def merge(c, carry):
        rows = pl.ds(pl.multiple_of(c * MERGE_ROWS, MERGE_ROWS), MERGE_ROWS)
        outs = [o_sc[gi, rows, :] for gi in range(N_DGROUPS)]
        lses = [l_sc[gi, rows, :] for gi in range(N_DGROUPS)]
        b_ref[rows, :] = _merge3(outs, lses).astype(b_ref.dtype)
        return carry

    lax.fori_loop(0, t // MERGE_ROWS, merge, 0)


def dilated_attn_prompt(qkv, batch, seq):
    assert seq % (BAND * max(d for _, d in DIL_GROUPS)) == 0 and seq % MERGE_ROWS == 0
    kv_rows = max(seq + dil * BAND for _, dil in DIL_GROUPS)

    def spec(which, gi):
        return pl.BlockSpec((seq, HEAD_DIM), lambda b, h: (b, (which * N_DGROUPS + gi) * HEADS + h))

    in_specs = [spec(which, gi) for gi in range(N_DGROUPS) for which in range(3)]
    return pl.pallas_call(
        _dil_prompt_kernel,
        grid=(batch, HEADS),
        in_specs=in_specs,
        out_specs=pl.BlockSpec((seq, HEAD_DIM), lambda b, h: (b, h)),
        out_shape=jax.ShapeDtypeStruct((batch * seq, D_GRP), BF16),
        scratch_shapes=[pltpu.VMEM((seq, HEAD_DIM), BF16), pltpu.VMEM((kv_rows, HEAD_DIM), BF16),
                        pltpu.VMEM((kv_rows, HEAD_DIM), BF16),
                        pltpu.VMEM((N_DGROUPS, seq, HEAD_DIM), F32), pltpu.VMEM((N_DGROUPS, seq, HEAD_DIM), F32),
                        pltpu.VMEM((3, seq // GATHER_STRIDE, HEAD_DIM), F32)],
        compiler_params=_params(("parallel", "parallel"), 40),
        name="dil_attn_prompt",
    )(*([qkv] * (3 * N_DGROUPS)))


def _dil_sample_kernel(q_ref, kn_ref, vn_ref, *rest):
    cache_refs, b_ref = rest[:2 * N_DGROUPS], rest[2 * N_DGROUPS]
    ds = q_ref.shape[0]
    qn = lax.broadcasted_iota(jnp.int32, (ds, ds), 0)
    pn = lax.broadcasted_iota(jnp.int32, (ds, ds), 1)
    for h in range(HEADS):
        outs, lses = [], []
        for gi, (win, dil) in enumerate(DIL_GROUPS):
            n = win // dil
            kc_ref, vc_ref = cache_refs[2 * gi], cache_refs[2 * gi + 1]
            cache_len = kc_ref.shape[0] // HEADS
            cols = slice(gi * D_GRP + h * HEAD_DIM, gi * D_GRP + (h + 1) * HEAD_DIM)
            qi = lax.broadcasted_iota(jnp.int32, (ds, cache_len), 0)
            pj = lax.broadcasted_iota(jnp.int32, (ds, cache_len), 1)
            dist = cache_len + qi - pj
            mask_c = (dist % dil == 0) & (dist <= dil * (n - 1))
            mask_n = (qn >= pn) & ((qn - pn) % dil == 0)
            q = q_ref[:, cols].astype(BF16)
            s_c = jnp.where(mask_c, _dot_nt(q, _head_rows(kc_ref, h, cache_len).astype(BF16)), NEG)
            s_n = jnp.where(mask_n, _dot_nt(q, kn_ref[:, cols].astype(BF16)), NEG)
            mx = jnp.maximum(jnp.max(s_c, axis=-1, keepdims=True), jnp.max(s_n, axis=-1, keepdims=True))
            p_c = jnp.exp(s_c - mx)
            p_n = jnp.exp(s_n - mx)
            l = jnp.sum(p_c, axis=-1, keepdims=True) + jnp.sum(p_n, axis=-1, keepdims=True)
            o = (_dot(p_c.astype(BF16), _head_rows(vc_ref, h, cache_len).astype(BF16))
                 + _dot(p_n.astype(BF16), vn_ref[:, cols].astype(BF16)))
            outs.append(o / l)
            lses.append(mx + jnp.log(l))
        b_ref[:, h * HEAD_DIM:(h + 1) * HEAD_DIM] = _merge3(outs, lses).astype(b_ref.dtype)


def dilated_attn_sample(qkv, caches, layer):
    b, ds, _ = qkv.shape
    cache_specs = []
    for gi, (win, dil) in enumerate(DIL_GROUPS):
        rows = caches[2 * gi].shape[2]
        assert rows // HEADS - dil * (win // dil - 1) >= 0
        cache_specs += [pl.BlockSpec((None, None, rows, HEAD_DIM), lambda i: (layer, i, 0, 0))] * 2

    new = lambda which: pl.BlockSpec((None, ds, D_ATT), lambda i: (i, 0, which))
    return pl.pallas_call(
        _dil_sample_kernel,
        grid=(b,),
        in_specs=[new(0), new(1), new(2)] + cache_specs,
        out_specs=pl.BlockSpec((None, ds, D_GRP), lambda i: (i, 0, 0)),
        out_shape=jax.ShapeDtypeStruct((b, ds, D_GRP), BF16),
        compiler_params=_params(("parallel",), 48),
        name="dil_attn_sample",
    )(qkv, qkv, qkv, *caches)


def _mix_out_kernel(x_ref, a_ref, b_ref, wa_ref, wb_ref, y_ref):
    y_ref[...] = x_ref[...] + _dot(a_ref[...], wa_ref[...]) + _dot(b_ref[...], wb_ref[...])


def mix_out(x, a, b, w_out, layer, tm):
    m = x.shape[0]
    row = lambda width: pl.BlockSpec((tm, width), lambda i: (i, 0))
    return pl.pallas_call(
        _mix_out_kernel,
        grid=(m // tm,),
        in_specs=[row(D_MODEL), row(D_CONV), row(D_GRP),
                  pl.BlockSpec((None, D_CONV, D_MODEL), lambda i: (layer, 0, 0)),
                  pl.BlockSpec((None, D_GRP, D_MODEL), lambda i: (layer, D_CONV // D_GRP, 0))],
        out_specs=row(D_MODEL),
        out_shape=jax.ShapeDtypeStruct((m, D_MODEL), F32),
        compiler_params=_params(("parallel",), 48),
        name="mix_out",
    )(x, a, b, w_out, w_out)


def _mix_mem_kernel(x_ref, c_ref, lg_ref, lb_ref, b_ref, wa_ref, wb_ref,
                    g_ref, wq_ref, qg_ref, k_ref, v_ref, wo_ref, y_ref):
    a = _ln_silu(c_ref[...], lg_ref, lb_ref).astype(BF16)
    x1 = x_ref[...] + _dot(a, wa_ref[...]) + _dot(b_ref[...], wb_ref[...])
    y_ref[...] = _mem_attn_tail(x1, 1, g_ref, wq_ref, qg_ref, _kv_head_of(k_ref, v_ref), wo_ref)


def mix_out_mem(x, c, lg, lb, b, w_out, layer, mem_args, batch, tm):
    m = x.shape[0]
    nt = m // batch // tm
    row = lambda width: pl.BlockSpec((tm, width), lambda i, j: (i * nt + j, 0))
    vec = _resident((1, D_CONV), (0, 0))
    mem_specs, mem_ops = _mem_operands(*mem_args)
    return pl.pallas_call(
        _mix_mem_kernel,
        grid=(batch, nt),
        in_specs=[row(D_MODEL), row(D_CONV), vec, vec, row(D_GRP),
                  _resident((None, D_CONV, D_MODEL), (layer, 0, 0)),
                  _resident((None, D_GRP, D_MODEL), (layer, D_CONV // D_GRP, 0))] + mem_specs,
        out_specs=row(D_MODEL),
        out_shape=jax.ShapeDtypeStruct((m, D_MODEL), F32),
        compiler_params=_params(("parallel", "parallel"), 56),
        name="mix_out_mem",
    )(x, c, lg, lb, b, w_out, w_out, *mem_ops)


def _sgu_gated(u_ref, gv_ref, lg_ref, lb_ref, ws_ref, bs_ref):
    gv = gv_ref[...]
    mu = jnp.mean(gv, axis=-1, keepdims=True)
    vc = gv - mu
    v = vc * lax.rsqrt(jnp.mean(vc * vc, axis=-1, keepdims=True) + EPS) * lg_ref[...] + lb_ref[...]
    vb = v.astype(BF16)
    rows, cm = gv.shape[0], ws_ref.shape[1]
    gated = []
    for g in range(N_SG):
        cols = slice(g * D_SG, (g + 1) * D_SG)
        bias = bs_ref[:, g:g + 1]
        if cm == SUBLANES:
            v3 = vb[:, cols].astype(F32).reshape(rows // cm, cm, D_SG)
            wsg = ws_ref[g].astype(F32)
            sv = bias[None] + sum(v3[:, s:s + 1, :] * wsg[:, s:s + 1][None] for s in range(cm))
            sv = sv.reshape(rows, D_SG)
        else:
            sv = jnp.concatenate([_dot(ws_ref[g], vb[c * cm:(c + 1) * cm, cols]) + bias
                                  for c in range(rows // cm)], axis=0)
        gated.append((u_ref[:, cols] * sv).astype(BF16))
    return jnp.concatenate(gated, axis=-1), v


def _sgu_kernel(x_ref, u_ref, gv_ref, lg_ref, lb_ref, ws_ref, bs_ref, w_ref, *out_refs, emit_v):
    gated, v = _sgu_gated(u_ref, gv_ref, lg_ref, lb_ref, ws_ref, bs_ref)
    if emit_v:
        out_refs[1][...] = v
    out_refs[0][...] = x_ref[...] + _dot(gated, w_ref[...])


def _sgu_mem_kernel(x_ref, u_ref, gv_ref, lg_ref, lb_ref, ws_ref, bs_ref, w_ref,
                    g_ref, wq_ref, qg_ref, k_ref, v_ref, wo_ref, y_ref):
    gated, _ = _sgu_gated(u_ref, gv_ref, lg_ref, lb_ref, ws_ref, bs_ref)
    x1 = x_ref[...] + _dot(gated, w_ref[...])
    y_ref[...] = _mem_attn_tail(x1, 1, g_ref, wq_ref, qg_ref, _kv_head_of(k_ref, v_ref), wo_ref)


def sgu_out_mem(x, z, lg, lb, ws, bs, w_out, layer, mem_args, batch, tm):
    m = x.shape[0]
    cm = ws.shape[1]
    nt = m // batch // tm
    assert tm % cm == 0
    row = lambda jblk: pl.BlockSpec((tm, D_GATE), lambda i, j: (i * nt + j, jblk))
    vec = pl.BlockSpec((1, D_GATE), lambda i, j: (0, 0))
    mem_specs, mem_ops = _mem_operands(*mem_args)
    return pl.pallas_call(
        _sgu_mem_kernel,
        grid=(batch, nt),
        in_specs=[row(0), row(0), row(1), vec, vec,
                  _resident((N_SG, cm, cm), (0, 0, 0)), _resident((cm, N_SG), (0, 0)),
                  _resident((None, D_GATE, D_MODEL), (layer, 0, 0))] + mem_specs,
        out_specs=row(0),
        out_shape=jax.ShapeDtypeStruct((m, D_MODEL), F32),
        compiler_params=_params(("parallel", "parallel"), 56),
        name="sgu_out_mem",
    )(x, z, z, lg, lb, ws, bs, w_out, *mem_ops)


def sgu_out(x, z, lg, lb, ws, bs, w_out, layer, tm, emit_v):
    m = x.shape[0]
    cm = ws.shape[1]
    assert tm % cm == 0
    row = lambda jblk: pl.BlockSpec((tm, D_GATE), lambda i: (i, jblk))
    vec = pl.BlockSpec((1, D_GATE), lambda i: (0, 0))
    out_specs = [row(0)]
    out_shape = [jax.ShapeDtypeStruct((m, D_MODEL), F32)]
    if emit_v:
        out_specs.append(row(0))
        out_shape.append(jax.ShapeDtypeStruct((m, D_GATE), F32))
    res = pl.pallas_call(
        functools.partial(_sgu_kernel, emit_v=emit_v),
        grid=(m // tm,),
        in_specs=[row(0), row(0), row(1), vec, vec,
                  pl.BlockSpec((N_SG, cm, cm), lambda i: (0, 0, 0)),
                  pl.BlockSpec((cm, N_SG), lambda i: (0, 0)),
                  pl.BlockSpec((None, D_GATE, D_MODEL), lambda i: (layer, 0, 0))],
        out_specs=out_specs,
        out_shape=out_shape,
        compiler_params=_params(("parallel",), 52),
        name="sgu_out",
    )(x, z, z, lg, lb, ws, bs, w_out)
    return res if emit_v else (res[0], None)


def _mem_attn_tail(x, n_seq, g_ref, wq_ref, qg_ref, kv_head, wo_ref):
    tm = x.shape[0] // n_seq
    h = (_rms(x) * g_ref[...]).astype(BF16)
    q = _head_norm(_dot(h, wq_ref[...]), qg_ref[...]).astype(BF16)
    per_seq = []
    for b in range(n_seq):
        outs = []
        for hd in range(HEADS):
            k, v = kv_head(b, hd)
            o, _ = _softmax_pv(_dot_nt(q[b * tm:(b + 1) * tm, hd * HEAD_DIM:(hd + 1) * HEAD_DIM], k), v)
            outs.append(o.astype(BF16))
        per_seq.append(jnp.concatenate(outs, axis=-1))
    o = jnp.concatenate(per_seq, axis=0) if n_seq > 1 else per_seq[0]
    return x + _dot(o, wo_ref[...])


def _kv_head_of(k_ref, v_ref):
    n_mem = k_ref.shape[0] // HEADS
    return lambda b, hd: (_head_rows(k_ref, hd, n_mem).astype(BF16), _head_rows(v_ref, hd, n_mem).astype(BF16))


def _resident(shape, index):
    return pl.BlockSpec(shape, lambda *_: index, pipeline_mode=pl.Buffered(1))


def _mem_operands(g, wq, q_gain, k, v, kv_layer, wo, layer):
    specs = [_resident((1, D_MODEL), (0, 0)),
             _resident((None, D_MODEL, D_MEMATT), (layer, 0, 0)),
             _resident((1, D_MEMATT), (0, 0)),
             pl.BlockSpec((None, None, k.shape[2], HEAD_DIM), lambda i, j: (kv_layer, i, 0, 0)),
             pl.BlockSpec((None, None, k.shape[2], HEAD_DIM), lambda i, j: (kv_layer, i, 0, 0)),
             _resident((None, D_MEMATT, D_MODEL), (layer, 0, 0))]
    return specs, (g, wq, q_gain, k, v, wo)


def _mem_attn_kernel(x_ref, g_ref, wq_ref, qg_ref, k_ref, v_ref, wo_ref, y_ref):
    bb, tm, _ = x_ref.shape
    n_mem = k_ref.shape[1] // HEADS

    def kv_head(b, hd):
        rows = pl.ds(hd, n_mem, stride=HEADS)
        return k_ref[b, rows, :].astype(BF16), v_ref[b, rows, :].astype(BF16)

    x = x_ref[...].reshape(bb * tm, D_MODEL)
    y_ref[...] = _mem_attn_tail(x, bb, g_ref, wq_ref, qg_ref, kv_head, wo_ref).reshape(bb, tm, D_MODEL)


def mem_attn(x, g, wq, q_gain, k, v, kv_layer, wo, layer, bb, tm):
    b, t, _ = x.shape
    full = lambda shape: pl.BlockSpec(shape, lambda i, j: (0,) * len(shape))
    kv = pl.BlockSpec((None, bb, k.shape[2], HEAD_DIM), lambda i, j: (kv_layer, i, 0, 0))
    xs = pl.BlockSpec((bb, tm, D_MODEL), lambda i, j: (i, j, 0))
    return pl.pallas_call(
        _mem_attn_kernel,
        grid=(b // bb, t // tm),
        in_specs=[xs, full((1, D_MODEL)),
                  pl.BlockSpec((None, D_MODEL, D_MEMATT), lambda i, j: (layer, 0, 0)),
                  full((1, D_MEMATT)), kv, kv,
                  pl.BlockSpec((None, D_MEMATT, D_MODEL), lambda i, j: (layer, 0, 0))],
        out_specs=xs,
        out_shape=jax.ShapeDtypeStruct((b, t, D_MODEL), F32),
        compiler_params=_params(("parallel", "parallel"), 40),
        name="mem_attn",
    )(x, g, wq, q_gain, k, v, wo)


def _tile_heads(g):
    return jnp.tile(g, HEADS)


def _row_tile(m):
    return min(m, 1024)


def kernel(x_prompt, x_sample, mem_prompt, state_conv, cache_k_w128, cache_v_w128, cache_k_w512, cache_v_w512,
           cache_k_w2048, cache_v_w2048, cache_mem_k, cache_mem_v, g_mix, w_in_e, conv_w, conv_b, conv_ln_g,
           conv_ln_b, q_norm_e, k_norm_e, w_out_e, w_in_o, b_in_o, v_ln_g, v_ln_b, w_s, b_s, w_out_o, g_xmem,
           g_mem, wq_mem, wk_mem, wv_mem, q_norm_mem, k_norm_mem, wo_mem, g_ffn, w_ffn1, w_ffn2):
    depth = g_mix.shape[0]
    bp, tp, _ = x_prompt.shape
    bs, ts, _ = x_sample.shape
    mp, ms = bp * tp, bs * ts
    scale = HEAD_DIM ** -0.5
    row = lambda v: v.reshape(1, -1)
    th_rows = lambda c: c.reshape(c.shape[0], c.shape[1], c.shape[2] * HEADS, HEAD_DIM)
    caches = [th_rows(c) for c in (cache_k_w128, cache_v_w128, cache_k_w512, cache_v_w512,
                                   cache_k_w2048, cache_v_w2048)]
    mem_k_s, mem_v_s = th_rows(cache_mem_k), th_rows(cache_mem_v)

    w_in_e, w_out_e, w_in_o, w_out_o = (w.astype(BF16) for w in (w_in_e, w_out_e, w_in_o, w_out_o))
    qkv_group0 = 2 * D_CONV // D_GRP
    wq_mem, wo_mem = wq_mem.astype(BF16), wo_mem.astype(BF16)
    wkv_mem = jnp.concatenate([wk_mem, wv_mem], axis=2).astype(BF16)

    xp = x_prompt.reshape(mp, D_MODEL)
    xs = x_sample.reshape(ms, D_MODEL)
    mem = mem_prompt.reshape(bp * N_MEM, D_MODEL)
    tm_p, tm_s, tm_mem = _row_tile(mp), _row_tile(ms), _row_tile(bp * N_MEM)
    assert tp % tm_p == 0 or tm_p % tp == 0
    assert ms == tm_s

    conv_pl, conv_sl, kv_pl, kv_sl, memk_pl, memv_pl, chunk_sl = [], [], [], [], [], [], []
    for i in range(depth):
        j = i // 2
        kgain = jnp.stack([_tile_heads(k_norm_mem[i]), jnp.ones((D_MEMATT,), F32)])
        mk, mv = headnorm_proj(mem, row(g_mem[i]), wkv_mem, i, 0, kgain, 1, tm_mem, D_MEMATT,
                               [(0, tm_mem, 1), (1, tm_mem, 1)], False)
        memk_pl.append(mk.reshape(bp, N_MEM, HEADS, HEAD_DIM))
        memv_pl.append(mv.reshape(bp, N_MEM, HEADS, HEAD_DIM))
        qgain = row(_tile_heads(q_norm_mem[i]) * scale)
        mem_p = (row(g_xmem[i]), wq_mem, qgain, mk.reshape(1, bp, N_MEM * HEADS, HEAD_DIM),
                 mv.reshape(1, bp, N_MEM * HEADS, HEAD_DIM), 0, wo_mem, i)
        mem_s = (row(g_xmem[i]), wq_mem, qgain, mem_k_s, mem_v_s, i, wo_mem, i)

        if i % 2 == 0:
            gains = jnp.concatenate([
                jnp.stack([_tile_heads(q_norm_e[j, gi]) * scale for gi in range(N_DGROUPS)]),
                jnp.stack([_tile_heads(k_norm_e[j, gi]) for gi in range(N_DGROUPS)]),
                jnp.ones((N_DGROUPS, D_GRP), F32)])
            conv_args = (conv_w[j], row(conv_b[j]), row(conv_ln_g[j]), row(conv_ln_b[j]))

            def kv_tiles(seq, tm):
                tiles = []
                for gi, (win, _) in enumerate(DIL_GROUPS):
                    keep = min(win, seq)
                    assert keep == seq or (keep <= tm and seq % tm == 0)
                    spec = (tm, 1) if keep == seq else (keep, seq // tm)
                    tiles += [((1 + which) * N_DGROUPS + gi,) + spec for which in range(2)]
                return tiles

            a = glu_proj(xp, row(g_mix[i]), w_in_e, j, tm_p)
            qkv, *new_kv = headnorm_proj(xp, row(g_mix[i]), w_in_e, j, qkv_group0, gains, 2 * N_DGROUPS, tm_p, D_ATT,
                                         kv_tiles(tp, tm_p), True)
            kv_pl.append([kv.reshape(bp, -1, HEADS, HEAD_DIM) for kv in new_kv])
            a3 = a.reshape(bp, tp, D_CONV)
            conv_pl.append(a3[:, tp - (CONV_W - 1):])
            c_out = conv_prompt(a3, *conv_args[:2], tt=512).reshape(mp, D_CONV)
            b_out = dilated_attn_prompt(qkv, bp, tp)
            xp = mix_out_mem(xp, c_out, *conv_args[2:], b_out, w_out_e, j, mem_p, bp, 512)

            a = glu_proj(xs, row(g_mix[i]), w_in_e, j, tm_s)
            qkv, *new_kv = headnorm_proj(xs, row(g_mix[i]), w_in_e, j, qkv_group0, gains, 2 * N_DGROUPS, tm_s, D_ATT,
                                         kv_tiles(ts, tm_s), True)
            kv_sl.append([kv.reshape(bs, -1, HEADS, HEAD_DIM) for kv in new_kv])
            apad = jnp.concatenate([state_conv[j], a.reshape(bs, ts, D_CONV)], axis=1)
            conv_sl.append(apad[:, apad.shape[1] - (CONV_W - 1):])
            a_out = conv_sample(apad, *conv_args).reshape(ms, D_CONV)
            b_out = dilated_attn_sample(qkv.reshape(bs, ts, 3 * D_ATT), caches, j).reshape(ms, D_GRP)
            xs = mix_out(xs, a_out, b_out, w_out_e, j, tm_s)
        else:
            sgu_vecs = (row(v_ln_g[j]), row(v_ln_b[j]))
            tril = jnp.tril(jnp.ones((CHUNK, CHUNK), F32))

            def spatial(t):
                c = min(CHUNK, t)
                assert c in (SUBLANES, CHUNK) and t % c == 0
                return (w_s[j][:, :c, :c] * tril[:c, :c]).astype(BF16), b_s[j][:, :c].T

            z = gelu_proj(xp, row(g_mix[i]), w_in_o, j, row(b_in_o[j]), tm_p)
            xp = sgu_out_mem(xp, z, *sgu_vecs, *spatial(tp), w_out_o, j, mem_p, bp, 512)

            z = gelu_proj(xs, row(g_mix[i]), w_in_o, j, row(b_in_o[j]), tm_s)
            xs, v = sgu_out(xs, z, *sgu_vecs, *spatial(ts), w_out_o, j, 256, emit_v=True)
            chunk_sl.append(v.reshape(bs, ts, D_GATE))

        xs = mem_attn(xs.reshape(bs, ts, D_MODEL), *mem_s, 8, ts).reshape(ms, D_MODEL)
        xs, w1, w2 = ffn_cast(xs, row(g_ffn[i]), w_ffn1, w_ffn2, i, 512)
        xp = ffn(xp, row(g_ffn[i]), w1, w2, 0, tm_p, 1024)

    stack = lambda items: jnp.stack(items)
    kv_p = [stack([kv[n] for kv in kv_pl]) for n in range(2 * N_DGROUPS)]
    kv_s = [stack([kv[n] for kv in kv_sl]) for n in range(2 * N_DGROUPS)]
    return (xp.reshape(bp, tp, D_MODEL), xs.reshape(bs, ts, D_MODEL), stack(conv_pl), stack(conv_sl),
            *kv_p, *kv_s, stack(memk_pl), stack(memv_pl), stack(chunk_sl))
```

```python
import functools
import math

import numpy as np
import jax
import jax.numpy as jnp
from jax import lax
from jax.experimental import pallas as pl
from jax.experimental.pallas import tpu as pltpu

D_MODEL = 2048
EPS = 1e-6
NEG = -1e30
D_CONV = D_MODEL // 2
CONV_W = 31
HEAD_DIM = 128
DIL_GROUPS = ((128, 1), (512, 4), (2048, 16))
N_DGROUPS = len(DIL_GROUPS)
HEADS = 4
D_GRP = HEADS * HEAD_DIM
D_ATT = N_DGROUPS * D_GRP
CHUNK = 128
D_GATE = D_MODEL
N_SG = 8
D_SG = D_GATE // N_SG
N_MEM = 256
D_MEMATT = HEADS * HEAD_DIM
D_FF = 4 * D_MODEL

F32 = jnp.float32
BF16 = jnp.bfloat16
MIB = 1024 * 1024
NORM_ROWS = 256
CONV_HALO = 32
SUBLANES = 8
MERGE_ROWS = 256
BAND = DIL_GROUPS[0][0] // DIL_GROUPS[0][1]
GATHER_STRIDE = 4
ATTN_UNROLL = 16
CONV_ROWS = 16
LANES = 128


def _params(sem, vmem_mib):
    return pltpu.CompilerParams(dimension_semantics=sem, vmem_limit_bytes=vmem_mib * MIB)


def _dot(a, b):
    return jnp.dot(a, b, preferred_element_type=F32)


def _dot_nt(a, b):
    return lax.dot_general(a, b, (((1,), (1,)), ((), ())), preferred_element_type=F32)


def _rms(x):
    return x * lax.rsqrt(jnp.mean(x * x, axis=-1, keepdims=True) + EPS)


def _norm_to_scratch(x_ref, g_ref, h_ref):
    rows = x_ref.shape[0]
    step = min(NORM_ROWS, rows)

    def body(c, carry):
        r = pl.multiple_of(c * step, step)
        x = x_ref[pl.ds(r, step), :]
        h_ref[pl.ds(r, step), :] = (_rms(x) * g_ref[...]).astype(BF16)
        return carry

    lax.fori_loop(0, rows // step, body, 0)


def _head_norm(acc, gain):
    parts = [_rms(acc[:, h * HEAD_DIM:(h + 1) * HEAD_DIM]) for h in range(acc.shape[1] // HEAD_DIM)]
    return jnp.concatenate(parts, axis=-1) * gain


def _head_rows(ref, h, rows):
    return ref[pl.ds(h, rows, stride=HEADS), :]


def _softmax_pv(s, v):
    mx = jnp.max(s, axis=-1, keepdims=True)
    p = jnp.exp(s - mx)
    l = jnp.sum(p, axis=-1, keepdims=True)
    return _dot(p.astype(BF16), v) / l, mx + jnp.log(l)


def _merge3(outs, lses):
    mx = jnp.maximum(jnp.maximum(lses[0], lses[1]), lses[2])
    es = [jnp.exp(l - mx) for l in lses]
    return (es[0] * outs[0] + es[1] * outs[1] + es[2] * outs[2]) / (es[0] + es[1] + es[2])


def _glu_kernel(x_ref, g_ref, wv_ref, wg_ref, o_ref, h_ref):
    @pl.when(pl.program_id(1) == 0)
    def _():
        _norm_to_scratch(x_ref, g_ref, h_ref)

    h = h_ref[...]
    val = _dot(h, wv_ref[...])
    gate = _dot(h, wg_ref[...])
    o_ref[...] = val * jax.nn.sigmoid(gate)


def glu_proj(x, g, w_in, layer, tm):
    m = x.shape[0]
    tn = D_CONV
    nj = D_CONV // tn
    return pl.pallas_call(
        _glu_kernel,
        grid=(m // tm, nj),
        in_specs=[
            pl.BlockSpec((tm, D_MODEL), lambda i, j: (i, 0)),
            pl.BlockSpec((1, D_MODEL), lambda i, j: (0, 0)),
            pl.BlockSpec((None, D_MODEL, tn), lambda i, j: (layer, 0, j)),
            pl.BlockSpec((None, D_MODEL, tn), lambda i, j: (layer, 0, j + nj)),
        ],
        out_specs=pl.BlockSpec((tm, tn), lambda i, j: (i, j)),
        out_shape=jax.ShapeDtypeStruct((m, D_CONV), F32),
        scratch_shapes=[pltpu.VMEM((tm, D_MODEL), BF16)],
        compiler_params=_params(("parallel", "arbitrary"), 48),
        name="glu_proj",
    )(x, g, w_in, w_in)


def _headnorm_proj_kernel(x_ref, g_ref, *rest, per_tile, n_normed, emit_main, th_tiles):
    w_refs, gain_ref, rest = rest[:per_tile], rest[per_tile], rest[per_tile + 1:]
    n_th = len(th_tiles)
    if emit_main:
        o_ref, th_refs, h_ref = rest[0], rest[1:1 + n_th], rest[1 + n_th]
    else:
        th_refs, h_ref, o_ref = rest[:n_th], rest[n_th], rest[n_th + 1]
    i, j = pl.program_id(0), pl.program_id(1)
    tm = x_ref.shape[0]

    @pl.when(j == 0)
    def _():
        _norm_to_scratch(x_ref, g_ref, h_ref)

    h = h_ref[...]
    for s in range(per_tile):
        cols = slice(s * D_GRP, (s + 1) * D_GRP)
        acc = _dot(h, w_refs[s][...])
        o_ref[:, cols] = jnp.where(j * per_tile + s < n_normed, _head_norm(acc, gain_ref[0, :, cols]), acc)

    for th_ref, (col, period) in zip(th_refs, th_tiles):
        rows = th_ref.shape[0] // HEADS
        c0 = col % per_tile * D_GRP

        @pl.when((j == col // per_tile) & (i % period == period - 1))
        def _():
            for h in range(HEADS):
                th_ref[pl.ds(h, rows, stride=HEADS), :] = o_ref[tm - rows:, c0 + h * HEAD_DIM:c0 + (h + 1) * HEAD_DIM]


def _th_index(period, i, j):
    return (i // period, 0)


def _w_group_index(layer, first, per_tile, s, i, j):
    return (layer, 0, first + j * per_tile + s)


def headnorm_proj(x, g, w, layer, group0, gains, n_normed, tm, tn, th_tiles, emit_main):
    m = x.shape[0]
    n = gains.shape[0] * D_GRP
    nj = n // tn
    per_tile = tn // D_GRP
    assert n % tn == 0 and tn % D_GRP == 0
    out_specs, out_shape = [], []
    if emit_main:
        out_specs.append(pl.BlockSpec((tm, tn), lambda i, j: (i, j)))
        out_shape.append(jax.ShapeDtypeStruct((m, n), F32))
    for _, keep, period in th_tiles:
        assert keep <= tm and (m // tm) % period == 0
        mode = {} if period == 1 else {"pipeline_mode": pl.Buffered(1)}
        out_specs.append(pl.BlockSpec((keep * HEADS, HEAD_DIM), functools.partial(_th_index, period), **mode))
        out_shape.append(jax.ShapeDtypeStruct((m // tm // period * keep * HEADS, HEAD_DIM), F32))
    scratch = [pltpu.VMEM((tm, D_MODEL), BF16)]
    if not emit_main:
        scratch.append(pltpu.VMEM((tm, tn), F32))
    return pl.pallas_call(
        functools.partial(_headnorm_proj_kernel, per_tile=per_tile, n_normed=n_normed, emit_main=emit_main,
                          th_tiles=tuple((col, period) for col, _, period in th_tiles)),
        grid=(m // tm, nj),
        in_specs=[pl.BlockSpec((tm, D_MODEL), lambda i, j: (i, 0)),
                  pl.BlockSpec((1, D_MODEL), lambda i, j: (0, 0))]
        + [pl.BlockSpec((None, D_MODEL, D_GRP), functools.partial(_w_group_index, layer, group0, per_tile, s))
           for s in range(per_tile)]
        + [pl.BlockSpec((1, 1, tn), lambda i, j: (j, 0, 0))],
        out_specs=out_specs,
        out_shape=out_shape,
        scratch_shapes=scratch,
        compiler_params=_params(("arbitrary", "arbitrary"), 61),
        name="headnorm_proj",
    )(x, g, *([w] * per_tile), gains.reshape(nj, 1, tn))


def _gelu_proj_kernel(x_ref, g_ref, w_ref, b_ref, o_ref, h_ref):
    @pl.when(pl.program_id(1) == 0)
    def _():
        _norm_to_scratch(x_ref, g_ref, h_ref)

    z = _dot(h_ref[...], w_ref[...]) + b_ref[...]
    o_ref[...] = 0.5 * z * (1.0 + lax.erf(z * np.float32(math.sqrt(0.5))))


def gelu_proj(x, g, w, layer, b, tm):
    m = x.shape[0]
    n = w.shape[2]
    tn = 1024
    return pl.pallas_call(
        _gelu_proj_kernel,
        grid=(m // tm, n // tn),
        in_specs=[
            pl.BlockSpec((tm, D_MODEL), lambda i, j: (i, 0)),
            pl.BlockSpec((1, D_MODEL), lambda i, j: (0, 0)),
            pl.BlockSpec((None, D_MODEL, tn), lambda i, j: (layer, 0, j)),
            pl.BlockSpec((1, tn), lambda i, j: (0, j)),
        ],
        out_specs=pl.BlockSpec((tm, tn), lambda i, j: (i, j)),
        out_shape=jax.ShapeDtypeStruct((m, n), F32),
        scratch_shapes=[pltpu.VMEM((tm, D_MODEL), BF16)],
        compiler_params=_params(("parallel", "arbitrary"), 48),
        name="gelu_proj",
    )(x, g, w, b)


def _ffn_kernel(x_ref, g_ref, w1_ref, w2_ref, o_ref, h_ref):
    @pl.when(pl.program_id(1) == 0)
    def _():
        _norm_to_scratch(x_ref, g_ref, h_ref)
        o_ref[...] = x_ref[...]

    hid = jnp.maximum(_dot(h_ref[...], w1_ref[...]), 0.0)
    o_ref[...] += _dot((hid * hid).astype(BF16), w2_ref[...])


def ffn(x, g, w1, w2, layer, tm, tf):
    m = x.shape[0]
    return pl.pallas_call(
        _ffn_kernel,
        grid=(m // tm, D_FF // tf),
        in_specs=[
            pl.BlockSpec((tm, D_MODEL), lambda i, f: (i, 0)),
            pl.BlockSpec((1, D_MODEL), lambda i, f: (0, 0)),
            pl.BlockSpec((None, D_MODEL, tf), lambda i, f: (layer, 0, f)),
            pl.BlockSpec((None, tf, D_MODEL), lambda i, f: (layer, f, 0)),
        ],
        out_specs=pl.BlockSpec((tm, D_MODEL), lambda i, f: (i, 0)),
        out_shape=jax.ShapeDtypeStruct((m, D_MODEL), F32),
        scratch_shapes=[pltpu.VMEM((tm, D_MODEL), BF16)],
        compiler_params=_params(("parallel", "arbitrary"), 60),
        name="ffn",
    )(x, g, w1, w2)


def _ffn_cast_kernel(x_ref, g_ref, w1_ref, w2_ref, o_ref, w1b_ref, w2b_ref, h_ref):
    @pl.when(pl.program_id(0) == 0)
    def _():
        _norm_to_scratch(x_ref, g_ref, h_ref)
        o_ref[...] = x_ref[...]

    w1 = w1_ref[...].astype(BF16)
    w2 = w2_ref[...].astype(BF16)
    w1b_ref[...] = w1
    w2b_ref[...] = w2
    hid = jnp.maximum(_dot(h_ref[...], w1), 0.0)
    o_ref[...] += _dot((hid * hid).astype(BF16), w2)


def ffn_cast(x, g, w1, w2, layer, tf):
    m = x.shape[0]
    const = lambda shape: pl.BlockSpec(shape, lambda f: (0,) * len(shape))
    return pl.pallas_call(
        _ffn_cast_kernel,
        grid=(D_FF // tf,),
        in_specs=[const((m, D_MODEL)), const((1, D_MODEL)),
                  pl.BlockSpec((None, D_MODEL, tf), lambda f: (layer, 0, f)),
                  pl.BlockSpec((None, tf, D_MODEL), lambda f: (layer, f, 0))],
        out_specs=[const((m, D_MODEL)),
                   pl.BlockSpec((None, D_MODEL, tf), lambda f: (0, 0, f)),
                   pl.BlockSpec((None, tf, D_MODEL), lambda f: (0, f, 0))],
        out_shape=[jax.ShapeDtypeStruct((m, D_MODEL), F32),
                   jax.ShapeDtypeStruct((1, D_MODEL, D_FF), BF16),
                   jax.ShapeDtypeStruct((1, D_FF, D_MODEL), BF16)],
        scratch_shapes=[pltpu.VMEM((m, D_MODEL), BF16)],
        compiler_params=_params(("arbitrary",), 48),
        name="ffn_cast",
    )(x, g, w1, w2)


def _ln_silu(acc, lg_ref, lb_ref):
    mu = jnp.mean(acc, axis=-1, keepdims=True)
    xc = acc - mu
    y = xc * lax.rsqrt(jnp.mean(xc * xc, axis=-1, keepdims=True) + EPS)
    y = y * lg_ref[...] + lb_ref[...]
    return y * jax.nn.sigmoid(y)


def _conv_prompt_kernel(a_ref, halo_ref, w_ref, cb_ref, o_ref, sh_ref, wrep_ref):
    tt = a_ref.shape[0]
    first = pl.program_id(1) == 0
    for k in range(CONV_W):
        wrep_ref[k] = jnp.broadcast_to(w_ref[k:k + 1, :], (SUBLANES, D_CONV))
    sh_ref[0, 0:CONV_HALO, :] = jnp.where(first, 0.0, halo_ref[...])
    sh_ref[0, CONV_HALO:, :] = a_ref[...]
    span = tt + CONV_HALO - SUBLANES
    for s in range(1, SUBLANES):
        sh_ref[s, 0:span, :] = sh_ref[0, s:s + span, :]
    lead = CONV_HALO - (CONV_W - 1)
    groups = CONV_ROWS // SUBLANES

    def conv_rows(c, carry):
        r = pl.multiple_of(c * CONV_ROWS, CONV_ROWS)
        acc = jnp.zeros((groups, SUBLANES, D_CONV), F32) + cb_ref[...]
        for k in range(CONV_W):
            q, s = divmod(k + lead, SUBLANES)
            x = sh_ref[s, pl.ds(r + q * SUBLANES, CONV_ROWS), :]
            acc = acc + x.reshape(groups, SUBLANES, D_CONV) * wrep_ref[k]
        o_ref[pl.ds(r, CONV_ROWS), :] = acc.reshape(CONV_ROWS, D_CONV)
        return carry

    lax.fori_loop(0, tt // CONV_ROWS, conv_rows, 0)


def conv_prompt(a, w, cb, tt):
    b, t, _ = a.shape
    hb = tt // CONV_HALO
    vec = pl.BlockSpec((1, D_CONV), lambda i, j: (0, 0))
    return pl.pallas_call(
        _conv_prompt_kernel,
        grid=(b, t // tt),
        in_specs=[
            pl.BlockSpec((None, tt, D_CONV), lambda i, j: (i, j, 0)),
            pl.BlockSpec((None, CONV_HALO, D_CONV), lambda i, j: (i, jnp.maximum(j * hb - 1, 0), 0)),
            pl.BlockSpec((CONV_W, D_CONV), lambda i, j: (0, 0)),
            vec,
        ],
        out_specs=pl.BlockSpec((None, tt, D_CONV), lambda i, j: (i, j, 0)),
        out_shape=jax.ShapeDtypeStruct((b, t, D_CONV), F32),
        scratch_shapes=[pltpu.VMEM((SUBLANES, CONV_HALO + tt, D_CONV), F32),
                        pltpu.VMEM((CONV_W, SUBLANES, D_CONV), F32)],
        compiler_params=_params(("parallel", "arbitrary"), 40),
        name="conv_prompt",
    )(a, a, w, cb)


def _conv_sample_kernel(apad_ref, w_ref, cb_ref, lg_ref, lb_ref, o_ref):
    rows = o_ref.shape[0]
    acc = jnp.zeros((rows, D_CONV), F32) + cb_ref[...]
    for k in range(CONV_W):
        acc = acc + apad_ref[k:k + rows, :] * w_ref[k:k + 1, :]
    o_ref[...] = _ln_silu(acc, lg_ref, lb_ref).astype(o_ref.dtype)


def conv_sample(apad, w, cb, lg, lb):
    b, tp, _ = apad.shape
    t = tp - (CONV_W - 1)
    vec = pl.BlockSpec((1, D_CONV), lambda i: (0, 0))
    return pl.pallas_call(
        _conv_sample_kernel,
        grid=(b,),
        in_specs=[
            pl.BlockSpec((None, tp, D_CONV), lambda i: (i, 0, 0)),
            pl.BlockSpec((CONV_W, D_CONV), lambda i: (0, 0)),
            vec, vec, vec,
        ],
        out_specs=pl.BlockSpec((None, t, D_CONV), lambda i: (i, 0, 0)),
        out_shape=jax.ShapeDtypeStruct((b, t, D_CONV), BF16),
        compiler_params=_params(("parallel",), 32),
        name="conv_sample",
    )(apad, w, cb, lg, lb)


def _dil_prompt_kernel(*refs):
    qkv_refs = refs[:3 * N_DGROUPS]
    b_ref, qbuf, kbuf, vbuf, o_sc, l_sc, stage = refs[3 * N_DGROUPS:]
    t = b_ref.shape[0]
    n = BAND
    qi = lax.broadcasted_iota(jnp.int32, (n, 2 * n), 0)
    kj = lax.broadcasted_iota(jnp.int32, (n, 2 * n), 1)
    band = (kj > qi) & (kj <= qi + n)

    for gi, (win, dil) in enumerate(DIL_GROUPS):
        assert win // dil == n
        q_ref, k_ref, v_ref = qkv_refs[3 * gi:3 * gi + 3]
        s_len = t // dil
        n_blk = s_len // n
        pitch = s_len + n

        def place(r, q_rows, k_rows, v_rows, s_len=s_len, pitch=pitch):
            k0 = pl.multiple_of(r * pitch, n)
            qbuf[pl.ds(pl.multiple_of(r * s_len, n), s_len), :] = q_rows.astype(BF16)
            kbuf[pl.ds(k0, n), :] = jnp.zeros((n, HEAD_DIM), BF16)
            vbuf[pl.ds(k0, n), :] = jnp.zeros((n, HEAD_DIM), BF16)
            kbuf[pl.ds(k0 + n, s_len), :] = k_rows.astype(BF16)
            vbuf[pl.ds(k0 + n, s_len), :] = v_rows.astype(BF16)

        if dil > GATHER_STRIDE and dil % GATHER_STRIDE == 0:
            outer = dil // GATHER_STRIDE
            mid = t // GATHER_STRIDE

            def gather(r_in, carry, refs=(q_ref, k_ref, v_ref), s_len=s_len, outer=outer, mid=mid, place=place):
                for w, ref in enumerate(refs):
                    stage[w, 0:mid, :] = ref[pl.ds(r_in, mid, stride=GATHER_STRIDE), :]
                for m in range(outer):
                    place(r_in + m * GATHER_STRIDE,
                          *[stage[w, pl.ds(m, s_len, stride=outer), :] for w in range(3)])
                return carry

            lax.fori_loop(0, GATHER_STRIDE, gather, 0)
        else:
            def gather(r, carry, refs=(q_ref, k_ref, v_ref), dil=dil, s_len=s_len, place=place):
                rows = pl.ds(r, s_len, stride=dil) if dil > 1 else pl.ds(0, s_len)
                place(r, *[ref[rows, :] for ref in refs])
                return carry

            lax.fori_loop(0, dil, gather, 0)

        def units(it, carry, dil=dil, n_blk=n_blk, pitch=pitch, gi=gi):
            for j in range(ATTN_UNROLL):
                u = it * ATTN_UNROLL + j
                if n_blk == 1:
                    r, blk = u, 0
                elif dil == 1:
                    r, blk = 0, u
                else:
                    r, blk = lax.div(u, jnp.int32(n_blk)), lax.rem(u, jnp.int32(n_blk))
                q = qbuf[pl.ds(pl.multiple_of(u * n, n), n), :]
                k0 = pl.multiple_of(r * pitch + blk * n, n)
                mask = band & (kj >= jnp.where(blk == 0, n, 0))
                s = jnp.where(mask, _dot_nt(q, kbuf[pl.ds(k0, 2 * n), :]), NEG)
                o, lse = _softmax_pv(s, vbuf[pl.ds(k0, 2 * n), :])
                start = r + blk * (n * dil)
                dst = pl.ds(start, n, stride=dil) if dil > 1 else pl.ds(pl.multiple_of(start, n), n)
                o_sc[gi, dst, :] = o
                l_sc[gi, dst, :] = jnp.broadcast_to(lse, (n, HEAD_DIM))
            return carry

        assert (dil * n_blk) % ATTN_UNROLL == 0
        lax.fori_loop(0, dil * n_blk // ATTN_UNROLL, units, 0)

    def merge(c, carry):
        rows = pl.ds(pl.multiple_of(c * MERGE_ROWS, MERGE_ROWS), MERGE_ROWS)
        outs = [o_sc[gi, rows, :] for gi in range(N_DGROUPS)]
        lses = [l_sc[gi, rows, :] for gi in range(N_DGROUPS)]
        b_ref[rows, :] = _merge3(outs, lses).astype(b_ref.dtype)
        return carry

    lax.fori_loop(0, t // MERGE_ROWS, merge, 0)


def dilated_attn_prompt(qkv, batch, seq):
    assert seq % (BAND * max(d for _, d in DIL_GROUPS)) == 0 and seq % MERGE_ROWS == 0
    kv_rows = max(seq + dil * BAND for _, dil in DIL_GROUPS)

    def spec(which, gi):
        return pl.BlockSpec((seq, HEAD_DIM), lambda b, h: (b, (which * N_DGROUPS + gi) * HEADS + h))

    in_specs = [spec(which, gi) for gi in range(N_DGROUPS) for which in range(3)]
    return pl.pallas_call(
        _dil_prompt_kernel,
        grid=(batch, HEADS),
        in_specs=in_specs,
        out_specs=pl.BlockSpec((seq, HEAD_DIM), lambda b, h: (b, h)),
        out_shape=jax.ShapeDtypeStruct((batch * seq, D_GRP), BF16),
        scratch_shapes=[pltpu.VMEM((seq, HEAD_DIM), BF16), pltpu.VMEM((kv_rows, HEAD_DIM), BF16),
                        pltpu.VMEM((kv_rows, HEAD_DIM), BF16),
                        pltpu.VMEM((N_DGROUPS, seq, HEAD_DIM), F32), pltpu.VMEM((N_DGROUPS, seq, HEAD_DIM), F32),
                        pltpu.VMEM((3, seq // GATHER_STRIDE, HEAD_DIM), F32)],
        compiler_params=_params(("parallel", "parallel"), 40),
        name="dil_attn_prompt",
    )(*([qkv] * (3 * N_DGROUPS)))


def _dil_sample_kernel(q_ref, kn_ref, vn_ref, *rest):
    cache_refs, b_ref = rest[:2 * N_DGROUPS], rest[2 * N_DGROUPS]
    ds = q_ref.shape[0]
    qn = lax.broadcasted_iota(jnp.int32, (ds, ds), 0)
    pn = lax.broadcasted_iota(jnp.int32, (ds, ds), 1)
    for h in range(HEADS):
        outs, lses = [], []
        for gi, (win, dil) in enumerate(DIL_GROUPS):
            n = win // dil
            kc_ref, vc_ref = cache_refs[2 * gi], cache_refs[2 * gi + 1]
            cache_len = kc_ref.shape[0] // HEADS
            cols = slice(gi * D_GRP + h * HEAD_DIM, gi * D_GRP + (h + 1) * HEAD_DIM)
            qi = lax.broadcasted_iota(jnp.int32, (ds, cache_len), 0)
            pj = lax.broadcasted_iota(jnp.int32, (ds, cache_len), 1)
            dist = cache_len + qi - pj
            mask_c = (dist % dil == 0) & (dist <= dil * (n - 1))
            mask_n = (qn >= pn) & ((qn - pn) % dil == 0)
            q = q_ref[:, cols].astype(BF16)
            s_c = jnp.where(mask_c, _dot_nt(q, _head_rows(kc_ref, h, cache_len).astype(BF16)), NEG)
            s_n = jnp.where(mask_n, _dot_nt(q, kn_ref[:, cols].astype(BF16)), NEG)
            mx = jnp.maximum(jnp.max(s_c, axis=-1, keepdims=True), jnp.max(s_n, axis=-1, keepdims=True))
            p_c = jnp.exp(s_c - mx)
            p_n = jnp.exp(s_n - mx)
            l = jnp.sum(p_c, axis=-1, keepdims=True) + jnp.sum(p_n, axis=-1, keepdims=True)
            o = (_dot(p_c.astype(BF16), _head_rows(vc_ref, h, cache_len).astype(BF16))
                 + _dot(p_n.astype(BF16), vn_ref[:, cols].astype(BF16)))
            outs.append(o / l)
            lses.append(mx + jnp.log(l))
        b_ref[:, h * HEAD_DIM:(h + 1) * HEAD_DIM] = _merge3(outs, lses).astype(b_ref.dtype)


def dilated_attn_sample(qkv, caches, layer):
    b, ds, _ = qkv.shape
    cache_specs = []
    for gi, (win, dil) in enumerate(DIL_GROUPS):
        rows = caches[2 * gi].shape[2]
        assert rows // HEADS - dil * (win // dil - 1) >= 0
        cache_specs += [pl.BlockSpec((None, None, rows, HEAD_DIM), lambda i: (layer, i, 0, 0))] * 2

    new = lambda which: pl.BlockSpec((None, ds, D_ATT), lambda i: (i, 0, which))
    return pl.pallas_call(
        _dil_sample_kernel,
        grid=(b,),
        in_specs=[new(0), new(1), new(2)] + cache_specs,
        out_specs=pl.BlockSpec((None, ds, D_GRP), lambda i: (i, 0, 0)),
        out_shape=jax.ShapeDtypeStruct((b, ds, D_GRP), BF16),
        compiler_params=_params(("parallel",), 48),
        name="dil_attn_sample",
    )(qkv, qkv, qkv, *caches)


def _mix_out_kernel(x_ref, a_ref, b_ref, wa_ref, wb_ref, y_ref):
    y_ref[...] = x_ref[...] + _dot(a_ref[...], wa_ref[...]) + _dot(b_ref[...], wb_ref[...])


def mix_out(x, a, b, w_out, layer, tm):
    m = x.shape[0]
    row = lambda width: pl.BlockSpec((tm, width), lambda i: (i, 0))
    return pl.pallas_call(
        _mix_out_kernel,
        grid=(m // tm,),
        in_specs=[row(D_MODEL), row(D_CONV), row(D_GRP),
                  pl.BlockSpec((None, D_CONV, D_MODEL), lambda i: (layer, 0, 0)),
                  pl.BlockSpec((None, D_GRP, D_MODEL), lambda i: (layer, D_CONV // D_GRP, 0))],
        out_specs=row(D_MODEL),
        out_shape=jax.ShapeDtypeStruct((m, D_MODEL), F32),
        compiler_params=_params(("parallel",), 48),
        name="mix_out",
    )(x, a, b, w_out, w_out)


def _mix_mem_kernel(x_ref, c_ref, lg_ref, lb_ref, b_ref, wa_ref, wb_ref,
                    g_ref, wq_ref, qg_ref, k_ref, v_ref, wo_ref, y_ref):
    a = _ln_silu(c_ref[...], lg_ref, lb_ref).astype(BF16)
    x1 = x_ref[...] + _dot(a, wa_ref[...]) + _dot(b_ref[...], wb_ref[...])
    y_ref[...] = _mem_attn_tail(x1, 1, g_ref, wq_ref, qg_ref, _kv_head_of(k_ref, v_ref), wo_ref)


def mix_out_mem(x, c, lg, lb, b, w_out, layer, mem_args, batch, tm):
    m = x.shape[0]
    nt = m // batch // tm
    row = lambda width: pl.BlockSpec((tm, width), lambda i, j: (i * nt + j, 0))
    vec = _resident((1, D_CONV), (0, 0))
    mem_specs, mem_ops = _mem_operands(*mem_args)
    return pl.pallas_call(
        _mix_mem_kernel,
        grid=(batch, nt),
        in_specs=[row(D_MODEL), row(D_CONV), vec, vec, row(D_GRP),
                  _resident((None, D_CONV, D_MODEL), (layer, 0, 0)),
                  _resident((None, D_GRP, D_MODEL), (layer, D_CONV // D_GRP, 0))] + mem_specs,
        out_specs=row(D_MODEL),
        out_shape=jax.ShapeDtypeStruct((m, D_MODEL), F32),
        compiler_params=_params(("parallel", "parallel"), 56),
        name="mix_out_mem",
    )(x, c, lg, lb, b, w_out, w_out, *mem_ops)


def _sgu_gated(u_ref, gv_ref, lg_ref, lb_ref, ws_ref, bs_ref):
    gv = gv_ref[...]
    mu = jnp.mean(gv, axis=-1, keepdims=True)
    vc = gv - mu
    v = vc * lax.rsqrt(jnp.mean(vc * vc, axis=-1, keepdims=True) + EPS) * lg_ref[...] + lb_ref[...]
    vb = v.astype(BF16)
    rows, cm = gv.shape[0], ws_ref.shape[1]
    gated = []
    for g in range(N_SG):
        cols = slice(g * D_SG, (g + 1) * D_SG)
        bias = bs_ref[:, g:g + 1]
        if cm == SUBLANES:
            v3 = vb[:, cols].astype(F32).reshape(rows // cm, cm, D_SG)
            wsg = ws_ref[g].astype(F32)
            sv = bias[None] + sum(v3[:, s:s + 1, :] * wsg[:, s:s + 1][None] for s in range(cm))
            sv = sv.reshape(rows, D_SG)
        else:
            sv = jnp.concatenate([_dot(ws_ref[g], vb[c * cm:(c + 1) * cm, cols]) + bias
                                  for c in range(rows // cm)], axis=0)
        gated.append((u_ref[:, cols] * sv).astype(BF16))
    return jnp.concatenate(gated, axis=-1), v


def _sgu_kernel(x_ref, u_ref, gv_ref, lg_ref, lb_ref, ws_ref, bs_ref, w_ref, *out_refs, emit_v):
    gated, v = _sgu_gated(u_ref, gv_ref, lg_ref, lb_ref, ws_ref, bs_ref)
    if emit_v:
        out_refs[1][...] = v
    out_refs[0][...] = x_ref[...] + _dot(gated, w_ref[...])


def _sgu_mem_kernel(x_ref, u_ref, gv_ref, lg_ref, lb_ref, ws_ref, bs_ref, w_ref,
                    g_ref, wq_ref, qg_ref, k_ref, v_ref, wo_ref, y_ref):
    gated, _ = _sgu_gated(u_ref, gv_ref, lg_ref, lb_ref, ws_ref, bs_ref)
    x1 = x_ref[...] + _dot(gated, w_ref[...])
    y_ref[...] = _mem_attn_tail(x1, 1, g_ref, wq_ref, qg_ref, _kv_head_of(k_ref, v_ref), wo_ref)


def sgu_out_mem(x, z, lg, lb, ws, bs, w_out, layer, mem_args, batch, tm):
    m = x.shape[0]
    cm = ws.shape[1]
    nt = m // batch // tm
    assert tm % cm == 0
    row = lambda jblk: pl.BlockSpec((tm, D_GATE), lambda i, j: (i * nt + j, jblk))
    vec = pl.BlockSpec((1, D_GATE), lambda i, j: (0, 0))
    mem_specs, mem_ops = _mem_operands(*mem_args)
    return pl.pallas_call(
        _sgu_mem_kernel,
        grid=(batch, nt),
        in_specs=[row(0), row(0), row(1), vec, vec,
                  _resident((N_SG, cm, cm), (0, 0, 0)), _resident((cm, N_SG), (0, 0)),
                  _resident((None, D_GATE, D_MODEL), (layer, 0, 0))] + mem_specs,
        out_specs=row(0),
        out_shape=jax.ShapeDtypeStruct((m, D_MODEL), F32),
        compiler_params=_params(("parallel", "parallel"), 56),
        name="sgu_out_mem",
    )(x, z, z, lg, lb, ws, bs, w_out, *mem_ops)


def sgu_out(x, z, lg, lb, ws, bs, w_out, layer, tm, emit_v):
    m = x.shape[0]
    cm = ws.shape[1]
    assert tm % cm == 0
    row = lambda jblk: pl.BlockSpec((tm, D_GATE), lambda i: (i, jblk))
    vec = pl.BlockSpec((1, D_GATE), lambda i: (0, 0))
    out_specs = [row(0)]
    out_shape = [jax.ShapeDtypeStruct((m, D_MODEL), F32)]
    if emit_v:
        out_specs.append(row(0))
        out_shape.append(jax.ShapeDtypeStruct((m, D_GATE), F32))
    res = pl.pallas_call(
        functools.partial(_sgu_kernel, emit_v=emit_v),
        grid=(m // tm,),
        in_specs=[row(0), row(0), row(1), vec, vec,
                  pl.BlockSpec((N_SG, cm, cm), lambda i: (0, 0, 0)),
                  pl.BlockSpec((cm, N_SG), lambda i: (0, 0)),
                  pl.BlockSpec((None, D_GATE, D_MODEL), lambda i: (layer, 0, 0))],
        out_specs=out_specs,
        out_shape=out_shape,
        compiler_params=_params(("parallel",), 52),
        name="sgu_out",
    )(x, z, z, lg, lb, ws, bs, w_out)
    return res if emit_v else (res[0], None)


def _mem_attn_tail(x, n_seq, g_ref, wq_ref, qg_ref, kv_head, wo_ref):
    tm = x.shape[0] // n_seq
    h = (_rms(x) * g_ref[...]).astype(BF16)
    q = _head_norm(_dot(h, wq_ref[...]), qg_ref[...]).astype(BF16)
    per_seq = []
    for b in range(n_seq):
        outs = []
        for hd in range(HEADS):
            k, v = kv_head(b, hd)
            o, _ = _softmax_pv(_dot_nt(q[b * tm:(b + 1) * tm, hd * HEAD_DIM:(hd + 1) * HEAD_DIM], k), v)
            outs.append(o.astype(BF16))
        per_seq.append(jnp.concatenate(outs, axis=-1))
    o = jnp.concatenate(per_seq, axis=0) if n_seq > 1 else per_seq[0]
    return x + _dot(o, wo_ref[...])


def _kv_head_of(k_ref, v_ref):
    n_mem = k_ref.shape[0] // HEADS
    return lambda b, hd: (_head_rows(k_ref, hd, n_mem).astype(BF16), _head_rows(v_ref, hd, n_mem).astype(BF16))


def _resident(shape, index):
    return pl.BlockSpec(shape, lambda *_: index, pipeline_mode=pl.Buffered(1))


def _mem_operands(g, wq, q_gain, k, v, kv_layer, wo, layer):
    specs = [_resident((1, D_MODEL), (0, 0)),
             _resident((None, D_MODEL, D_MEMATT), (layer, 0, 0)),
             _resident((1, D_MEMATT), (0, 0)),
             pl.BlockSpec((None, None, k.shape[2], HEAD_DIM), lambda i, j: (kv_layer, i, 0, 0)),
             pl.BlockSpec((None, None, k.shape[2], HEAD_DIM), lambda i, j: (kv_layer, i, 0, 0)),
             _resident((None, D_MEMATT, D_MODEL), (layer, 0, 0))]
    return specs, (g, wq, q_gain, k, v, wo)


def _mem_attn_kernel(x_ref, g_ref, wq_ref, qg_ref, k_ref, v_ref, wo_ref, y_ref):
    bb, tm, _ = x_ref.shape
    n_mem = k_ref.shape[1] // HEADS

    def kv_head(b, hd):
        rows = pl.ds(hd, n_mem, stride=HEADS)
        return k_ref[b, rows, :].astype(BF16), v_ref[b, rows, :].astype(BF16)

    x = x_ref[...].reshape(bb * tm, D_MODEL)
    y_ref[...] = _mem_attn_tail(x, bb, g_ref, wq_ref, qg_ref, kv_head, wo_ref).reshape(bb, tm, D_MODEL)


def mem_attn(x, g, wq, q_gain, k, v, kv_layer, wo, layer, bb, tm):
    b, t, _ = x.shape
    full = lambda shape: pl.BlockSpec(shape, lambda i, j: (0,) * len(shape))
    kv = pl.BlockSpec((None, bb, k.shape[2], HEAD_DIM), lambda i, j: (kv_layer, i, 0, 0))
    xs = pl.BlockSpec((bb, tm, D_MODEL), lambda i, j: (i, j, 0))
    return pl.pallas_call(
        _mem_attn_kernel,
        grid=(b // bb, t // tm),
        in_specs=[xs, full((1, D_MODEL)),
                  pl.BlockSpec((None, D_MODEL, D_MEMATT), lambda i, j: (layer, 0, 0)),
                  full((1, D_MEMATT)), kv, kv,
                  pl.BlockSpec((None, D_MEMATT, D_MODEL), lambda i, j: (layer, 0, 0))],
        out_specs=xs,
        out_shape=jax.ShapeDtypeStruct((b, t, D_MODEL), F32),
        compiler_params=_params(("parallel", "parallel"), 40),
        name="mem_attn",
    )(x, g, wq, q_gain, k, v, wo)


def _tile_heads(g):
    return jnp.tile(g, HEADS)


def _row_tile(m):
    return min(m, 1024)


def kernel(x_prompt, x_sample, mem_prompt, state_conv, cache_k_w128, cache_v_w128, cache_k_w512, cache_v_w512,
           cache_k_w2048, cache_v_w2048, cache_mem_k, cache_mem_v, g_mix, w_in_e, conv_w, conv_b, conv_ln_g,
           conv_ln_b, q_norm_e, k_norm_e, w_out_e, w_in_o, b_in_o, v_ln_g, v_ln_b, w_s, b_s, w_out_o, g_xmem,
           g_mem, wq_mem, wk_mem, wv_mem, q_norm_mem, k_norm_mem, wo_mem, g_ffn, w_ffn1, w_ffn2):
    depth = g_mix.shape[0]
    bp, tp, _ = x_prompt.shape
    bs, ts, _ = x_sample.shape
    mp, ms = bp * tp, bs * ts
    scale = HEAD_DIM ** -0.5
    row = lambda v: v.reshape(1, -1)
    th_rows = lambda c: c.reshape(c.shape[0], c.shape[1], c.shape[2] * HEADS, HEAD_DIM)
    caches = [th_rows(c) for c in (cache_k_w128, cache_v_w128, cache_k_w512, cache_v_w512,
                                   cache_k_w2048, cache_v_w2048)]
    mem_k_s, mem_v_s = th_rows(cache_mem_k), th_rows(cache_mem_v)

    w_in_e, w_out_e, w_in_o, w_out_o = (w.astype(BF16) for w in (w_in_e, w_out_e, w_in_o, w_out_o))
    qkv_group0 = 2 * D_CONV // D_GRP
    wq_mem, wo_mem = wq_mem.astype(BF16), wo_mem.astype(BF16)
    wkv_mem = jnp.concatenate([wk_mem, wv_mem], axis=2).astype(BF16)

    xp = x_prompt.reshape(mp, D_MODEL)
    xs = x_sample.reshape(ms, D_MODEL)
    mem = mem_prompt.reshape(bp * N_MEM, D_MODEL)
    tm_p, tm_s, tm_mem = _row_tile(mp), _row_tile(ms), _row_tile(bp * N_MEM)
    assert tp % tm_p == 0 or tm_p % tp == 0
    assert ms == tm_s

    conv_pl, conv_sl, kv_pl, kv_sl, memk_pl, memv_pl, chunk_sl = [], [], [], [], [], [], []
    for i in range(depth):
        j = i // 2
        kgain = jnp.stack([_tile_heads(k_norm_mem[i]), jnp.ones((D_MEMATT,), F32)])
        mk, mv = headnorm_proj(mem, row(g_mem[i]), wkv_mem, i, 0, kgain, 1, tm_mem, D_MEMATT,
                               [(0, tm_mem, 1), (1, tm_mem, 1)], False)
        memk_pl.append(mk.reshape(bp, N_MEM, HEADS, HEAD_DIM))
        memv_pl.append(mv.reshape(bp, N_MEM, HEADS, HEAD_DIM))
        qgain = row(_tile_heads(q_norm_mem[i]) * scale)
        mem_p = (row(g_xmem[i]), wq_mem, qgain, mk.reshape(1, bp, N_MEM * HEADS, HEAD_DIM),
                 mv.reshape(1, bp, N_MEM * HEADS, HEAD_DIM), 0, wo_mem, i)
        mem_s = (row(g_xmem[i]), wq_mem, qgain, mem_k_s, mem_v_s, i, wo_mem, i)

        if i % 2 == 0:
            gains = jnp.concatenate([
                jnp.stack([_tile_heads(q_norm_e[j, gi]) * scale for gi in range(N_DGROUPS)]),
                jnp.stack([_tile_heads(k_norm_e[j, gi]) for gi in range(N_DGROUPS)]),
                jnp.ones((N_DGROUPS, D_GRP), F32)])
            conv_args = (conv_w[j], row(conv_b[j]), row(conv_ln_g[j]), row(conv_ln_b[j]))

            def kv_tiles(seq, tm):
                tiles = []
                for gi, (win, _) in enumerate(DIL_GROUPS):
                    keep = min(win, seq)
                    assert keep == seq or (keep <= tm and seq % tm == 0)
                    spec = (tm, 1) if keep == seq else (keep, seq // tm)
                    tiles += [((1 + which) * N_DGROUPS + gi,) + spec for which in range(2)]
                return tiles

            a = glu_proj(xp, row(g_mix[i]), w_in_e, j, tm_p)
            qkv, *new_kv = headnorm_proj(xp, row(g_mix[i]), w_in_e, j, qkv_group0, gains, 2 * N_DGROUPS, tm_p, D_ATT,
                                         kv_tiles(tp, tm_p), True)
            kv_pl.append([kv.reshape(bp, -1, HEADS, HEAD_DIM) for kv in new_kv])
            a3 = a.reshape(bp, tp, D_CONV)
            conv_pl.append(a3[:, tp - (CONV_W - 1):])
            c_out = conv_prompt(a3, *conv_args[:2], tt=512).reshape(mp, D_CONV)
            b_out = dilated_attn_prompt(qkv, bp, tp)
            xp = mix_out_mem(xp, c_out, *conv_args[2:], b_out, w_out_e, j, mem_p, bp, 512)

            a = glu_proj(xs, row(g_mix[i]), w_in_e, j, tm_s)
            qkv, *new_kv = headnorm_proj(xs, row(g_mix[i]), w_in_e, j, qkv_group0, gains, 2 * N_DGROUPS, tm_s, D_ATT,
                                         kv_tiles(ts, tm_s), True)
            kv_sl.append([kv.reshape(bs, -1, HEADS, HEAD_DIM) for kv in new_kv])
            apad = jnp.concatenate([state_conv[j], a.reshape(bs, ts, D_CONV)], axis=1)
            conv_sl.append(apad[:, apad.shape[1] - (CONV_W - 1):])
            a_out = conv_sample(apad, *conv_args).reshape(ms, D_CONV)
            b_out = dilated_attn_sample(qkv.reshape(bs, ts, 3 * D_ATT), caches, j).reshape(ms, D_GRP)
            xs = mix_out(xs, a_out, b_out, w_out_e, j, tm_s)
        else:
            sgu_vecs = (row(v_ln_g[j]), row(v_ln_b[j]))
            tril = jnp.tril(jnp.ones((CHUNK, CHUNK), F32))

            def spatial(t):
                c = min(CHUNK, t)
                assert c in (SUBLANES, CHUNK) and t % c == 0
                return (w_s[j][:, :c, :c] * tril[:c, :c]).astype(BF16), b_s[j][:, :c].T

            z = gelu_proj(xp, row(g_mix[i]), w_in_o, j, row(b_in_o[j]), tm_p)
            xp = sgu_out_mem(xp, z, *sgu_vecs, *spatial(tp), w_out_o, j, mem_p, bp, 512)

            z = gelu_proj(xs, row(g_mix[i]), w_in_o, j, row(b_in_o[j]), tm_s)
            xs, v = sgu_out(xs, z, *sgu_vecs, *spatial(ts), w_out_o, j, 256, emit_v=True)
            chunk_sl.append(v.reshape(bs, ts, D_GATE))

        xs = mem_attn(xs.reshape(bs, ts, D_MODEL), *mem_s, 8, ts).reshape(ms, D_MODEL)
        xs, w1, w2 = ffn_cast(xs, row(g_ffn[i]), w_ffn1, w_ffn2, i, 512)
        xp = ffn(xp, row(g_ffn[i]), w1, w2, 0, tm_p, 1024)

    stack = lambda items: jnp.stack(items)
    kv_p = [stack([kv[n] for kv in kv_pl]) for n in range(2 * N_DGROUPS)]
    kv_s = [stack([kv[n] for kv in kv_sl]) for n in range(2 * N_DGROUPS)]
    return (xp.reshape(bp, tp, D_MODEL), xs.reshape(bs, ts, D_MODEL), stack(conv_pl), stack(conv_sl),
            *kv_p, *kv_s, stack(memk_pl), stack(memv_pl), stack(chunk_sl))
```

```python
import functools
import math

import numpy as np
import jax
import jax.numpy as jnp
from jax import lax
from jax.experimental import pallas as pl
from jax.experimental.pallas import tpu as pltpu

D_MODEL = 2048
EPS = 1e-6
NEG = -1e30
D_CONV = D_MODEL // 2
CONV_W = 31
HEAD_DIM = 128
DIL_GROUPS = ((128, 1), (512, 4), (2048, 16))
N_DGROUPS = len(DIL_GROUPS)
HEADS = 4
D_GRP = HEADS * HEAD_DIM
D_ATT = N_DGROUPS * D_GRP
CHUNK = 128
D_GATE = D_MODEL
N_SG = 8
D_SG = D_GATE // N_SG
N_MEM = 256
D_MEMATT = HEADS * HEAD_DIM
D_FF = 4 * D_MODEL

F32 = jnp.float32
BF16 = jnp.bfloat16
MIB = 1024 * 1024
NORM_ROWS = 256
CONV_HALO = 32
SUBLANES = 8
MERGE_ROWS = 256
BAND = DIL_GROUPS[0][0] // DIL_GROUPS[0][1]
GATHER_STRIDE = 4
ATTN_UNROLL = 16
CONV_ROWS = 16
LANES = 128


def _params(sem, vmem_mib):
    return pltpu.CompilerParams(dimension_semantics=sem, vmem_limit_bytes=vmem_mib * MIB)


def _dot(a, b):
    return jnp.dot(a, b, preferred_element_type=F32)


def _dot_nt(a, b):
    return lax.dot_general(a, b, (((1,), (1,)), ((), ())), preferred_element_type=F32)


def _rms(x):
    return x * lax.rsqrt(jnp.mean(x * x, axis=-1, keepdims=True) + EPS)


def _norm_to_scratch(x_ref, g_ref, h_ref):
    rows = x_ref.shape[0]
    step = min(NORM_ROWS, rows)

    def body(c, carry):
        r = pl.multiple_of(c * step, step)
        x = x_ref[pl.ds(r, step), :]
        h_ref[pl.ds(r, step), :] = (_rms(x) * g_ref[...]).astype(BF16)
        return carry

    lax.fori_loop(0, rows // step, body, 0)


def _head_norm(acc, gain):
    parts = [_rms(acc[:, h * HEAD_DIM:(h + 1) * HEAD_DIM]) for h in range(acc.shape[1] // HEAD_DIM)]
    return jnp.concatenate(parts, axis=-1) * gain


def _head_rows(ref, h, rows):
    return ref[pl.ds(h, rows, stride=HEADS), :]


def _softmax_pv(s, v):
    mx = jnp.max(s, axis=-1, keepdims=True)
    p = jnp.exp(s - mx)
    l = jnp.sum(p, axis=-1, keepdims=True)
    return _dot(p.astype(BF16), v) / l, mx + jnp.log(l)


def _merge3(outs, lses):
    mx = jnp.maximum(jnp.maximum(lses[0], lses[1]), lses[2])
    es = [jnp.exp(l - mx) for l in lses]
    return (es[0] * outs[0] + es[1] * outs[1] + es[2] * outs[2]) / (es[0] + es[1] + es[2])


def _glu_kernel(x_ref, g_ref, wv_ref, wg_ref, o_ref, h_ref):
    @pl.when(pl.program_id(1) == 0)
    def _():
        _norm_to_scratch(x_ref, g_ref, h_ref)

    h = h_ref[...]
    val = _dot(h, wv_ref[...])
    gate = _dot(h, wg_ref[...])
    o_ref[...] = val * jax.nn.sigmoid(gate)


def glu_proj(x, g, w_in, layer, tm):
    m = x.shape[0]
    tn = D_CONV
    nj = D_CONV // tn
    return pl.pallas_call(
        _glu_kernel,
        grid=(m // tm, nj),
        in_specs=[
            pl.BlockSpec((tm, D_MODEL), lambda i, j: (i, 0)),
            pl.BlockSpec((1, D_MODEL), lambda i, j: (0, 0)),
            pl.BlockSpec((None, D_MODEL, tn), lambda i, j: (layer, 0, j)),
            pl.BlockSpec((None, D_MODEL, tn), lambda i, j: (layer, 0, j + nj)),
        ],
        out_specs=pl.BlockSpec((tm, tn), lambda i, j: (i, j)),
        out_shape=jax.ShapeDtypeStruct((m, D_CONV), F32),
        scratch_shapes=[pltpu.VMEM((tm, D_MODEL), BF16)],
        compiler_params=_params(("parallel", "arbitrary"), 48),
        name="glu_proj",
    )(x, g, w_in, w_in)


def _headnorm_proj_kernel(x_ref, g_ref, *rest, per_tile, n_normed, emit_main, th_tiles):
    w_refs, gain_ref, rest = rest[:per_tile], rest[per_tile], rest[per_tile + 1:]
    n_th = len(th_tiles)
    if emit_main:
        o_ref, th_refs, h_ref = rest[0], rest[1:1 + n_th], rest[1 + n_th]
    else:
        th_refs, h_ref, o_ref = rest[:n_th], rest[n_th], rest[n_th + 1]
    i, j = pl.program_id(0), pl.program_id(1)
    tm = x_ref.shape[0]

    @pl.when(j == 0)
    def _():
        _norm_to_scratch(x_ref, g_ref, h_ref)

    h = h_ref[...]
    for s in range(per_tile):
        cols = slice(s * D_GRP, (s + 1) * D_GRP)
        acc = _dot(h, w_refs[s][...])
        o_ref[:, cols] = jnp.where(j * per_tile + s < n_normed, _head_norm(acc, gain_ref[0, :, cols]), acc)

    for th_ref, (col, period) in zip(th_refs, th_tiles):
        rows = th_ref.shape[0] // HEADS
        c0 = col % per_tile * D_GRP

        @pl.when((j == col // per_tile) & (i % period == period - 1))
        def _():
            for h in range(HEADS):
                th_ref[pl.ds(h, rows, stride=HEADS), :] = o_ref[tm - rows:, c0 + h * HEAD_DIM:c0 + (h + 1) * HEAD_DIM]


def _th_index(period, i, j):
    return (i // period, 0)


def _w_group_index(layer, first, per_tile, s, i, j):
    return (layer, 0, first + j * per_tile + s)


def headnorm_proj(x, g, w, layer, group0, gains, n_normed, tm, tn, th_tiles, emit_main):
    m = x.shape[0]
    n = gains.shape[0] * D_GRP
    nj = n // tn
    per_tile = tn // D_GRP
    assert n % tn == 0 and tn % D_GRP == 0
    out_specs, out_shape = [], []
    if emit_main:
        out_specs.append(pl.BlockSpec((tm, tn), lambda i, j: (i, j)))
        out_shape.append(jax.ShapeDtypeStruct((m, n), F32))
    for _, keep, period in th_tiles:
        assert keep <= tm and (m // tm) % period == 0
        mode = {} if period == 1 else {"pipeline_mode": pl.Buffered(1)}
        out_specs.append(pl.BlockSpec((keep * HEADS, HEAD_DIM), functools.partial(_th_index, period), **mode))
        out_shape.append(jax.ShapeDtypeStruct((m // tm // period * keep * HEADS, HEAD_DIM), F32))
    scratch = [pltpu.VMEM((tm, D_MODEL), BF16)]
    if not emit_main:
        scratch.append(pltpu.VMEM((tm, tn), F32))
    return pl.pallas_call(
        functools.partial(_headnorm_proj_kernel, per_tile=per_tile, n_normed=n_normed, emit_main=emit_main,
                          th_tiles=tuple((col, period) for col, _, period in th_tiles)),
        grid=(m // tm, nj),
        in_specs=[pl.BlockSpec((tm, D_MODEL), lambda i, j: (i, 0)),
                  pl.BlockSpec((1, D_MODEL), lambda i, j: (0, 0))]
        + [pl.BlockSpec((None, D_MODEL, D_GRP), functools.partial(_w_group_index, layer, group0, per_tile, s))
           for s in range(per_tile)]
        + [pl.BlockSpec((1, 1, tn), lambda i, j: (j, 0, 0))],
        out_specs=out_specs,
        out_shape=out_shape,
        scratch_shapes=scratch,
        compiler_params=_params(("arbitrary", "arbitrary"), 61),
        name="headnorm_proj",
    )(x, g, *([w] * per_tile), gains.reshape(nj, 1, tn))


def _gelu_proj_kernel(x_ref, g_ref, w_ref, b_ref, o_ref, h_ref):
    @pl.when(pl.program_id(1) == 0)
    def _():
        _norm_to_scratch(x_ref, g_ref, h_ref)

    z = _dot(h_ref[...], w_ref[...]) + b_ref[...]
    o_ref[...] = 0.5 * z * (1.0 + lax.erf(z * np.float32(math.sqrt(0.5))))


def gelu_proj(x, g, w, layer, b, tm):
    m = x.shape[0]
    n = w.shape[2]
    tn = 1024
    return pl.pallas_call(
        _gelu_proj_kernel,
        grid=(m // tm, n // tn),
        in_specs=[
            pl.BlockSpec((tm, D_MODEL), lambda i, j: (i, 0)),
            pl.BlockSpec((1, D_MODEL), lambda i, j: (0, 0)),
            pl.BlockSpec((None, D_MODEL, tn), lambda i, j: (layer, 0, j)),
            pl.BlockSpec((1, tn), lambda i, j: (0, j)),
        ],
        out_specs=pl.BlockSpec((tm, tn), lambda i, j: (i, j)),
        out_shape=jax.ShapeDtypeStruct((m, n), F32),
        scratch_shapes=[pltpu.VMEM((tm, D_MODEL), BF16)],
        compiler_params=_params(("parallel", "arbitrary"), 48),
        name="gelu_proj",
    )(x, g, w, b)


def _ffn_kernel(x_ref, g_ref, w1_ref, w2_ref, o_ref, h_ref):
    @pl.when(pl.program_id(1) == 0)
    def _():
        _norm_to_scratch(x_ref, g_ref, h_ref)
        o_ref[...] = x_ref[...]

    hid = jnp.maximum(_dot(h_ref[...], w1_ref[...]), 0.0)
    o_ref[...] += _dot((hid * hid).astype(BF16), w2_ref[...])


def ffn(x, g, w1, w2, layer, tm, tf):
    m = x.shape[0]
    return pl.pallas_call(
        _ffn_kernel,
        grid=(m // tm, D_FF // tf),
        in_specs=[
            pl.BlockSpec((tm, D_MODEL), lambda i, f: (i, 0)),
            pl.BlockSpec((1, D_MODEL), lambda i, f: (0, 0)),
            pl.BlockSpec((None, D_MODEL, tf), lambda i, f: (layer, 0, f)),
            pl.BlockSpec((None, tf, D_MODEL), lambda i, f: (layer, f, 0)),
        ],
        out_specs=pl.BlockSpec((tm, D_MODEL), lambda i, f: (i, 0)),
        out_shape=jax.ShapeDtypeStruct((m, D_MODEL), F32),
        scratch_shapes=[pltpu.VMEM((tm, D_MODEL), BF16)],
        compiler_params=_params(("parallel", "arbitrary"), 60),
        name="ffn",
    )(x, g, w1, w2)


def _ffn_cast_kernel(x_ref, g_ref, w1_ref, w2_ref, o_ref, w1b_ref, w2b_ref, h_ref):
    @pl.when(pl.program_id(0) == 0)
    def _():
        _norm_to_scratch(x_ref, g_ref, h_ref)
        o_ref[...] = x_ref[...]

    w1 = w1_ref[...].astype(BF16)
    w2 = w2_ref[...].astype(BF16)
    w1b_ref[...] = w1
    w2b_ref[...] = w2
    hid = jnp.maximum(_dot(h_ref[...], w1), 0.0)
    o_ref[...] += _dot((hid * hid).astype(BF16), w2)


def ffn_cast(x, g, w1, w2, layer, tf):
    m = x.shape[0]
    const = lambda shape: pl.BlockSpec(shape, lambda f: (0,) * len(shape))
    return pl.pallas_call(
        _ffn_cast_kernel,
        grid=(D_FF // tf,),
        in_specs=[const((m, D_MODEL)), const((1, D_MODEL)),
                  pl.BlockSpec((None, D_MODEL, tf), lambda f: (layer, 0, f)),
                  pl.BlockSpec((None, tf, D_MODEL), lambda f: (layer, f, 0))],
        out_specs=[const((m, D_MODEL)),
                   pl.BlockSpec((None, D_MODEL, tf), lambda f: (0, 0, f)),
                   pl.BlockSpec((None, tf, D_MODEL), lambda f: (0, f, 0))],
        out_shape=[jax.ShapeDtypeStruct((m, D_MODEL), F32),
                   jax.ShapeDtypeStruct((1, D_MODEL, D_FF), BF16),
                   jax.ShapeDtypeStruct((1, D_FF, D_MODEL), BF16)],
        scratch_shapes=[pltpu.VMEM((m, D_MODEL), BF16)],
        compiler_params=_params(("arbitrary",), 48),
        name="ffn_cast",
    )(x, g, w1, w2)


def _ln_silu(acc, lg_ref, lb_ref):
    mu = jnp.mean(acc, axis=-1, keepdims=True)
    xc = acc - mu
    y = xc * lax.rsqrt(jnp.mean(xc * xc, axis=-1, keepdims=True) + EPS)
    y = y * lg_ref[...] + lb_ref[...]
    return y * jax.nn.sigmoid(y)


def _conv_prompt_kernel(a_ref, halo_ref, w_ref, cb_ref, o_ref, sh_ref, wrep_ref):
    tt = a_ref.shape[0]
    first = pl.program_id(1) == 0
    for k in range(CONV_W):
        wrep_ref[k] = jnp.broadcast_to(w_ref[k:k + 1, :], (SUBLANES, D_CONV))
    sh_ref[0, 0:CONV_HALO, :] = jnp.where(first, 0.0, halo_ref[...])
    sh_ref[0, CONV_HALO:, :] = a_ref[...]
    span = tt + CONV_HALO - SUBLANES
    for s in range(1, SUBLANES):
        sh_ref[s, 0:span, :] = sh_ref[0, s:s + span, :]
    lead = CONV_HALO - (CONV_W - 1)
    groups = CONV_ROWS // SUBLANES

    def conv_rows(c, carry):
        r = pl.multiple_of(c * CONV_ROWS, CONV_ROWS)
        acc = jnp.zeros((groups, SUBLANES, D_CONV), F32) + cb_ref[...]
        for k in range(CONV_W):
            q, s = divmod(k + lead, SUBLANES)
            x = sh_ref[s, pl.ds(r + q * SUBLANES, CONV_ROWS), :]
            acc = acc + x.reshape(groups, SUBLANES, D_CONV) * wrep_ref[k]
        o_ref[pl.ds(r, CONV_ROWS), :] = acc.reshape(CONV_ROWS, D_CONV)
        return carry

    lax.fori_loop(0, tt // CONV_ROWS, conv_rows, 0)


def conv_prompt(a, w, cb, tt):
    b, t, _ = a.shape
    hb = tt // CONV_HALO
    vec = pl.BlockSpec((1, D_CONV), lambda i, j: (0, 0))
    return pl.pallas_call(
        _conv_prompt_kernel,
        grid=(b, t // tt),
        in_specs=[
            pl.BlockSpec((None, tt, D_CONV), lambda i, j: (i, j, 0)),
            pl.BlockSpec((None, CONV_HALO, D_CONV), lambda i, j: (i, jnp.maximum(j * hb - 1, 0), 0)),
            pl.BlockSpec((CONV_W, D_CONV), lambda i, j: (0, 0)),
            vec,
        ],
        out_specs=pl.BlockSpec((None, tt, D_CONV), lambda i, j: (i, j, 0)),
        out_shape=jax.ShapeDtypeStruct((b, t, D_CONV), F32),
        scratch_shapes=[pltpu.VMEM((SUBLANES, CONV_HALO + tt, D_CONV), F32),
                        pltpu.VMEM((CONV_W, SUBLANES, D_CONV), F32)],
        compiler_params=_params(("parallel", "arbitrary"), 40),
        name="conv_prompt",
    )(a, a, w, cb)


def _conv_sample_kernel(apad_ref, w_ref, cb_ref, lg_ref, lb_ref, o_ref):
    rows = o_ref.shape[0]
    acc = jnp.zeros((rows, D_CONV), F32) + cb_ref[...]
    for k in range(CONV_W):
        acc = acc + apad_ref[k:k + rows, :] * w_ref[k:k + 1, :]
    o_ref[...] = _ln_silu(acc, lg_ref, lb_ref).astype(o_ref.dtype)


def conv_sample(apad, w, cb, lg, lb):
    b, tp, _ = apad.shape
    t = tp - (CONV_W - 1)
    vec = pl.BlockSpec((1, D_CONV), lambda i: (0, 0))
    return pl.pallas_call(
        _conv_sample_kernel,
        grid=(b,),
        in_specs=[
            pl.BlockSpec((None, tp, D_CONV), lambda i: (i, 0, 0)),
            pl.BlockSpec((CONV_W, D_CONV), lambda i: (0, 0)),
            vec, vec, vec,
        ],
        out_specs=pl.BlockSpec((None, t, D_CONV), lambda i: (i, 0, 0)),
        out_shape=jax.ShapeDtypeStruct((b, t, D_CONV), BF16),
        compiler_params=_params(("parallel",), 32),
        name="conv_sample",
    )(apad, w, cb, lg, lb)


def _dil_prompt_kernel(*refs):
    qkv_refs = refs[:3 * N_DGROUPS]
    b_ref, qbuf, kbuf, vbuf, o_sc, l_sc, stage = refs[3 * N_DGROUPS:]
    t = b_ref.shape[0]
    n = BAND
    qi = lax.broadcasted_iota(jnp.int32, (n, 2 * n), 0)
    kj = lax.broadcasted_iota(jnp.int32, (n, 2 * n), 1)
    band = (kj > qi) & (kj <= qi + n)

    for gi, (win, dil) in enumerate(DIL_GROUPS):
        assert win // dil == n
        q_ref, k_ref, v_ref = qkv_refs[3 * gi:3 * gi + 3]
        s_len = t // dil
        n_blk = s_len // n
        pitch = s_len + n

        def place(r, q_rows, k_rows, v_rows, s_len=s_len, pitch=pitch):
            k0 = pl.multiple_of(r * pitch, n)
            qbuf[pl.ds(pl.multiple_of(r * s_len, n), s_len), :] = q_rows.astype(BF16)
            kbuf[pl.ds(k0, n), :] = jnp.zeros((n, HEAD_DIM), BF16)
            vbuf[pl.ds(k0, n), :] = jnp.zeros((n, HEAD_DIM), BF16)
            kbuf[pl.ds(k0 + n, s_len), :] = k_rows.astype(BF16)
            vbuf[pl.ds(k0 + n, s_len), :] = v_rows.astype(BF16)

        if dil > GATHER_STRIDE and dil % GATHER_STRIDE == 0:
            outer = dil // GATHER_STRIDE
            mid = t // GATHER_STRIDE

            def gather(r_in, carry, refs=(q_ref, k_ref, v_ref), s_len=s_len, outer=outer, mid=mid, place=place):
                for w, ref in enumerate(refs):
                    stage[w, 0:mid, :] = ref[pl.ds(r_in, mid, stride=GATHER_STRIDE), :]
                for m in range(outer):
                    place(r_in + m * GATHER_STRIDE,
                          *[stage[w, pl.ds(m, s_len, stride=outer), :] for w in range(3)])
                return carry

            lax.fori_loop(0, GATHER_STRIDE, gather, 0)
        else:
            def gather(r, carry, refs=(q_ref, k_ref, v_ref), dil=dil, s_len=s_len, place=place):
                rows = pl.ds(r, s_len, stride=dil) if dil > 1 else pl.ds(0, s_len)
                place(r, *[ref[rows, :] for ref in refs])
                return carry

            lax.fori_loop(0, dil, gather, 0)

        def units(it, carry, dil=dil, n_blk=n_blk, pitch=pitch, gi=gi):
            for j in range(ATTN_UNROLL):
                u = it * ATTN_UNROLL + j
                if n_blk == 1:
                    r, blk = u, 0
                elif dil == 1:
                    r, blk = 0, u
                else:
                    r, blk = lax.div(u, jnp.int32(n_blk)), lax.rem(u, jnp.int32(n_blk))
                q = qbuf[pl.ds(pl.multiple_of(u * n, n), n), :]
                k0 = pl.multiple_of(r * pitch + blk * n, n)
                mask = band & (kj >= jnp.where(blk == 0, n, 0))
                s = jnp.where(mask, _dot_nt(q, kbuf[pl.ds(k0, 2 * n), :]), NEG)
                o, lse = _softmax_pv(s, vbuf[pl.ds(k0, 2 * n), :])
                start = r + blk * (n * dil)
                dst = pl.ds(start, n, stride=dil) if dil > 1 else pl.ds(pl.multiple_of(start, n), n)
                o_sc[gi, dst, :] = o
                l_sc[gi, dst, :] = jnp.broadcast_to(lse, (n, HEAD_DIM))
            return carry

        assert (dil * n_blk) % ATTN_UNROLL == 0
        lax.fori_loop(0, dil * n_blk // ATTN_UNROLL, units, 0)

    def merge(c, carry):
        rows = pl.ds(pl.multiple_of(c * MERGE_ROWS, MERGE_ROWS), MERGE_ROWS)
        outs = [o_sc[gi, rows, :] for gi in range(N_DGROUPS)]
        lses = [l_sc[gi, rows, :] for gi in range(N_DGROUPS)]
        b_ref[rows, :] = _merge3(outs, lses).astype(b_ref.dtype)
        return carry

    lax.fori_loop(0, t // MERGE_ROWS, merge, 0)


def dilated_attn_prompt(qkv, batch, seq):
    assert seq % (BAND * max(d for _, d in DIL_GROUPS)) == 0 and seq % MERGE_ROWS == 0
    kv_rows = max(seq + dil * BAND for _, dil in DIL_GROUPS)

    def spec(which, gi):
        return pl.BlockSpec((seq, HEAD_DIM), lambda b, h: (b, (which * N_DGROUPS + gi) * HEADS + h))

    in_specs = [spec(which, gi) for gi in range(N_DGROUPS) for which in range(3)]
    return pl.pallas_call(
        _dil_prompt_kernel,
        grid=(batch, HEADS),
        in_specs=in_specs,
        out_specs=pl.BlockSpec((seq, HEAD_DIM), lambda b, h: (b, h)),
        out_shape=jax.ShapeDtypeStruct((batch * seq, D_GRP), BF16),
        scratch_shapes=[pltpu.VMEM((seq, HEAD_DIM), BF16), pltpu.VMEM((kv_rows, HEAD_DIM), BF16),
                        pltpu.VMEM((kv_rows, HEAD_DIM), BF16),
                        pltpu.VMEM((N_DGROUPS, seq, HEAD_DIM), F32), pltpu.VMEM((N_DGROUPS, seq, HEAD_DIM), F32),
                        pltpu.VMEM((3, seq // GATHER_STRIDE, HEAD_DIM), F32)],
        compiler_params=_params(("parallel", "parallel"), 40),
        name="dil_attn_prompt",
    )(*([qkv] * (3 * N_DGROUPS)))


def _dil_sample_kernel(q_ref, kn_ref, vn_ref, *rest):
    cache_refs, b_ref = rest[:2 * N_DGROUPS], rest[2 * N_DGROUPS]
    ds = q_ref.shape[0]
    qn = lax.broadcasted_iota(jnp.int32, (ds, ds), 0)
    pn = lax.broadcasted_iota(jnp.int32, (ds, ds), 1)
    for h in range(HEADS):
        outs, lses = [], []
        for gi, (win, dil) in enumerate(DIL_GROUPS):
            n = win // dil
            kc_ref, vc_ref = cache_refs[2 * gi], cache_refs[2 * gi + 1]
            cols = slice(gi * D_GRP + h * HEAD_DIM, gi * D_GRP + (h + 1) * HEAD_DIM)
            if len(kc_ref.shape) == 3:
                n_sub, kept = kc_ref.shape[0], kc_ref.shape[1] // HEADS
                cache_len, n_keys = n_sub * dil, n_sub * kept
                head = lambda ref: ref[:, pl.ds(h, kept, stride=HEADS), :].reshape(n_keys, HEAD_DIM).astype(BF16)
                fj = lax.broadcasted_iota(jnp.int32, (ds, n_keys), 1)
                pj = fj // kept * dil + fj % kept
            else:
                cache_len = n_keys = kc_ref.shape[0] // HEADS
                head = lambda ref: _head_rows(ref, h, cache_len).astype(BF16)
                pj = lax.broadcasted_iota(jnp.int32, (ds, n_keys), 1)
            qi = lax.broadcasted_iota(jnp.int32, (ds, n_keys), 0)
            dist = cache_len + qi - pj
            mask_c = (dist % dil == 0) & (dist <= dil * (n - 1))
            mask_n = (qn >= pn) & ((qn - pn) % dil == 0)
            q = q_ref[:, cols].astype(BF16)
            s_c = jnp.where(mask_c, _dot_nt(q, head(kc_ref)), NEG)
            s_n = jnp.where(mask_n, _dot_nt(q, kn_ref[:, cols].astype(BF16)), NEG)
            mx = jnp.maximum(jnp.max(s_c, axis=-1, keepdims=True), jnp.max(s_n, axis=-1, keepdims=True))
            p_c = jnp.exp(s_c - mx)
            p_n = jnp.exp(s_n - mx)
            l = jnp.sum(p_c, axis=-1, keepdims=True) + jnp.sum(p_n, axis=-1, keepdims=True)
            o = _dot(p_c.astype(BF16), head(vc_ref)) + _dot(p_n.astype(BF16), vn_ref[:, cols].astype(BF16))
            outs.append(o / l)
            lses.append(mx + jnp.log(l))
        b_ref[:, h * HEAD_DIM:(h + 1) * HEAD_DIM] = _merge3(outs, lses).astype(b_ref.dtype)


def dilated_attn_sample(qkv, caches, layer):
    b, ds, _ = qkv.shape
    caches = list(caches)
    cache_specs = []
    for gi, (win, dil) in enumerate(DIL_GROUPS):
        layers, _, rows, _ = caches[2 * gi].shape
        cache_len = rows // HEADS
        assert cache_len - dil * (win // dil - 1) >= 0
        kept = dil // 2
        if cache_len % dil == 0 and ds <= kept and kept * HEADS % SUBLANES == 0:
            for c in (2 * gi, 2 * gi + 1):
                caches[c] = caches[c].reshape(layers, b, cache_len // dil, dil * HEADS, HEAD_DIM)
            cache_specs += [pl.BlockSpec((None, None, cache_len // dil, kept * HEADS, HEAD_DIM),
                                         lambda i: (layer, i, 0, 0, 0))] * 2
        else:
            cache_specs += [pl.BlockSpec((None, None, rows, HEAD_DIM), lambda i: (layer, i, 0, 0))] * 2

    new = lambda which: pl.BlockSpec((None, ds, D_ATT), lambda i: (i, 0, which))
    return pl.pallas_call(
        _dil_sample_kernel,
        grid=(b,),
        in_specs=[new(0), new(1), new(2)] + cache_specs,
        out_specs=pl.BlockSpec((None, ds, D_GRP), lambda i: (i, 0, 0)),
        out_shape=jax.ShapeDtypeStruct((b, ds, D_GRP), BF16),
        compiler_params=_params(("parallel",), 48),
        name="dil_attn_sample",
    )(qkv, qkv, qkv, *caches)


def _mix_out_kernel(x_ref, a_ref, b_ref, wa_ref, wb_ref, y_ref):
    y_ref[...] = x_ref[...] + _dot(a_ref[...], wa_ref[...]) + _dot(b_ref[...], wb_ref[...])


def mix_out(x, a, b, w_out, layer, tm):
    m = x.shape[0]
    row = lambda width: pl.BlockSpec((tm, width), lambda i: (i, 0))
    return pl.pallas_call(
        _mix_out_kernel,
        grid=(m // tm,),
        in_specs=[row(D_MODEL), row(D_CONV), row(D_GRP),
                  pl.BlockSpec((None, D_CONV, D_MODEL), lambda i: (layer, 0, 0)),
                  pl.BlockSpec((None, D_GRP, D_MODEL), lambda i: (layer, D_CONV // D_GRP, 0))],
        out_specs=row(D_MODEL),
        out_shape=jax.ShapeDtypeStruct((m, D_MODEL), F32),
        compiler_params=_params(("parallel",), 48),
        name="mix_out",
    )(x, a, b, w_out, w_out)


def _mix_mem_kernel(x_ref, c_ref, lg_ref, lb_ref, b_ref, wa_ref, wb_ref,
                    g_ref, wq_ref, qg_ref, k_ref, v_ref, wo_ref, y_ref):
    a = _ln_silu(c_ref[...], lg_ref, lb_ref).astype(BF16)
    x1 = x_ref[...] + _dot(a, wa_ref[...]) + _dot(b_ref[...], wb_ref[...])
    y_ref[...] = _mem_attn_tail(x1, 1, g_ref, wq_ref, qg_ref, _kv_head_of(k_ref, v_ref), wo_ref)


def mix_out_mem(x, c, lg, lb, b, w_out, layer, mem_args, batch, tm):
    m = x.shape[0]
    nt = m // batch // tm
    row = lambda width: pl.BlockSpec((tm, width), lambda i, j: (i * nt + j, 0))
    vec = _resident((1, D_CONV), (0, 0))
    mem_specs, mem_ops = _mem_operands(*mem_args)
    return pl.pallas_call(
        _mix_mem_kernel,
        grid=(batch, nt),
        in_specs=[row(D_MODEL), row(D_CONV), vec, vec, row(D_GRP),
                  _resident((None, D_CONV, D_MODEL), (layer, 0, 0)),
                  _resident((None, D_GRP, D_MODEL), (layer, D_CONV // D_GRP, 0))] + mem_specs,
        out_specs=row(D_MODEL),
        out_shape=jax.ShapeDtypeStruct((m, D_MODEL), F32),
        compiler_params=_params(("parallel", "parallel"), 56),
        name="mix_out_mem",
    )(x, c, lg, lb, b, w_out, w_out, *mem_ops)


def _sgu_gated(u_ref, gv_ref, lg_ref, lb_ref, ws_ref, bs_ref):
    gv = gv_ref[...]
    mu = jnp.mean(gv, axis=-1, keepdims=True)
    vc = gv - mu
    v = vc * lax.rsqrt(jnp.mean(vc * vc, axis=-1, keepdims=True) + EPS) * lg_ref[...] + lb_ref[...]
    vb = v.astype(BF16)
    rows, cm = gv.shape[0], ws_ref.shape[1]
    gated = []
    for g in range(N_SG):
        cols = slice(g * D_SG, (g + 1) * D_SG)
        bias = bs_ref[:, g:g + 1]
        if cm == SUBLANES:
            v3 = vb[:, cols].astype(F32).reshape(rows // cm, cm, D_SG)
            wsg = ws_ref[g].astype(F32)
            sv = bias[None] + sum(v3[:, s:s + 1, :] * wsg[:, s:s + 1][None] for s in range(cm))
            sv = sv.reshape(rows, D_SG)
        else:
            sv = jnp.concatenate([_dot(ws_ref[g], vb[c * cm:(c + 1) * cm, cols]) + bias
                                  for c in range(rows // cm)], axis=0)
        gated.append((u_ref[:, cols] * sv).astype(BF16))
    return jnp.concatenate(gated, axis=-1), v


def _sgu_kernel(x_ref, u_ref, gv_ref, lg_ref, lb_ref, ws_ref, bs_ref, w_ref, *out_refs, emit_v):
    gated, v = _sgu_gated(u_ref, gv_ref, lg_ref, lb_ref, ws_ref, bs_ref)
    if emit_v:
        out_refs[1][...] = v
    out_refs[0][...] = x_ref[...] + _dot(gated, w_ref[...])


def _sgu_mem_kernel(x_ref, u_ref, gv_ref, lg_ref, lb_ref, ws_ref, bs_ref, w_ref,
                    g_ref, wq_ref, qg_ref, k_ref, v_ref, wo_ref, y_ref):
    gated, _ = _sgu_gated(u_ref, gv_ref, lg_ref, lb_ref, ws_ref, bs_ref)
    x1 = x_ref[...] + _dot(gated, w_ref[...])
    y_ref[...] = _mem_attn_tail(x1, 1, g_ref, wq_ref, qg_ref, _kv_head_of(k_ref, v_ref), wo_ref)


def sgu_out_mem(x, z, lg, lb, ws, bs, w_out, layer, mem_args, batch, tm):
    m = x.shape[0]
    cm = ws.shape[1]
    nt = m // batch // tm
    assert tm % cm == 0
    row = lambda jblk: pl.BlockSpec((tm, D_GATE), lambda i, j: (i * nt + j, jblk))
    vec = pl.BlockSpec((1, D_GATE), lambda i, j: (0, 0))
    mem_specs, mem_ops = _mem_operands(*mem_args)
    return pl.pallas_call(
        _sgu_mem_kernel,
        grid=(batch, nt),
        in_specs=[row(0), row(0), row(1), vec, vec,
                  _resident((N_SG, cm, cm), (0, 0, 0)), _resident((cm, N_SG), (0, 0)),
                  _resident((None, D_GATE, D_MODEL), (layer, 0, 0))] + mem_specs,
        out_specs=row(0),
        out_shape=jax.ShapeDtypeStruct((m, D_MODEL), F32),
        compiler_params=_params(("parallel", "parallel"), 56),
        name="sgu_out_mem",
    )(x, z, z, lg, lb, ws, bs, w_out, *mem_ops)


def sgu_out(x, z, lg, lb, ws, bs, w_out, layer, tm, emit_v):
    m = x.shape[0]
    cm = ws.shape[1]
    assert tm % cm == 0
    row = lambda jblk: pl.BlockSpec((tm, D_GATE), lambda i: (i, jblk))
    vec = pl.BlockSpec((1, D_GATE), lambda i: (0, 0))
    out_specs = [row(0)]
    out_shape = [jax.ShapeDtypeStruct((m, D_MODEL), F32)]
    if emit_v:
        out_specs.append(row(0))
        out_shape.append(jax.ShapeDtypeStruct((m, D_GATE), F32))
    res = pl.pallas_call(
        functools.partial(_sgu_kernel, emit_v=emit_v),
        grid=(m // tm,),
        in_specs=[row(0), row(0), row(1), vec, vec,
                  pl.BlockSpec((N_SG, cm, cm), lambda i: (0, 0, 0)),
                  pl.BlockSpec((cm, N_SG), lambda i: (0, 0)),
                  pl.BlockSpec((None, D_GATE, D_MODEL), lambda i: (layer, 0, 0))],
        out_specs=out_specs,
        out_shape=out_shape,
        compiler_params=_params(("parallel",), 52),
        name="sgu_out",
    )(x, z, z, lg, lb, ws, bs, w_out)
    return res if emit_v else (res[0], None)


def _mem_attn_tail(x, n_seq, g_ref, wq_ref, qg_ref, kv_head, wo_ref):
    tm = x.shape[0] // n_seq
    h = (_rms(x) * g_ref[...]).astype(BF16)
    q = _head_norm(_dot(h, wq_ref[...]), qg_ref[...]).astype(BF16)
    per_seq = []
    for b in range(n_seq):
        outs = []
        for hd in range(HEADS):
            k, v = kv_head(b, hd)
            o, _ = _softmax_pv(_dot_nt(q[b * tm:(b + 1) * tm, hd * HEAD_DIM:(hd + 1) * HEAD_DIM], k), v)
            outs.append(o.astype(BF16))
        per_seq.append(jnp.concatenate(outs, axis=-1))
    o = jnp.concatenate(per_seq, axis=0) if n_seq > 1 else per_seq[0]
    return x + _dot(o, wo_ref[...])


def _kv_head_of(k_ref, v_ref):
    n_mem = k_ref.shape[0] // HEADS
    return lambda b, hd: (_head_rows(k_ref, hd, n_mem).astype(BF16), _head_rows(v_ref, hd, n_mem).astype(BF16))


def _resident(shape, index):
    return pl.BlockSpec(shape, lambda *_: index, pipeline_mode=pl.Buffered(1))


def _mem_operands(g, wq, q_gain, k, v, kv_layer, wo, layer):
    specs = [_resident((1, D_MODEL), (0, 0)),
             _resident((None, D_MODEL, D_MEMATT), (layer, 0, 0)),
             _resident((1, D_MEMATT), (0, 0)),
             pl.BlockSpec((None, None, k.shape[2], HEAD_DIM), lambda i, j: (kv_layer, i, 0, 0)),
             pl.BlockSpec((None, None, k.shape[2], HEAD_DIM), lambda i, j: (kv_layer, i, 0, 0)),
             _resident((None, D_MEMATT, D_MODEL), (layer, 0, 0))]
    return specs, (g, wq, q_gain, k, v, wo)


def _mem_attn_kernel(x_ref, g_ref, wq_ref, qg_ref, k_ref, v_ref, wo_ref, y_ref):
    bb, tm, _ = x_ref.shape
    n_mem = k_ref.shape[1] // HEADS

    def kv_head(b, hd):
        rows = pl.ds(hd, n_mem, stride=HEADS)
        return k_ref[b, rows, :].astype(BF16), v_ref[b, rows, :].astype(BF16)

    x = x_ref[...].reshape(bb * tm, D_MODEL)
    y_ref[...] = _mem_attn_tail(x, bb, g_ref, wq_ref, qg_ref, kv_head, wo_ref).reshape(bb, tm, D_MODEL)


def mem_attn(x, g, wq, q_gain, k, v, kv_layer, wo, layer, bb, tm):
    b, t, _ = x.shape
    full = lambda shape: pl.BlockSpec(shape, lambda i, j: (0,) * len(shape))
    kv = pl.BlockSpec((None, bb, k.shape[2], HEAD_DIM), lambda i, j: (kv_layer, i, 0, 0))
    xs = pl.BlockSpec((bb, tm, D_MODEL), lambda i, j: (i, j, 0))
    return pl.pallas_call(
        _mem_attn_kernel,
        grid=(b // bb, t // tm),
        in_specs=[xs, full((1, D_MODEL)),
                  pl.BlockSpec((None, D_MODEL, D_MEMATT), lambda i, j: (layer, 0, 0)),
                  full((1, D_MEMATT)), kv, kv,
                  pl.BlockSpec((None, D_MEMATT, D_MODEL), lambda i, j: (layer, 0, 0))],
        out_specs=xs,
        out_shape=jax.ShapeDtypeStruct((b, t, D_MODEL), F32),
        compiler_params=_params(("parallel", "parallel"), 40),
        name="mem_attn",
    )(x, g, wq, q_gain, k, v, wo)


def _tile_heads(g):
    return jnp.tile(g, HEADS)


def _row_tile(m):
    return min(m, 1024)


def kernel(x_prompt, x_sample, mem_prompt, state_conv, cache_k_w128, cache_v_w128, cache_k_w512, cache_v_w512,
           cache_k_w2048, cache_v_w2048, cache_mem_k, cache_mem_v, g_mix, w_in_e, conv_w, conv_b, conv_ln_g,
           conv_ln_b, q_norm_e, k_norm_e, w_out_e, w_in_o, b_in_o, v_ln_g, v_ln_b, w_s, b_s, w_out_o, g_xmem,
           g_mem, wq_mem, wk_mem, wv_mem, q_norm_mem, k_norm_mem, wo_mem, g_ffn, w_ffn1, w_ffn2):
    depth = g_mix.shape[0]
    bp, tp, _ = x_prompt.shape
    bs, ts, _ = x_sample.shape
    mp, ms = bp * tp, bs * ts
    scale = HEAD_DIM ** -0.5
    row = lambda v: v.reshape(1, -1)
    th_rows = lambda c: c.reshape(c.shape[0], c.shape[1], c.shape[2] * HEADS, HEAD_DIM)
    caches = [th_rows(c) for c in (cache_k_w128, cache_v_w128, cache_k_w512, cache_v_w512,
                                   cache_k_w2048, cache_v_w2048)]
    mem_k_s, mem_v_s = th_rows(cache_mem_k), th_rows(cache_mem_v)

    w_in_e, w_out_e, w_in_o, w_out_o = (w.astype(BF16) for w in (w_in_e, w_out_e, w_in_o, w_out_o))
    qkv_group0 = 2 * D_CONV // D_GRP
    wq_mem, wo_mem = wq_mem.astype(BF16), wo_mem.astype(BF16)
    wkv_mem = jnp.concatenate([wk_mem, wv_mem], axis=2).astype(BF16)

    xp = x_prompt.reshape(mp, D_MODEL)
    xs = x_sample.reshape(ms, D_MODEL)
    mem = mem_prompt.reshape(bp * N_MEM, D_MODEL)
    tm_p, tm_s, tm_mem = _row_tile(mp), _row_tile(ms), _row_tile(bp * N_MEM)
    assert tp % tm_p == 0 or tm_p % tp == 0
    assert ms == tm_s

    conv_pl, conv_sl, kv_pl, kv_sl, memk_pl, memv_pl, chunk_sl = [], [], [], [], [], [], []
    for i in range(depth):
        j = i // 2
        kgain = jnp.stack([_tile_heads(k_norm_mem[i]), jnp.ones((D_MEMATT,), F32)])
        mk, mv = headnorm_proj(mem, row(g_mem[i]), wkv_mem, i, 0, kgain, 1, tm_mem, D_MEMATT,
                               [(0, tm_mem, 1), (1, tm_mem, 1)], False)
        memk_pl.append(mk.reshape(bp, N_MEM, HEADS, HEAD_DIM))
        memv_pl.append(mv.reshape(bp, N_MEM, HEADS, HEAD_DIM))
        qgain = row(_tile_heads(q_norm_mem[i]) * scale)
        mem_p = (row(g_xmem[i]), wq_mem, qgain, mk.reshape(1, bp, N_MEM * HEADS, HEAD_DIM),
                 mv.reshape(1, bp, N_MEM * HEADS, HEAD_DIM), 0, wo_mem, i)
        mem_s = (row(g_xmem[i]), wq_mem, qgain, mem_k_s, mem_v_s, i, wo_mem, i)

        if i % 2 == 0:
            gains = jnp.concatenate([
                jnp.stack([_tile_heads(q_norm_e[j, gi]) * scale for gi in range(N_DGROUPS)]),
                jnp.stack([_tile_heads(k_norm_e[j, gi]) for gi in range(N_DGROUPS)]),
                jnp.ones((N_DGROUPS, D_GRP), F32)])
            conv_args = (conv_w[j], row(conv_b[j]), row(conv_ln_g[j]), row(conv_ln_b[j]))

            def kv_tiles(seq, tm):
                tiles = []
                for gi, (win, _) in enumerate(DIL_GROUPS):
                    keep = min(win, seq)
                    assert keep == seq or (keep <= tm and seq % tm == 0)
                    spec = (tm, 1) if keep == seq else (keep, seq // tm)
                    tiles += [((1 + which) * N_DGROUPS + gi,) + spec for which in range(2)]
                return tiles

            a = glu_proj(xp, row(g_mix[i]), w_in_e, j, tm_p)
            qkv, *new_kv = headnorm_proj(xp, row(g_mix[i]), w_in_e, j, qkv_group0, gains, 2 * N_DGROUPS, tm_p, D_ATT,
                                         kv_tiles(tp, tm_p), True)
            kv_pl.append([kv.reshape(bp, -1, HEADS, HEAD_DIM) for kv in new_kv])
            a3 = a.reshape(bp, tp, D_CONV)
            conv_pl.append(a3[:, tp - (CONV_W - 1):])
            c_out = conv_prompt(a3, *conv_args[:2], tt=512).reshape(mp, D_CONV)
            b_out = dilated_attn_prompt(qkv, bp, tp)
            xp = mix_out_mem(xp, c_out, *conv_args[2:], b_out, w_out_e, j, mem_p, bp, 512)

            a = glu_proj(xs, row(g_mix[i]), w_in_e, j, tm_s)
            qkv, *new_kv = headnorm_proj(xs, row(g_mix[i]), w_in_e, j, qkv_group0, gains, 2 * N_DGROUPS, tm_s, D_ATT,
                                         kv_tiles(ts, tm_s), True)
            kv_sl.append([kv.reshape(bs, -1, HEADS, HEAD_DIM) for kv in new_kv])
            apad = jnp.concatenate([state_conv[j], a.reshape(bs, ts, D_CONV)], axis=1)
            conv_sl.append(apad[:, apad.shape[1] - (CONV_W - 1):])
            a_out = conv_sample(apad, *conv_args).reshape(ms, D_CONV)
            b_out = dilated_attn_sample(qkv.reshape(bs, ts, 3 * D_ATT), caches, j).reshape(ms, D_GRP)
            xs = mix_out(xs, a_out, b_out, w_out_e, j, tm_s)
        else:
            sgu_vecs = (row(v_ln_g[j]), row(v_ln_b[j]))
            tril = jnp.tril(jnp.ones((CHUNK, CHUNK), F32))

            def spatial(t):
                c = min(CHUNK, t)
                assert c in (SUBLANES, CHUNK) and t % c == 0
                return (w_s[j][:, :c, :c] * tril[:c, :c]).astype(BF16), b_s[j][:, :c].T

            z = gelu_proj(xp, row(g_mix[i]), w_in_o, j, row(b_in_o[j]), tm_p)
            xp = sgu_out_mem(xp, z, *sgu_vecs, *spatial(tp), w_out_o, j, mem_p, bp, 512)

            z = gelu_proj(xs, row(g_mix[i]), w_in_o, j, row(b_in_o[j]), tm_s)
            xs, v = sgu_out(xs, z, *sgu_vecs, *spatial(ts), w_out_o, j, 256, emit_v=True)
            chunk_sl.append(v.reshape(bs, ts, D_GATE))

        xs = mem_attn(xs.reshape(bs, ts, D_MODEL), *mem_s, 8, ts).reshape(ms, D_MODEL)
        xs, w1, w2 = ffn_cast(xs, row(g_ffn[i]), w_ffn1, w_ffn2, i, 512)
        xp = ffn(xp, row(g_ffn[i]), w1, w2, 0, tm_p, 1024)

    stack = lambda items: jnp.stack(items)
    kv_p = [stack([kv[n] for kv in kv_pl]) for n in range(2 * N_DGROUPS)]
    kv_s = [stack([kv[n] for kv in kv_sl]) for n in range(2 * N_DGROUPS)]
    return (xp.reshape(bp, tp, D_MODEL), xs.reshape(bs, ts, D_MODEL), stack(conv_pl), stack(conv_sl),
            *kv_p, *kv_s, stack(memk_pl), stack(memv_pl), stack(chunk_sl))
```

```python
import functools
import math

import numpy as np
import jax
import jax.numpy as jnp
from jax import lax
from jax.experimental import pallas as pl
from jax.experimental.pallas import tpu as pltpu

D_MODEL = 2048
EPS = 1e-6
NEG = -1e30
D_CONV = D_MODEL // 2
CONV_W = 31
HEAD_DIM = 128
DIL_GROUPS = ((128, 1), (512, 4), (2048, 16))
N_DGROUPS = len(DIL_GROUPS)
HEADS = 4
D_GRP = HEADS * HEAD_DIM
D_ATT = N_DGROUPS * D_GRP
CHUNK = 128
D_GATE = D_MODEL
N_SG = 8
D_SG = D_GATE // N_SG
N_MEM = 256
D_MEMATT = HEADS * HEAD_DIM
D_FF = 4 * D_MODEL

F32 = jnp.float32
BF16 = jnp.bfloat16
MIB = 1024 * 1024
NORM_ROWS = 256
CONV_HALO = 32
SUBLANES = 8
MERGE_ROWS = 256
BAND = DIL_GROUPS[0][0] // DIL_GROUPS[0][1]
GATHER_STRIDE = 4
ATTN_UNROLL = 16
CONV_ROWS = 16
LANES = 128


def _params(sem, vmem_mib):
    return pltpu.CompilerParams(dimension_semantics=sem, vmem_limit_bytes=vmem_mib * MIB)


def _dot(a, b):
    return jnp.dot(a, b, preferred_element_type=F32)


def _dot_nt(a, b):
    return lax.dot_general(a, b, (((1,), (1,)), ((), ())), preferred_element_type=F32)


def _rms(x):
    return x * lax.rsqrt(jnp.mean(x * x, axis=-1, keepdims=True) + EPS)


def _norm_to_scratch(x_ref, g_ref, h_ref):
    rows = x_ref.shape[0]
    step = min(NORM_ROWS, rows)

    def body(c, carry):
        r = pl.multiple_of(c * step, step)
        x = x_ref[pl.ds(r, step), :]
        h_ref[pl.ds(r, step), :] = (_rms(x) * g_ref[...]).astype(BF16)
        return carry

    lax.fori_loop(0, rows // step, body, 0)


def _head_norm(acc, gain):
    parts = [_rms(acc[:, h * HEAD_DIM:(h + 1) * HEAD_DIM]) for h in range(acc.shape[1] // HEAD_DIM)]
    return jnp.concatenate(parts, axis=-1) * gain


def _head_rows(ref, h, rows):
    return ref[pl.ds(h, rows, stride=HEADS), :]


def _softmax_pv(s, v):
    mx = jnp.max(s, axis=-1, keepdims=True)
    p = jnp.exp(s - mx)
    l = jnp.sum(p, axis=-1, keepdims=True)
    return _dot(p.astype(BF16), v) / l, mx + jnp.log(l)


def _merge3(outs, lses):
    mx = jnp.maximum(jnp.maximum(lses[0], lses[1]), lses[2])
    es = [jnp.exp(l - mx) for l in lses]
    return (es[0] * outs[0] + es[1] * outs[1] + es[2] * outs[2]) / (es[0] + es[1] + es[2])


def _glu_kernel(x_ref, g_ref, wv_ref, wg_ref, o_ref, h_ref):
    @pl.when(pl.program_id(1) == 0)
    def _():
        _norm_to_scratch(x_ref, g_ref, h_ref)

    h = h_ref[...]
    val = _dot(h, wv_ref[...])
    gate = _dot(h, wg_ref[...])
    o_ref[...] = val * jax.nn.sigmoid(gate)


def glu_proj(x, g, w_in, layer, tm):
    m = x.shape[0]
    tn = D_CONV
    nj = D_CONV // tn
    return pl.pallas_call(
        _glu_kernel,
        grid=(m // tm, nj),
        in_specs=[
            pl.BlockSpec((tm, D_MODEL), lambda i, j: (i, 0)),
            pl.BlockSpec((1, D_MODEL), lambda i, j: (0, 0)),
            pl.BlockSpec((None, D_MODEL, tn), lambda i, j: (layer, 0, j)),
            pl.BlockSpec((None, D_MODEL, tn), lambda i, j: (layer, 0, j + nj)),
        ],
        out_specs=pl.BlockSpec((tm, tn), lambda i, j: (i, j)),
        out_shape=jax.ShapeDtypeStruct((m, D_CONV), F32),
        scratch_shapes=[pltpu.VMEM((tm, D_MODEL), BF16)],
        compiler_params=_params(("parallel", "arbitrary"), 48),
        name="glu_proj",
    )(x, g, w_in, w_in)


def _headnorm_proj_kernel(x_ref, g_ref, *rest, per_tile, n_normed, emit_main, th_tiles):
    w_refs, gain_ref, rest = rest[:per_tile], rest[per_tile], rest[per_tile + 1:]
    n_th = len(th_tiles)
    if emit_main:
        o_ref, th_refs, h_ref = rest[0], rest[1:1 + n_th], rest[1 + n_th]
    else:
        th_refs, h_ref, o_ref = rest[:n_th], rest[n_th], rest[n_th + 1]
    i, j = pl.program_id(0), pl.program_id(1)
    tm = x_ref.shape[0]

    @pl.when(j == 0)
    def _():
        _norm_to_scratch(x_ref, g_ref, h_ref)

    h = h_ref[...]
    for s in range(per_tile):
        cols = slice(s * D_GRP, (s + 1) * D_GRP)
        acc = _dot(h, w_refs[s][...])
        o_ref[:, cols] = jnp.where(j * per_tile + s < n_normed, _head_norm(acc, gain_ref[0, :, cols]), acc)

    for th_ref, (col, period) in zip(th_refs, th_tiles):
        rows = th_ref.shape[0] // HEADS
        c0 = col % per_tile * D_GRP

        @pl.when((j == col // per_tile) & (i % period == period - 1))
        def _():
            for h in range(HEADS):
                th_ref[pl.ds(h, rows, stride=HEADS), :] = o_ref[tm - rows:, c0 + h * HEAD_DIM:c0 + (h + 1) * HEAD_DIM]


def _th_index(period, i, j):
    return (i // period, 0)


def _w_group_index(layer, first, per_tile, s, i, j):
    return (layer, 0, first + j * per_tile + s)


def headnorm_proj(x, g, w, layer, group0, gains, n_normed, tm, tn, th_tiles, emit_main):
    m = x.shape[0]
    n = gains.shape[0] * D_GRP
    nj = n // tn
    per_tile = tn // D_GRP
    assert n % tn == 0 and tn % D_GRP == 0
    out_specs, out_shape = [], []
    if emit_main:
        out_specs.append(pl.BlockSpec((tm, tn), lambda i, j: (i, j)))
        out_shape.append(jax.ShapeDtypeStruct((m, n), F32))
    for _, keep, period in th_tiles:
        assert keep <= tm and (m // tm) % period == 0
        mode = {} if period == 1 else {"pipeline_mode": pl.Buffered(1)}
        out_specs.append(pl.BlockSpec((keep * HEADS, HEAD_DIM), functools.partial(_th_index, period), **mode))
        out_shape.append(jax.ShapeDtypeStruct((m // tm // period * keep * HEADS, HEAD_DIM), F32))
    scratch = [pltpu.VMEM((tm, D_MODEL), BF16)]
    if not emit_main:
        scratch.append(pltpu.VMEM((tm, tn), F32))
    return pl.pallas_call(
        functools.partial(_headnorm_proj_kernel, per_tile=per_tile, n_normed=n_normed, emit_main=emit_main,
                          th_tiles=tuple((col, period) for col, _, period in th_tiles)),
        grid=(m // tm, nj),
        in_specs=[pl.BlockSpec((tm, D_MODEL), lambda i, j: (i, 0)),
                  pl.BlockSpec((1, D_MODEL), lambda i, j: (0, 0))]
        + [pl.BlockSpec((None, D_MODEL, D_GRP), functools.partial(_w_group_index, layer, group0, per_tile, s))
           for s in range(per_tile)]
        + [pl.BlockSpec((1, 1, tn), lambda i, j: (j, 0, 0))],
        out_specs=out_specs,
        out_shape=out_shape,
        scratch_shapes=scratch,
        compiler_params=_params(("arbitrary", "arbitrary"), 61),
        name="headnorm_proj",
    )(x, g, *([w] * per_tile), gains.reshape(nj, 1, tn))


def _gelu_proj_kernel(x_ref, g_ref, w_ref, b_ref, o_ref, h_ref):
    @pl.when(pl.program_id(1) == 0)
    def _():
        _norm_to_scratch(x_ref, g_ref, h_ref)

    z = _dot(h_ref[...], w_ref[...]) + b_ref[...]
    o_ref[...] = 0.5 * z * (1.0 + lax.erf(z * np.float32(math.sqrt(0.5))))


def gelu_proj(x, g, w, layer, b, tm):
    m = x.shape[0]
    n = w.shape[2]
    tn = 1024
    return pl.pallas_call(
        _gelu_proj_kernel,
        grid=(m // tm, n // tn),
        in_specs=[
            pl.BlockSpec((tm, D_MODEL), lambda i, j: (i, 0)),
            pl.BlockSpec((1, D_MODEL), lambda i, j: (0, 0)),
            pl.BlockSpec((None, D_MODEL, tn), lambda i, j: (layer, 0, j)),
            pl.BlockSpec((1, tn), lambda i, j: (0, j)),
        ],
        out_specs=pl.BlockSpec((tm, tn), lambda i, j: (i, j)),
        out_shape=jax.ShapeDtypeStruct((m, n), F32),
        scratch_shapes=[pltpu.VMEM((tm, D_MODEL), BF16)],
        compiler_params=_params(("parallel", "arbitrary"), 48),
        name="gelu_proj",
    )(x, g, w, b)


def _ffn_kernel(x_ref, g_ref, w1_ref, w2_ref, o_ref, h_ref):
    @pl.when(pl.program_id(1) == 0)
    def _():
        _norm_to_scratch(x_ref, g_ref, h_ref)
        o_ref[...] = x_ref[...]

    hid = jnp.maximum(_dot(h_ref[...], w1_ref[...]), 0.0)
    o_ref[...] += _dot((hid * hid).astype(BF16), w2_ref[...])


def ffn(x, g, w1, w2, layer, tm, tf):
    m = x.shape[0]
    return pl.pallas_call(
        _ffn_kernel,
        grid=(m // tm, D_FF // tf),
        in_specs=[
            pl.BlockSpec((tm, D_MODEL), lambda i, f: (i, 0)),
            pl.BlockSpec((1, D_MODEL), lambda i, f: (0, 0)),
            pl.BlockSpec((None, D_MODEL, tf), lambda i, f: (layer, 0, f)),
            pl.BlockSpec((None, tf, D_MODEL), lambda i, f: (layer, f, 0)),
        ],
        out_specs=pl.BlockSpec((tm, D_MODEL), lambda i, f: (i, 0)),
        out_shape=jax.ShapeDtypeStruct((m, D_MODEL), F32),
        scratch_shapes=[pltpu.VMEM((tm, D_MODEL), BF16)],
        compiler_params=_params(("parallel", "arbitrary"), 60),
        name="ffn",
    )(x, g, w1, w2)


def _ffn_cast_kernel(x_ref, g_ref, w1_ref, w2_ref, o_ref, w1b_ref, w2b_ref, h_ref):
    @pl.when(pl.program_id(0) == 0)
    def _():
        _norm_to_scratch(x_ref, g_ref, h_ref)
        o_ref[...] = x_ref[...]

    w1 = w1_ref[...].astype(BF16)
    w2 = w2_ref[...].astype(BF16)
    w1b_ref[...] = w1
    w2b_ref[...] = w2
    hid = jnp.maximum(_dot(h_ref[...], w1), 0.0)
    o_ref[...] += _dot((hid * hid).astype(BF16), w2)


def ffn_cast(x, g, w1, w2, layer, tf):
    m = x.shape[0]
    const = lambda shape: pl.BlockSpec(shape, lambda f: (0,) * len(shape))
    return pl.pallas_call(
        _ffn_cast_kernel,
        grid=(D_FF // tf,),
        in_specs=[const((m, D_MODEL)), const((1, D_MODEL)),
                  pl.BlockSpec((None, D_MODEL, tf), lambda f: (layer, 0, f)),
                  pl.BlockSpec((None, tf, D_MODEL), lambda f: (layer, f, 0))],
        out_specs=[const((m, D_MODEL)),
                   pl.BlockSpec((None, D_MODEL, tf), lambda f: (0, 0, f)),
                   pl.BlockSpec((None, tf, D_MODEL), lambda f: (0, f, 0))],
        out_shape=[jax.ShapeDtypeStruct((m, D_MODEL), F32),
                   jax.ShapeDtypeStruct((1, D_MODEL, D_FF), BF16),
                   jax.ShapeDtypeStruct((1, D_FF, D_MODEL), BF16)],
        scratch_shapes=[pltpu.VMEM((m, D_MODEL), BF16)],
        compiler_params=_params(("arbitrary",), 48),
        name="ffn_cast",
    )(x, g, w1, w2)


def _ln_silu(acc, lg_ref, lb_ref):
    mu = jnp.mean(acc, axis=-1, keepdims=True)
    xc = acc - mu
    y = xc * lax.rsqrt(jnp.mean(xc * xc, axis=-1, keepdims=True) + EPS)
    y = y * lg_ref[...] + lb_ref[...]
    return y * jax.nn.sigmoid(y)


def _conv_prompt_kernel(a_ref, halo_ref, w_ref, cb_ref, o_ref, sh_ref, wrep_ref):
    tt = a_ref.shape[0]
    first = pl.program_id(1) == 0
    for k in range(CONV_W):
        wrep_ref[k] = jnp.broadcast_to(w_ref[k:k + 1, :], (SUBLANES, D_CONV))
    sh_ref[0, 0:CONV_HALO, :] = jnp.where(first, 0.0, halo_ref[...])
    sh_ref[0, CONV_HALO:, :] = a_ref[...]
    span = tt + CONV_HALO - SUBLANES
    for s in range(1, SUBLANES):
        sh_ref[s, 0:span, :] = sh_ref[0, s:s + span, :]
    lead = CONV_HALO - (CONV_W - 1)
    groups = CONV_ROWS // SUBLANES

    def conv_rows(c, carry):
        r = pl.multiple_of(c * CONV_ROWS, CONV_ROWS)
        acc = jnp.zeros((groups, SUBLANES, D_CONV), F32) + cb_ref[...]
        for k in range(CONV_W):
            q, s = divmod(k + lead, SUBLANES)
            x = sh_ref[s, pl.ds(r + q * SUBLANES, CONV_ROWS), :]
            acc = acc + x.reshape(groups, SUBLANES, D_CONV) * wrep_ref[k]
        o_ref[pl.ds(r, CONV_ROWS), :] = acc.reshape(CONV_ROWS, D_CONV)
        return carry

    lax.fori_loop(0, tt // CONV_ROWS, conv_rows, 0)


def conv_prompt(a, w, cb, tt):
    b, t, _ = a.shape
    hb = tt // CONV_HALO
    vec = pl.BlockSpec((1, D_CONV), lambda i, j: (0, 0))
    return pl.pallas_call(
        _conv_prompt_kernel,
        grid=(b, t // tt),
        in_specs=[
            pl.BlockSpec((None, tt, D_CONV), lambda i, j: (i, j, 0)),
            pl.BlockSpec((None, CONV_HALO, D_CONV), lambda i, j: (i, jnp.maximum(j * hb - 1, 0), 0)),
            pl.BlockSpec((CONV_W, D_CONV), lambda i, j: (0, 0)),
            vec,
        ],
        out_specs=pl.BlockSpec((None, tt, D_CONV), lambda i, j: (i, j, 0)),
        out_shape=jax.ShapeDtypeStruct((b, t, D_CONV), F32),
        scratch_shapes=[pltpu.VMEM((SUBLANES, CONV_HALO + tt, D_CONV), F32),
                        pltpu.VMEM((CONV_W, SUBLANES, D_CONV), F32)],
        compiler_params=_params(("parallel", "arbitrary"), 40),
        name="conv_prompt",
    )(a, a, w, cb)


def _conv_sample_kernel(apad_ref, w_ref, cb_ref, lg_ref, lb_ref, o_ref):
    rows = o_ref.shape[0]
    acc = jnp.zeros((rows, D_CONV), F32) + cb_ref[...]
    for k in range(CONV_W):
        acc = acc + apad_ref[k:k + rows, :] * w_ref[k:k + 1, :]
    o_ref[...] = _ln_silu(acc, lg_ref, lb_ref).astype(o_ref.dtype)


def conv_sample(apad, w, cb, lg, lb):
    b, tp, _ = apad.shape
    t = tp - (CONV_W - 1)
    vec = pl.BlockSpec((1, D_CONV), lambda i: (0, 0))
    return pl.pallas_call(
        _conv_sample_kernel,
        grid=(b,),
        in_specs=[
            pl.BlockSpec((None, tp, D_CONV), lambda i: (i, 0, 0)),
            pl.BlockSpec((CONV_W, D_CONV), lambda i: (0, 0)),
            vec, vec, vec,
        ],
        out_specs=pl.BlockSpec((None, t, D_CONV), lambda i: (i, 0, 0)),
        out_shape=jax.ShapeDtypeStruct((b, t, D_CONV), BF16),
        compiler_params=_params(("parallel",), 32),
        name="conv_sample",
    )(apad, w, cb, lg, lb)


def _dil_prompt_kernel(*refs):
    qkv_refs = refs[:3 * N_DGROUPS]
    b_ref, qbuf, kbuf, vbuf, o_sc, l_sc, stage = refs[3 * N_DGROUPS:]
    t = b_ref.shape[0]
    n = BAND
    qi = lax.broadcasted_iota(jnp.int32, (n, 2 * n), 0)
    kj = lax.broadcasted_iota(jnp.int32, (n, 2 * n), 1)
    band = (kj > qi) & (kj <= qi + n)

    for gi, (win, dil) in enumerate(DIL_GROUPS):
        assert win // dil == n
        q_ref, k_ref, v_ref = qkv_refs[3 * gi:3 * gi + 3]
        s_len = t // dil
        n_blk = s_len // n
        pitch = s_len + n

        def place(r, q_rows, k_rows, v_rows, s_len=s_len, pitch=pitch):
            k0 = pl.multiple_of(r * pitch, n)
            qbuf[pl.ds(pl.multiple_of(r * s_len, n), s_len), :] = q_rows.astype(BF16)
            kbuf[pl.ds(k0, n), :] = jnp.zeros((n, HEAD_DIM), BF16)
            vbuf[pl.ds(k0, n), :] = jnp.zeros((n, HEAD_DIM), BF16)
            kbuf[pl.ds(k0 + n, s_len), :] = k_rows.astype(BF16)
            vbuf[pl.ds(k0 + n, s_len), :] = v_rows.astype(BF16)

        if dil > GATHER_STRIDE and dil % GATHER_STRIDE == 0:
            outer = dil // GATHER_STRIDE
            mid = t // GATHER_STRIDE

            def gather(r_in, carry, refs=(q_ref, k_ref, v_ref), s_len=s_len, outer=outer, mid=mid, place=place):
                for w, ref in enumerate(refs):
                    stage[w, 0:mid, :] = ref[pl.ds(r_in, mid, stride=GATHER_STRIDE), :]
                for m in range(outer):
                    place(r_in + m * GATHER_STRIDE,
                          *[stage[w, pl.ds(m, s_len, stride=outer), :] for w in range(3)])
                return carry

            lax.fori_loop(0, GATHER_STRIDE, gather, 0)
        else:
            def gather(r, carry, refs=(q_ref, k_ref, v_ref), dil=dil, s_len=s_len, place=place):
                rows = pl.ds(r, s_len, stride=dil) if dil > 1 else pl.ds(0, s_len)
                place(r, *[ref[rows, :] for ref in refs])
                return carry

            lax.fori_loop(0, dil, gather, 0)

        def units(it, carry, dil=dil, n_blk=n_blk, pitch=pitch, gi=gi):
            for j in range(ATTN_UNROLL):
                u = it * ATTN_UNROLL + j
                if n_blk == 1:
                    r, blk = u, 0
                elif dil == 1:
                    r, blk = 0, u
                else:
                    r, blk = lax.div(u, jnp.int32(n_blk)), lax.rem(u, jnp.int32(n_blk))
                q = qbuf[pl.ds(pl.multiple_of(u * n, n), n), :]
                k0 = pl.multiple_of(r * pitch + blk * n, n)
                mask = band & (kj >= jnp.where(blk == 0, n, 0))
                s = jnp.where(mask, _dot_nt(q, kbuf[pl.ds(k0, 2 * n), :]), NEG)
                o, lse = _softmax_pv(s, vbuf[pl.ds(k0, 2 * n), :])
                start = r + blk * (n * dil)
                dst = pl.ds(start, n, stride=dil) if dil > 1 else pl.ds(pl.multiple_of(start, n), n)
                o_sc[gi, dst, :] = o
                l_sc[gi, dst, :] = jnp.broadcast_to(lse, (n, HEAD_DIM))
            return carry

        assert (dil * n_blk) % ATTN_UNROLL == 0
        lax.fori_loop(0, dil * n_blk // ATTN_UNROLL, units, 0)

    def merge(c, carry):
        rows = pl.ds(pl.multiple_of(c * MERGE_ROWS, MERGE_ROWS), MERGE_ROWS)
        outs = [o_sc[gi, rows, :] for gi in range(N_DGROUPS)]
        lses = [l_sc[gi, rows, :] for gi in range(N_DGROUPS)]
        b_ref[rows, :] = _merge3(outs, lses).astype(b_ref.dtype)
        return carry

    lax.fori_loop(0, t // MERGE_ROWS, merge, 0)


def dilated_attn_prompt(qkv, batch, seq):
    assert seq % (BAND * max(d for _, d in DIL_GROUPS)) == 0 and seq % MERGE_ROWS == 0
    kv_rows = max(seq + dil * BAND for _, dil in DIL_GROUPS)

    def spec(which, gi):
        return pl.BlockSpec((seq, HEAD_DIM), lambda b, h: (b, (which * N_DGROUPS + gi) * HEADS + h))

    in_specs = [spec(which, gi) for gi in range(N_DGROUPS) for which in range(3)]
    return pl.pallas_call(
        _dil_prompt_kernel,
        grid=(batch, HEADS),
        in_specs=in_specs,
        out_specs=pl.BlockSpec((seq, HEAD_DIM), lambda b, h: (b, h)),
        out_shape=jax.ShapeDtypeStruct((batch * seq, D_GRP), BF16),
        scratch_shapes=[pltpu.VMEM((seq, HEAD_DIM), BF16), pltpu.VMEM((kv_rows, HEAD_DIM), BF16),
                        pltpu.VMEM((kv_rows, HEAD_DIM), BF16),
                        pltpu.VMEM((N_DGROUPS, seq, HEAD_DIM), F32), pltpu.VMEM((N_DGROUPS, seq, HEAD_DIM), F32),
                        pltpu.VMEM((3, seq // GATHER_STRIDE, HEAD_DIM), F32)],
        compiler_params=_params(("parallel", "parallel"), 40),
        name="dil_attn_prompt",
    )(*([qkv] * (3 * N_DGROUPS)))


def _dil_sample_kernel(q_ref, kn_ref, vn_ref, *rest):
    cache_refs, b_ref = rest[:2 * N_DGROUPS], rest[2 * N_DGROUPS]
    ds = q_ref.shape[0]
    qn = lax.broadcasted_iota(jnp.int32, (ds, ds), 0)
    pn = lax.broadcasted_iota(jnp.int32, (ds, ds), 1)
    units = [(h, gi) for h in range(HEADS) for gi in range(N_DGROUPS)]

    def keys_of(h, gi):
        dil = DIL_GROUPS[gi][1]
        kc_ref = cache_refs[2 * gi]
        if len(kc_ref.shape) == 3:
            n_sub, kept = kc_ref.shape[0], kc_ref.shape[1] // HEADS
            n_keys = n_sub * kept
            head = lambda ref: ref[:, pl.ds(h, kept, stride=HEADS), :].reshape(n_keys, HEAD_DIM).astype(BF16)
            fj = lax.broadcasted_iota(jnp.int32, (ds, n_keys), 1)
            return head, fj // kept * dil + fj % kept, n_sub * dil
        n_keys = kc_ref.shape[0] // HEADS
        head = lambda ref: _head_rows(ref, h, n_keys).astype(BF16)
        return head, lax.broadcasted_iota(jnp.int32, (ds, n_keys), 1), n_keys

    scores = []
    for h, gi in units:
        win, dil = DIL_GROUPS[gi]
        cols = slice(gi * D_GRP + h * HEAD_DIM, gi * D_GRP + (h + 1) * HEAD_DIM)
        head, pj, cache_len = keys_of(h, gi)
        dist = cache_len + lax.broadcasted_iota(jnp.int32, pj.shape, 0) - pj
        mask_c = (dist % dil == 0) & (dist <= dil * (win // dil - 1))
        mask_n = (qn >= pn) & ((qn - pn) % dil == 0)
        q = q_ref[:, cols].astype(BF16)
        scores.append((jnp.where(mask_c, _dot_nt(q, head(cache_refs[2 * gi])), NEG),
                       jnp.where(mask_n, _dot_nt(q, kn_ref[:, cols].astype(BF16)), NEG)))
    probs = []
    for s_c, s_n in scores:
        mx = jnp.maximum(jnp.max(s_c, axis=-1, keepdims=True), jnp.max(s_n, axis=-1, keepdims=True))
        p_c, p_n = jnp.exp(s_c - mx), jnp.exp(s_n - mx)
        l = jnp.sum(p_c, axis=-1, keepdims=True) + jnp.sum(p_n, axis=-1, keepdims=True)
        probs.append((p_c.astype(BF16), p_n.astype(BF16), l, mx + jnp.log(l)))
    outs = {}
    for (h, gi), (p_c, p_n, l, lse) in zip(units, probs):
        cols = slice(gi * D_GRP + h * HEAD_DIM, gi * D_GRP + (h + 1) * HEAD_DIM)
        head, _, _ = keys_of(h, gi)
        o = _dot(p_c, head(cache_refs[2 * gi + 1])) + _dot(p_n, vn_ref[:, cols].astype(BF16))
        outs[h, gi] = (o / l, lse)
    for h in range(HEADS):
        per_group = [outs[h, gi] for gi in range(N_DGROUPS)]
        merged = _merge3([o for o, _ in per_group], [lse for _, lse in per_group])
        b_ref[:, h * HEAD_DIM:(h + 1) * HEAD_DIM] = merged.astype(b_ref.dtype)


def dilated_attn_sample(qkv, caches, layer):
    b, ds, _ = qkv.shape
    caches = list(caches)
    cache_specs = []
    for gi, (win, dil) in enumerate(DIL_GROUPS):
        layers, _, rows, _ = caches[2 * gi].shape
        cache_len = rows // HEADS
        assert cache_len - dil * (win // dil - 1) >= 0
        kept = dil // 2
        if cache_len % dil == 0 and ds <= kept and kept * HEADS % SUBLANES == 0:
            for c in (2 * gi, 2 * gi + 1):
                caches[c] = caches[c].reshape(layers, b, cache_len // dil, dil * HEADS, HEAD_DIM)
            cache_specs += [pl.BlockSpec((None, None, cache_len // dil, kept * HEADS, HEAD_DIM),
                                         lambda i: (layer, i, 0, 0, 0))] * 2
        else:
            cache_specs += [pl.BlockSpec((None, None, rows, HEAD_DIM), lambda i: (layer, i, 0, 0))] * 2

    new = lambda which: pl.BlockSpec((None, ds, D_ATT), lambda i: (i, 0, which))
    return pl.pallas_call(
        _dil_sample_kernel,
        grid=(b,),
        in_specs=[new(0), new(1), new(2)] + cache_specs,
        out_specs=pl.BlockSpec((None, ds, D_GRP), lambda i: (i, 0, 0)),
        out_shape=jax.ShapeDtypeStruct((b, ds, D_GRP), BF16),
        compiler_params=_params(("parallel",), 48),
        name="dil_attn_sample",
    )(qkv, qkv, qkv, *caches)


def _mix_out_kernel(x_ref, a_ref, b_ref, wa_ref, wb_ref, y_ref):
    y_ref[...] = x_ref[...] + _dot(a_ref[...], wa_ref[...]) + _dot(b_ref[...], wb_ref[...])


def mix_out(x, a, b, w_out, layer, tm):
    m = x.shape[0]
    row = lambda width: pl.BlockSpec((tm, width), lambda i: (i, 0))
    return pl.pallas_call(
        _mix_out_kernel,
        grid=(m // tm,),
        in_specs=[row(D_MODEL), row(D_CONV), row(D_GRP),
                  pl.BlockSpec((None, D_CONV, D_MODEL), lambda i: (layer, 0, 0)),
                  pl.BlockSpec((None, D_GRP, D_MODEL), lambda i: (layer, D_CONV // D_GRP, 0))],
        out_specs=row(D_MODEL),
        out_shape=jax.ShapeDtypeStruct((m, D_MODEL), F32),
        compiler_params=_params(("parallel",), 48),
        name="mix_out",
    )(x, a, b, w_out, w_out)


def _mix_mem_kernel(x_ref, c_ref, lg_ref, lb_ref, b_ref, wa_ref, wb_ref,
                    g_ref, wq_ref, qg_ref, k_ref, v_ref, wo_ref, y_ref):
    a = _ln_silu(c_ref[...], lg_ref, lb_ref).astype(BF16)
    x1 = x_ref[...] + _dot(a, wa_ref[...]) + _dot(b_ref[...], wb_ref[...])
    y_ref[...] = _mem_attn_tail(x1, 1, g_ref, wq_ref, qg_ref, _kv_head_of(k_ref, v_ref), wo_ref)


def mix_out_mem(x, c, lg, lb, b, w_out, layer, mem_args, batch, tm):
    m = x.shape[0]
    nt = m // batch // tm
    row = lambda width: pl.BlockSpec((tm, width), lambda i, j: (i * nt + j, 0))
    vec = _resident((1, D_CONV), (0, 0))
    mem_specs, mem_ops = _mem_operands(*mem_args)
    return pl.pallas_call(
        _mix_mem_kernel,
        grid=(batch, nt),
        in_specs=[row(D_MODEL), row(D_CONV), vec, vec, row(D_GRP),
                  _resident((None, D_CONV, D_MODEL), (layer, 0, 0)),
                  _resident((None, D_GRP, D_MODEL), (layer, D_CONV // D_GRP, 0))] + mem_specs,
        out_specs=row(D_MODEL),
        out_shape=jax.ShapeDtypeStruct((m, D_MODEL), F32),
        compiler_params=_params(("parallel", "parallel"), 56),
        name="mix_out_mem",
    )(x, c, lg, lb, b, w_out, w_out, *mem_ops)


def _sgu_gated(u_ref, gv_ref, lg_ref, lb_ref, ws_ref, bs_ref):
    gv = gv_ref[...]
    mu = jnp.mean(gv, axis=-1, keepdims=True)
    vc = gv - mu
    v = vc * lax.rsqrt(jnp.mean(vc * vc, axis=-1, keepdims=True) + EPS) * lg_ref[...] + lb_ref[...]
    vb = v.astype(BF16)
    rows, cm = gv.shape[0], ws_ref.shape[1]
    gated = []
    for g in range(N_SG):
        cols = slice(g * D_SG, (g + 1) * D_SG)
        bias = bs_ref[:, g:g + 1]
        if cm == SUBLANES:
            v3 = vb[:, cols].astype(F32).reshape(rows // cm, cm, D_SG)
            wsg = ws_ref[g].astype(F32)
            sv = bias[None] + sum(v3[:, s:s + 1, :] * wsg[:, s:s + 1][None] for s in range(cm))
            sv = sv.reshape(rows, D_SG)
        else:
            sv = jnp.concatenate([_dot(ws_ref[g], vb[c * cm:(c + 1) * cm, cols]) + bias
                                  for c in range(rows // cm)], axis=0)
        gated.append((u_ref[:, cols] * sv).astype(BF16))
    return jnp.concatenate(gated, axis=-1), v


def _sgu_kernel(x_ref, u_ref, gv_ref, lg_ref, lb_ref, ws_ref, bs_ref, w_ref, *out_refs, emit_v):
    gated, v = _sgu_gated(u_ref, gv_ref, lg_ref, lb_ref, ws_ref, bs_ref)
    if emit_v:
        out_refs[1][...] = v
    out_refs[0][...] = x_ref[...] + _dot(gated, w_ref[...])


def _sgu_mem_kernel(x_ref, u_ref, gv_ref, lg_ref, lb_ref, ws_ref, bs_ref, w_ref,
                    g_ref, wq_ref, qg_ref, k_ref, v_ref, wo_ref, y_ref):
    gated, _ = _sgu_gated(u_ref, gv_ref, lg_ref, lb_ref, ws_ref, bs_ref)
    x1 = x_ref[...] + _dot(gated, w_ref[...])
    y_ref[...] = _mem_attn_tail(x1, 1, g_ref, wq_ref, qg_ref, _kv_head_of(k_ref, v_ref), wo_ref)


def sgu_out_mem(x, z, lg, lb, ws, bs, w_out, layer, mem_args, batch, tm):
    m = x.shape[0]
    cm = ws.shape[1]
    nt = m // batch // tm
    assert tm % cm == 0
    row = lambda jblk: pl.BlockSpec((tm, D_GATE), lambda i, j: (i * nt + j, jblk))
    vec = pl.BlockSpec((1, D_GATE), lambda i, j: (0, 0))
    mem_specs, mem_ops = _mem_operands(*mem_args)
    return pl.pallas_call(
        _sgu_mem_kernel,
        grid=(batch, nt),
        in_specs=[row(0), row(0), row(1), vec, vec,
                  _resident((N_SG, cm, cm), (0, 0, 0)), _resident((cm, N_SG), (0, 0)),
                  _resident((None, D_GATE, D_MODEL), (layer, 0, 0))] + mem_specs,
        out_specs=row(0),
        out_shape=jax.ShapeDtypeStruct((m, D_MODEL), F32),
        compiler_params=_params(("parallel", "parallel"), 56),
        name="sgu_out_mem",
    )(x, z, z, lg, lb, ws, bs, w_out, *mem_ops)


def sgu_out(x, z, lg, lb, ws, bs, w_out, layer, tm, emit_v):
    m = x.shape[0]
    cm = ws.shape[1]
    assert tm % cm == 0
    row = lambda jblk: pl.BlockSpec((tm, D_GATE), lambda i: (i, jblk))
    vec = pl.BlockSpec((1, D_GATE), lambda i: (0, 0))
    out_specs = [row(0)]
    out_shape = [jax.ShapeDtypeStruct((m, D_MODEL), F32)]
    if emit_v:
        out_specs.append(row(0))
        out_shape.append(jax.ShapeDtypeStruct((m, D_GATE), F32))
    res = pl.pallas_call(
        functools.partial(_sgu_kernel, emit_v=emit_v),
        grid=(m // tm,),
        in_specs=[row(0), row(0), row(1), vec, vec,
                  pl.BlockSpec((N_SG, cm, cm), lambda i: (0, 0, 0)),
                  pl.BlockSpec((cm, N_SG), lambda i: (0, 0)),
                  pl.BlockSpec((None, D_GATE, D_MODEL), lambda i: (layer, 0, 0))],
        out_specs=out_specs,
        out_shape=out_shape,
        compiler_params=_params(("parallel",), 52),
        name="sgu_out",
    )(x, z, z, lg, lb, ws, bs, w_out)
    return res if emit_v else (res[0], None)


def _mem_attn_tail(x, n_seq, g_ref, wq_ref, qg_ref, kv_head, wo_ref):
    tm = x.shape[0] // n_seq
    h = (_rms(x) * g_ref[...]).astype(BF16)
    q = _head_norm(_dot(h, wq_ref[...]), qg_ref[...]).astype(BF16)
    units = [(b, hd) for b in range(n_seq) for hd in range(HEADS)]
    scores = [_dot_nt(q[b * tm:(b + 1) * tm, hd * HEAD_DIM:(hd + 1) * HEAD_DIM], kv_head(b, hd)[0]) for b, hd in units]
    outs = [_softmax_pv(s, kv_head(b, hd)[1])[0].astype(BF16) for s, (b, hd) in zip(scores, units)]
    per_seq = [jnp.concatenate(outs[b * HEADS:(b + 1) * HEADS], axis=-1) for b in range(n_seq)]
    o = jnp.concatenate(per_seq, axis=0) if n_seq > 1 else per_seq[0]
    return x + _dot(o, wo_ref[...])


def _kv_head_of(k_ref, v_ref):
    n_mem = k_ref.shape[0] // HEADS
    return lambda b, hd: (_head_rows(k_ref, hd, n_mem).astype(BF16), _head_rows(v_ref, hd, n_mem).astype(BF16))


def _resident(shape, index):
    return pl.BlockSpec(shape, lambda *_: index, pipeline_mode=pl.Buffered(1))


def _mem_operands(g, wq, q_gain, k, v, kv_layer, wo, layer):
    specs = [_resident((1, D_MODEL), (0, 0)),
             _resident((None, D_MODEL, D_MEMATT), (layer, 0, 0)),
             _resident((1, D_MEMATT), (0, 0)),
             pl.BlockSpec((None, None, k.shape[2], HEAD_DIM), lambda i, j: (kv_layer, i, 0, 0)),
             pl.BlockSpec((None, None, k.shape[2], HEAD_DIM), lambda i, j: (kv_layer, i, 0, 0)),
             _resident((None, D_MEMATT, D_MODEL), (layer, 0, 0))]
    return specs, (g, wq, q_gain, k, v, wo)


def _mem_attn_kernel(x_ref, g_ref, wq_ref, qg_ref, k_ref, v_ref, wo_ref, y_ref):
    bb, tm, _ = x_ref.shape
    n_mem = k_ref.shape[1] // HEADS

    def kv_head(b, hd):
        rows = pl.ds(hd, n_mem, stride=HEADS)
        return k_ref[b, rows, :].astype(BF16), v_ref[b, rows, :].astype(BF16)

    x = x_ref[...].reshape(bb * tm, D_MODEL)
    y_ref[...] = _mem_attn_tail(x, bb, g_ref, wq_ref, qg_ref, kv_head, wo_ref).reshape(bb, tm, D_MODEL)


def mem_attn(x, g, wq, q_gain, k, v, kv_layer, wo, layer, bb, tm):
    b, t, _ = x.shape
    full = lambda shape: pl.BlockSpec(shape, lambda i, j: (0,) * len(shape))
    kv = pl.BlockSpec((None, bb, k.shape[2], HEAD_DIM), lambda i, j: (kv_layer, i, 0, 0))
    xs = pl.BlockSpec((bb, tm, D_MODEL), lambda i, j: (i, j, 0))
    return pl.pallas_call(
        _mem_attn_kernel,
        grid=(b // bb, t // tm),
        in_specs=[xs, full((1, D_MODEL)),
                  pl.BlockSpec((None, D_MODEL, D_MEMATT), lambda i, j: (layer, 0, 0)),
                  full((1, D_MEMATT)), kv, kv,
                  pl.BlockSpec((None, D_MEMATT, D_MODEL), lambda i, j: (layer, 0, 0))],
        out_specs=xs,
        out_shape=jax.ShapeDtypeStruct((b, t, D_MODEL), F32),
        compiler_params=_params(("parallel", "parallel"), 40),
        name="mem_attn",
    )(x, g, wq, q_gain, k, v, wo)


def _tile_heads(g):
    return jnp.tile(g, HEADS)


def _row_tile(m):
    return min(m, 1024)


def kernel(x_prompt, x_sample, mem_prompt, state_conv, cache_k_w128, cache_v_w128, cache_k_w512, cache_v_w512,
           cache_k_w2048, cache_v_w2048, cache_mem_k, cache_mem_v, g_mix, w_in_e, conv_w, conv_b, conv_ln_g,
           conv_ln_b, q_norm_e, k_norm_e, w_out_e, w_in_o, b_in_o, v_ln_g, v_ln_b, w_s, b_s, w_out_o, g_xmem,
           g_mem, wq_mem, wk_mem, wv_mem, q_norm_mem, k_norm_mem, wo_mem, g_ffn, w_ffn1, w_ffn2):
    depth = g_mix.shape[0]
    bp, tp, _ = x_prompt.shape
    bs, ts, _ = x_sample.shape
    mp, ms = bp * tp, bs * ts
    scale = HEAD_DIM ** -0.5
    row = lambda v: v.reshape(1, -1)
    th_rows = lambda c: c.reshape(c.shape[0], c.shape[1], c.shape[2] * HEADS, HEAD_DIM)
    caches = [th_rows(c) for c in (cache_k_w128, cache_v_w128, cache_k_w512, cache_v_w512,
                                   cache_k_w2048, cache_v_w2048)]
    mem_k_s, mem_v_s = th_rows(cache_mem_k), th_rows(cache_mem_v)

    w_in_e, w_out_e, w_in_o, w_out_o = (w.astype(BF16) for w in (w_in_e, w_out_e, w_in_o, w_out_o))
    qkv_group0 = 2 * D_CONV // D_GRP
    wq_mem, wo_mem = wq_mem.astype(BF16), wo_mem.astype(BF16)
    wkv_mem = jnp.concatenate([wk_mem, wv_mem], axis=2).astype(BF16)

    xp = x_prompt.reshape(mp, D_MODEL)
    xs = x_sample.reshape(ms, D_MODEL)
    mem = mem_prompt.reshape(bp * N_MEM, D_MODEL)
    tm_p, tm_s, tm_mem = _row_tile(mp), _row_tile(ms), _row_tile(bp * N_MEM)
    assert tp % tm_p == 0 or tm_p % tp == 0
    assert ms == tm_s

    conv_pl, conv_sl, kv_pl, kv_sl, memk_pl, memv_pl, chunk_sl = [], [], [], [], [], [], []
    for i in range(depth):
        j = i // 2
        kgain = jnp.stack([_tile_heads(k_norm_mem[i]), jnp.ones((D_MEMATT,), F32)])
        mk, mv = headnorm_proj(mem, row(g_mem[i]), wkv_mem, i, 0, kgain, 1, tm_mem, D_MEMATT,
                               [(0, tm_mem, 1), (1, tm_mem, 1)], False)
        memk_pl.append(mk.reshape(bp, N_MEM, HEADS, HEAD_DIM))
        memv_pl.append(mv.reshape(bp, N_MEM, HEADS, HEAD_DIM))
        qgain = row(_tile_heads(q_norm_mem[i]) * scale)
        mem_p = (row(g_xmem[i]), wq_mem, qgain, mk.reshape(1, bp, N_MEM * HEADS, HEAD_DIM),
                 mv.reshape(1, bp, N_MEM * HEADS, HEAD_DIM), 0, wo_mem, i)
        mem_s = (row(g_xmem[i]), wq_mem, qgain, mem_k_s, mem_v_s, i, wo_mem, i)

        if i % 2 == 0:
            gains = jnp.concatenate([
                jnp.stack([_tile_heads(q_norm_e[j, gi]) * scale for gi in range(N_DGROUPS)]),
                jnp.stack([_tile_heads(k_norm_e[j, gi]) for gi in range(N_DGROUPS)]),
                jnp.ones((N_DGROUPS, D_GRP), F32)])
            conv_args = (conv_w[j], row(conv_b[j]), row(conv_ln_g[j]), row(conv_ln_b[j]))

            def kv_tiles(seq, tm):
                tiles = []
                for gi, (win, _) in enumerate(DIL_GROUPS):
                    keep = min(win, seq)
                    assert keep == seq or (keep <= tm and seq % tm == 0)
                    spec = (tm, 1) if keep == seq else (keep, seq // tm)
                    tiles += [((1 + which) * N_DGROUPS + gi,) + spec for which in range(2)]
                return tiles

            a = glu_proj(xp, row(g_mix[i]), w_in_e, j, tm_p)
            qkv, *new_kv = headnorm_proj(xp, row(g_mix[i]), w_in_e, j, qkv_group0, gains, 2 * N_DGROUPS, tm_p, D_ATT,
                                         kv_tiles(tp, tm_p), True)
            kv_pl.append([kv.reshape(bp, -1, HEADS, HEAD_DIM) for kv in new_kv])
            a3 = a.reshape(bp, tp, D_CONV)
            conv_pl.append(a3[:, tp - (CONV_W - 1):])
            c_out = conv_prompt(a3, *conv_args[:2], tt=512).reshape(mp, D_CONV)
            b_out = dilated_attn_prompt(qkv, bp, tp)
            xp = mix_out_mem(xp, c_out, *conv_args[2:], b_out, w_out_e, j, mem_p, bp, 512)

            a = glu_proj(xs, row(g_mix[i]), w_in_e, j, tm_s)
            qkv, *new_kv = headnorm_proj(xs, row(g_mix[i]), w_in_e, j, qkv_group0, gains, 2 * N_DGROUPS, tm_s, D_ATT,
                                         kv_tiles(ts, tm_s), True)
            kv_sl.append([kv.reshape(bs, -1, HEADS, HEAD_DIM) for kv in new_kv])
            apad = jnp.concatenate([state_conv[j], a.reshape(bs, ts, D_CONV)], axis=1)
            conv_sl.append(apad[:, apad.shape[1] - (CONV_W - 1):])
            a_out = conv_sample(apad, *conv_args).reshape(ms, D_CONV)
            b_out = dilated_attn_sample(qkv.reshape(bs, ts, 3 * D_ATT), caches, j).reshape(ms, D_GRP)
            xs = mix_out(xs, a_out, b_out, w_out_e, j, tm_s)
        else:
            sgu_vecs = (row(v_ln_g[j]), row(v_ln_b[j]))
            tril = jnp.tril(jnp.ones((CHUNK, CHUNK), F32))

            def spatial(t):
                c = min(CHUNK, t)
                assert c in (SUBLANES, CHUNK) and t % c == 0
                return (w_s[j][:, :c, :c] * tril[:c, :c]).astype(BF16), b_s[j][:, :c].T

            z = gelu_proj(xp, row(g_mix[i]), w_in_o, j, row(b_in_o[j]), tm_p)
            xp = sgu_out_mem(xp, z, *sgu_vecs, *spatial(tp), w_out_o, j, mem_p, bp, 512)

            z = gelu_proj(xs, row(g_mix[i]), w_in_o, j, row(b_in_o[j]), tm_s)
            xs, v = sgu_out(xs, z, *sgu_vecs, *spatial(ts), w_out_o, j, 256, emit_v=True)
            chunk_sl.append(v.reshape(bs, ts, D_GATE))

        xs = mem_attn(xs.reshape(bs, ts, D_MODEL), *mem_s, 8, ts).reshape(ms, D_MODEL)
        xs, w1, w2 = ffn_cast(xs, row(g_ffn[i]), w_ffn1, w_ffn2, i, 512)
        xp = ffn(xp, row(g_ffn[i]), w1, w2, 0, tm_p, 1024)

    stack = lambda items: jnp.stack(items)
    kv_p = [stack([kv[n] for kv in kv_pl]) for n in range(2 * N_DGROUPS)]
    kv_s = [stack([kv[n] for kv in kv_sl]) for n in range(2 * N_DGROUPS)]
    return (xp.reshape(bp, tp, D_MODEL), xs.reshape(bs, ts, D_MODEL), stack(conv_pl), stack(conv_sl),
            *kv_p, *kv_s, stack(memk_pl), stack(memv_pl), stack(chunk_sl))
```

```python
import functools
import math

import numpy as np
import jax
import jax.numpy as jnp
from jax import lax
from jax.experimental import pallas as pl
from jax.experimental.pallas import tpu as pltpu

D_MODEL = 2048
EPS = 1e-6
NEG = -1e30
D_CONV = D_MODEL // 2
CONV_W = 31
HEAD_DIM = 128
DIL_GROUPS = ((128, 1), (512, 4), (2048, 16))
N_DGROUPS = len(DIL_GROUPS)
HEADS = 4
D_GRP = HEADS * HEAD_DIM
D_ATT = N_DGROUPS * D_GRP
CHUNK = 128
D_GATE = D_MODEL
N_SG = 8
D_SG = D_GATE // N_SG
N_MEM = 256
D_MEMATT = HEADS * HEAD_DIM
D_FF = 4 * D_MODEL

F32 = jnp.float32
BF16 = jnp.bfloat16
MIB = 1024 * 1024
NORM_ROWS = 256
CONV_HALO = 32
SUBLANES = 8
MERGE_ROWS = 256
BAND = DIL_GROUPS[0][0] // DIL_GROUPS[0][1]
GATHER_STRIDE = 4
ATTN_UNROLL = 16
CONV_ROWS = 16
ROW_TILE = 1024
FUSED_ROW_TILE = 512
SGU_SAMPLE_ROW_TILE = 256
FFN_HIDDEN_TILE = 1024
FFN_CAST_HIDDEN_TILE = 512
MEM_SEQS_PER_STEP = 8


def _params(sem, vmem_mib):
    return pltpu.CompilerParams(dimension_semantics=sem, vmem_limit_bytes=vmem_mib * MIB)


def _dot(a, b):
    return jnp.dot(a, b, preferred_element_type=F32)


def _dot_nt(a, b):
    return lax.dot_general(a, b, (((1,), (1,)), ((), ())), preferred_element_type=F32)


def _rms(x):
    return x * lax.rsqrt(jnp.mean(x * x, axis=-1, keepdims=True) + EPS)


def _norm_to_scratch(x_ref, g_ref, h_ref):
    rows = x_ref.shape[0]
    step = min(NORM_ROWS, rows)

    def body(c, carry):
        r = pl.multiple_of(c * step, step)
        x = x_ref[pl.ds(r, step), :]
        h_ref[pl.ds(r, step), :] = (_rms(x) * g_ref[...]).astype(BF16)
        return carry

    lax.fori_loop(0, rows // step, body, 0)


def _head_norm(acc, gain):
    parts = [_rms(acc[:, h * HEAD_DIM:(h + 1) * HEAD_DIM]) for h in range(acc.shape[1] // HEAD_DIM)]
    return jnp.concatenate(parts, axis=-1) * gain


def _head_rows(ref, h, rows):
    return ref[pl.ds(h, rows, stride=HEADS), :]


def _softmax_pv(s, v):
    mx = jnp.max(s, axis=-1, keepdims=True)
    p = jnp.exp(s - mx)
    l = jnp.sum(p, axis=-1, keepdims=True)
    return _dot(p.astype(BF16), v) / l, mx + jnp.log(l)


def _merge3(outs, lses):
    mx = jnp.maximum(jnp.maximum(lses[0], lses[1]), lses[2])
    es = [jnp.exp(l - mx) for l in lses]
    return (es[0] * outs[0] + es[1] * outs[1] + es[2] * outs[2]) / (es[0] + es[1] + es[2])


def _glu_kernel(x_ref, g_ref, wv_ref, wg_ref, o_ref, h_ref):
    @pl.when(pl.program_id(1) == 0)
    def _():
        _norm_to_scratch(x_ref, g_ref, h_ref)

    h = h_ref[...]
    val = _dot(h, wv_ref[...])
    gate = _dot(h, wg_ref[...])
    o_ref[...] = val * jax.nn.sigmoid(gate)


def glu_proj(x, g, w_in, layer, tm):
    m = x.shape[0]
    tn = D_CONV
    nj = D_CONV // tn
    return pl.pallas_call(
        _glu_kernel,
        grid=(m // tm, nj),
        in_specs=[
            pl.BlockSpec((tm, D_MODEL), lambda i, j: (i, 0)),
            pl.BlockSpec((1, D_MODEL), lambda i, j: (0, 0)),
            pl.BlockSpec((None, D_MODEL, tn), lambda i, j: (layer, 0, j)),
            pl.BlockSpec((None, D_MODEL, tn), lambda i, j: (layer, 0, j + nj)),
        ],
        out_specs=pl.BlockSpec((tm, tn), lambda i, j: (i, j)),
        out_shape=jax.ShapeDtypeStruct((m, D_CONV), F32),
        scratch_shapes=[pltpu.VMEM((tm, D_MODEL), BF16)],
        compiler_params=_params(("parallel", "arbitrary"), 48),
        name="glu_proj",
    )(x, g, w_in, w_in)


def _headnorm_proj_kernel(x_ref, g_ref, *rest, per_tile, n_normed, emit_main, th_tiles):
    w_refs, gain_ref, rest = rest[:per_tile], rest[per_tile], rest[per_tile + 1:]
    n_th = len(th_tiles)
    if emit_main:
        o_ref, th_refs, h_ref = rest[0], rest[1:1 + n_th], rest[1 + n_th]
    else:
        th_refs, h_ref, o_ref = rest[:n_th], rest[n_th], rest[n_th + 1]
    i, j = pl.program_id(0), pl.program_id(1)
    tm = x_ref.shape[0]

    @pl.when(j == 0)
    def _():
        _norm_to_scratch(x_ref, g_ref, h_ref)

    h = h_ref[...]
    for s in range(per_tile):
        cols = slice(s * D_GRP, (s + 1) * D_GRP)
        acc = _dot(h, w_refs[s][...])
        o_ref[:, cols] = jnp.where(j * per_tile + s < n_normed, _head_norm(acc, gain_ref[0, :, cols]), acc)

    for th_ref, (col, period) in zip(th_refs, th_tiles):
        rows = th_ref.shape[0] // HEADS
        c0 = col % per_tile * D_GRP

        @pl.when((j == col // per_tile) & (i % period == period - 1))
        def _():
            for h in range(HEADS):
                th_ref[pl.ds(h, rows, stride=HEADS), :] = o_ref[tm - rows:, c0 + h * HEAD_DIM:c0 + (h + 1) * HEAD_DIM]


def _th_index(period, i, j):
    return (i // period, 0)


def _w_group_index(layer, first, per_tile, s, i, j):
    return (layer, 0, first + j * per_tile + s)


def headnorm_proj(x, g, w, layer, group0, gains, n_normed, tm, tn, th_tiles, emit_main):
    m = x.shape[0]
    n = gains.shape[0] * D_GRP
    nj = n // tn
    per_tile = tn // D_GRP
    assert n % tn == 0 and tn % D_GRP == 0
    out_specs, out_shape = [], []
    if emit_main:
        out_specs.append(pl.BlockSpec((tm, tn), lambda i, j: (i, j)))
        out_shape.append(jax.ShapeDtypeStruct((m, n), F32))
    for _, keep, period in th_tiles:
        assert keep <= tm and (m // tm) % period == 0
        mode = {} if period == 1 else {"pipeline_mode": pl.Buffered(1)}
        out_specs.append(pl.BlockSpec((keep * HEADS, HEAD_DIM), functools.partial(_th_index, period), **mode))
        out_shape.append(jax.ShapeDtypeStruct((m // tm // period * keep * HEADS, HEAD_DIM), F32))
    scratch = [pltpu.VMEM((tm, D_MODEL), BF16)]
    if not emit_main:
        scratch.append(pltpu.VMEM((tm, tn), F32))
    return pl.pallas_call(
        functools.partial(_headnorm_proj_kernel, per_tile=per_tile, n_normed=n_normed, emit_main=emit_main,
                          th_tiles=tuple((col, period) for col, _, period in th_tiles)),
        grid=(m // tm, nj),
        in_specs=[pl.BlockSpec((tm, D_MODEL), lambda i, j: (i, 0)),
                  pl.BlockSpec((1, D_MODEL), lambda i, j: (0, 0))]
        + [pl.BlockSpec((None, D_MODEL, D_GRP), functools.partial(_w_group_index, layer, group0, per_tile, s))
           for s in range(per_tile)]
        + [pl.BlockSpec((1, 1, tn), lambda i, j: (j, 0, 0))],
        out_specs=out_specs,
        out_shape=out_shape,
        scratch_shapes=scratch,
        compiler_params=_params(("arbitrary", "arbitrary"), 61),
        name="headnorm_proj",
    )(x, g, *([w] * per_tile), gains.reshape(nj, 1, tn))


def _gelu_proj_kernel(x_ref, g_ref, w_ref, b_ref, o_ref, h_ref):
    @pl.when(pl.program_id(1) == 0)
    def _():
        _norm_to_scratch(x_ref, g_ref, h_ref)

    z = _dot(h_ref[...], w_ref[...]) + b_ref[...]
    o_ref[...] = 0.5 * z * (1.0 + lax.erf(z * np.float32(math.sqrt(0.5))))


def gelu_proj(x, g, w, layer, b, tm):
    m = x.shape[0]
    n = w.shape[2]
    tn = 1024
    return pl.pallas_call(
        _gelu_proj_kernel,
        grid=(m // tm, n // tn),
        in_specs=[
            pl.BlockSpec((tm, D_MODEL), lambda i, j: (i, 0)),
            pl.BlockSpec((1, D_MODEL), lambda i, j: (0, 0)),
            pl.BlockSpec((None, D_MODEL, tn), lambda i, j: (layer, 0, j)),
            pl.BlockSpec((1, tn), lambda i, j: (0, j)),
        ],
        out_specs=pl.BlockSpec((tm, tn), lambda i, j: (i, j)),
        out_shape=jax.ShapeDtypeStruct((m, n), F32),
        scratch_shapes=[pltpu.VMEM((tm, D_MODEL), BF16)],
        compiler_params=_params(("parallel", "arbitrary"), 48),
        name="gelu_proj",
    )(x, g, w, b)


def _ffn_kernel(x_ref, g_ref, w1_ref, w2_ref, o_ref, h_ref):
    @pl.when(pl.program_id(1) == 0)
    def _():
        _norm_to_scratch(x_ref, g_ref, h_ref)
        o_ref[...] = x_ref[...]

    hid = jnp.maximum(_dot(h_ref[...], w1_ref[...]), 0.0)
    o_ref[...] += _dot((hid * hid).astype(BF16), w2_ref[...])


def ffn(x, g, w1, w2, layer, tm, tf):
    m = x.shape[0]
    return pl.pallas_call(
        _ffn_kernel,
        grid=(m // tm, D_FF // tf),
        in_specs=[
            pl.BlockSpec((tm, D_MODEL), lambda i, f: (i, 0)),
            pl.BlockSpec((1, D_MODEL), lambda i, f: (0, 0)),
            pl.BlockSpec((None, D_MODEL, tf), lambda i, f: (layer, 0, f)),
            pl.BlockSpec((None, tf, D_MODEL), lambda i, f: (layer, f, 0)),
        ],
        out_specs=pl.BlockSpec((tm, D_MODEL), lambda i, f: (i, 0)),
        out_shape=jax.ShapeDtypeStruct((m, D_MODEL), F32),
        scratch_shapes=[pltpu.VMEM((tm, D_MODEL), BF16)],
        compiler_params=_params(("parallel", "arbitrary"), 60),
        name="ffn",
    )(x, g, w1, w2)


def _ffn_cast_kernel(x_ref, g_ref, w1_ref, w2_ref, o_ref, w1b_ref, w2b_ref, h_ref):
    @pl.when(pl.program_id(0) == 0)
    def _():
        _norm_to_scratch(x_ref, g_ref, h_ref)
        o_ref[...] = x_ref[...]

    w1 = w1_ref[...].astype(BF16)
    w2 = w2_ref[...].astype(BF16)
    w1b_ref[...] = w1
    w2b_ref[...] = w2
    hid = jnp.maximum(_dot(h_ref[...], w1), 0.0)
    o_ref[...] += _dot((hid * hid).astype(BF16), w2)


def ffn_cast(x, g, w1, w2, layer, tf):
    m = x.shape[0]
    const = lambda shape: pl.BlockSpec(shape, lambda f: (0,) * len(shape))
    return pl.pallas_call(
        _ffn_cast_kernel,
        grid=(D_FF // tf,),
        in_specs=[const((m, D_MODEL)), const((1, D_MODEL)),
                  pl.BlockSpec((None, D_MODEL, tf), lambda f: (layer, 0, f)),
                  pl.BlockSpec((None, tf, D_MODEL), lambda f: (layer, f, 0))],
        out_specs=[const((m, D_MODEL)),
                   pl.BlockSpec((None, D_MODEL, tf), lambda f: (0, 0, f)),
                   pl.BlockSpec((None, tf, D_MODEL), lambda f: (0, f, 0))],
        out_shape=[jax.ShapeDtypeStruct((m, D_MODEL), F32),
                   jax.ShapeDtypeStruct((1, D_MODEL, D_FF), BF16),
                   jax.ShapeDtypeStruct((1, D_FF, D_MODEL), BF16)],
        scratch_shapes=[pltpu.VMEM((m, D_MODEL), BF16)],
        compiler_params=_params(("arbitrary",), 48),
        name="ffn_cast",
    )(x, g, w1, w2)


def _ln_silu(acc, lg_ref, lb_ref):
    mu = jnp.mean(acc, axis=-1, keepdims=True)
    xc = acc - mu
    y = xc * lax.rsqrt(jnp.mean(xc * xc, axis=-1, keepdims=True) + EPS)
    y = y * lg_ref[...] + lb_ref[...]
    return y * jax.nn.sigmoid(y)


def _conv_prompt_kernel(a_ref, halo_ref, w_ref, cb_ref, o_ref, sh_ref, wrep_ref):
    tt = a_ref.shape[0]
    first = pl.program_id(1) == 0
    for k in range(CONV_W):
        wrep_ref[k] = jnp.broadcast_to(w_ref[k:k + 1, :], (SUBLANES, D_CONV))
    sh_ref[0, 0:CONV_HALO, :] = jnp.where(first, 0.0, halo_ref[...])
    sh_ref[0, CONV_HALO:, :] = a_ref[...]
    span = tt + CONV_HALO - SUBLANES
    for s in range(1, SUBLANES):
        sh_ref[s, 0:span, :] = sh_ref[0, s:s + span, :]
    lead = CONV_HALO - (CONV_W - 1)
    groups = CONV_ROWS // SUBLANES

    def conv_rows(c, carry):
        r = pl.multiple_of(c * CONV_ROWS, CONV_ROWS)
        acc = jnp.zeros((groups, SUBLANES, D_CONV), F32) + cb_ref[...]
        for k in range(CONV_W):
            q, s = divmod(k + lead, SUBLANES)
            x = sh_ref[s, pl.ds(r + q * SUBLANES, CONV_ROWS), :]
            acc = acc + x.reshape(groups, SUBLANES, D_CONV) * wrep_ref[k]
        o_ref[pl.ds(r, CONV_ROWS), :] = acc.reshape(CONV_ROWS, D_CONV)
        return carry

    lax.fori_loop(0, tt // CONV_ROWS, conv_rows, 0)


def conv_prompt(a, w, cb, tt):
    b, t, _ = a.shape
    hb = tt // CONV_HALO
    vec = pl.BlockSpec((1, D_CONV), lambda i, j: (0, 0))
    return pl.pallas_call(
        _conv_prompt_kernel,
        grid=(b, t // tt),
        in_specs=[
            pl.BlockSpec((None, tt, D_CONV), lambda i, j: (i, j, 0)),
            pl.BlockSpec((None, CONV_HALO, D_CONV), lambda i, j: (i, jnp.maximum(j * hb - 1, 0), 0)),
            pl.BlockSpec((CONV_W, D_CONV), lambda i, j: (0, 0)),
            vec,
        ],
        out_specs=pl.BlockSpec((None, tt, D_CONV), lambda i, j: (i, j, 0)),
        out_shape=jax.ShapeDtypeStruct((b, t, D_CONV), F32),
        scratch_shapes=[pltpu.VMEM((SUBLANES, CONV_HALO + tt, D_CONV), F32),
                        pltpu.VMEM((CONV_W, SUBLANES, D_CONV), F32)],
        compiler_params=_params(("parallel", "arbitrary"), 40),
        name="conv_prompt",
    )(a, a, w, cb)


def _conv_sample_kernel(apad_ref, w_ref, cb_ref, lg_ref, lb_ref, o_ref):
    rows = o_ref.shape[0]
    acc = jnp.zeros((rows, D_CONV), F32) + cb_ref[...]
    for k in range(CONV_W):
        acc = acc + apad_ref[k:k + rows, :] * w_ref[k:k + 1, :]
    o_ref[...] = _ln_silu(acc, lg_ref, lb_ref).astype(o_ref.dtype)


def conv_sample(apad, w, cb, lg, lb):
    b, tp, _ = apad.shape
    t = tp - (CONV_W - 1)
    vec = pl.BlockSpec((1, D_CONV), lambda i: (0, 0))
    return pl.pallas_call(
        _conv_sample_kernel,
        grid=(b,),
        in_specs=[
            pl.BlockSpec((None, tp, D_CONV), lambda i: (i, 0, 0)),
            pl.BlockSpec((CONV_W, D_CONV), lambda i: (0, 0)),
            vec, vec, vec,
        ],
        out_specs=pl.BlockSpec((None, t, D_CONV), lambda i: (i, 0, 0)),
        out_shape=jax.ShapeDtypeStruct((b, t, D_CONV), BF16),
        compiler_params=_params(("parallel",), 32),
        name="conv_sample",
    )(apad, w, cb, lg, lb)


def _dil_prompt_kernel(*refs):
    qkv_refs = refs[:3 * N_DGROUPS]
    b_ref, qbuf, kbuf, vbuf, o_sc, l_sc, stage = refs[3 * N_DGROUPS:]
    t = b_ref.shape[0]
    n = BAND
    qi = lax.broadcasted_iota(jnp.int32, (n, 2 * n), 0)
    kj = lax.broadcasted_iota(jnp.int32, (n, 2 * n), 1)
    band = (kj > qi) & (kj <= qi + n)

    for gi, (win, dil) in enumerate(DIL_GROUPS):
        assert win // dil == n
        q_ref, k_ref, v_ref = qkv_refs[3 * gi:3 * gi + 3]
        s_len = t // dil
        n_blk = s_len // n
        pitch = s_len + n

        def place(r, q_rows, k_rows, v_rows, s_len=s_len, pitch=pitch):
            k0 = pl.multiple_of(r * pitch, n)
            qbuf[pl.ds(pl.multiple_of(r * s_len, n), s_len), :] = q_rows.astype(BF16)
            kbuf[pl.ds(k0, n), :] = jnp.zeros((n, HEAD_DIM), BF16)
            vbuf[pl.ds(k0, n), :] = jnp.zeros((n, HEAD_DIM), BF16)
            kbuf[pl.ds(k0 + n, s_len), :] = k_rows.astype(BF16)
            vbuf[pl.ds(k0 + n, s_len), :] = v_rows.astype(BF16)

        if dil > GATHER_STRIDE and dil % GATHER_STRIDE == 0:
            outer = dil // GATHER_STRIDE
            mid = t // GATHER_STRIDE

            def gather(r_in, carry, refs=(q_ref, k_ref, v_ref), s_len=s_len, outer=outer, mid=mid, place=place):
                for w, ref in enumerate(refs):
                    stage[w, 0:mid, :] = ref[pl.ds(r_in, mid, stride=GATHER_STRIDE), :]
                for m in range(outer):
                    place(r_in + m * GATHER_STRIDE,
                          *[stage[w, pl.ds(m, s_len, stride=outer), :] for w in range(3)])
                return carry

            lax.fori_loop(0, GATHER_STRIDE, gather, 0)
        else:
            def gather(r, carry, refs=(q_ref, k_ref, v_ref), dil=dil, s_len=s_len, place=place):
                rows = pl.ds(r, s_len, stride=dil) if dil > 1 else pl.ds(0, s_len)
                place(r, *[ref[rows, :] for ref in refs])
                return carry

            lax.fori_loop(0, dil, gather, 0)

        def units(it, carry, dil=dil, n_blk=n_blk, pitch=pitch, gi=gi):
            for j in range(ATTN_UNROLL):
                u = it * ATTN_UNROLL + j
                if n_blk == 1:
                    r, blk = u, 0
                elif dil == 1:
                    r, blk = 0, u
                else:
                    r, blk = lax.div(u, jnp.int32(n_blk)), lax.rem(u, jnp.int32(n_blk))
                q = qbuf[pl.ds(pl.multiple_of(u * n, n), n), :]
                k0 = pl.multiple_of(r * pitch + blk * n, n)
                mask = band & (kj >= jnp.where(blk == 0, n, 0))
                s = jnp.where(mask, _dot_nt(q, kbuf[pl.ds(k0, 2 * n), :]), NEG)
                o, lse = _softmax_pv(s, vbuf[pl.ds(k0, 2 * n), :])
                start = r + blk * (n * dil)
                dst = pl.ds(start, n, stride=dil) if dil > 1 else pl.ds(pl.multiple_of(start, n), n)
                o_sc[gi, dst, :] = o
                l_sc[gi, dst, :] = jnp.broadcast_to(lse, (n, HEAD_DIM))
            return carry

        assert (dil * n_blk) % ATTN_UNROLL == 0
        lax.fori_loop(0, dil * n_blk // ATTN_UNROLL, units, 0)

    def merge(c, carry):
        rows = pl.ds(pl.multiple_of(c * MERGE_ROWS, MERGE_ROWS), MERGE_ROWS)
        outs = [o_sc[gi, rows, :] for gi in range(N_DGROUPS)]
        lses = [l_sc[gi, rows, :] for gi in range(N_DGROUPS)]
        b_ref[rows, :] = _merge3(outs, lses).astype(b_ref.dtype)
        return carry

    lax.fori_loop(0, t // MERGE_ROWS, merge, 0)


def dilated_attn_prompt(qkv, batch, seq):
    assert seq % (BAND * max(d for _, d in DIL_GROUPS)) == 0 and seq % MERGE_ROWS == 0
    kv_rows = max(seq + dil * BAND for _, dil in DIL_GROUPS)

    def spec(which, gi):
        return pl.BlockSpec((seq, HEAD_DIM), lambda b, h: (b, (which * N_DGROUPS + gi) * HEADS + h))

    in_specs = [spec(which, gi) for gi in range(N_DGROUPS) for which in range(3)]
    return pl.pallas_call(
        _dil_prompt_kernel,
        grid=(batch, HEADS),
        in_specs=in_specs,
        out_specs=pl.BlockSpec((seq, HEAD_DIM), lambda b, h: (b, h)),
        out_shape=jax.ShapeDtypeStruct((batch * seq, D_GRP), BF16),
        scratch_shapes=[pltpu.VMEM((seq, HEAD_DIM), BF16), pltpu.VMEM((kv_rows, HEAD_DIM), BF16),
                        pltpu.VMEM((kv_rows, HEAD_DIM), BF16),
                        pltpu.VMEM((N_DGROUPS, seq, HEAD_DIM), F32), pltpu.VMEM((N_DGROUPS, seq, HEAD_DIM), F32),
                        pltpu.VMEM((3, seq // GATHER_STRIDE, HEAD_DIM), F32)],
        compiler_params=_params(("parallel", "parallel"), 40),
        name="dil_attn_prompt",
    )(*([qkv] * (3 * N_DGROUPS)))


def _dil_sample_kernel(q_ref, kn_ref, vn_ref, *rest):
    cache_refs, b_ref = rest[:2 * N_DGROUPS], rest[2 * N_DGROUPS]
    ds = q_ref.shape[0]
    qn = lax.broadcasted_iota(jnp.int32, (ds, ds), 0)
    pn = lax.broadcasted_iota(jnp.int32, (ds, ds), 1)
    units = [(h, gi) for h in range(HEADS) for gi in range(N_DGROUPS)]

    def keys_of(h, gi):
        dil = DIL_GROUPS[gi][1]
        kc_ref = cache_refs[2 * gi]
        if len(kc_ref.shape) == 3:
            n_sub, kept = kc_ref.shape[0], kc_ref.shape[1] // HEADS
            n_keys = n_sub * kept
            head = lambda ref: ref[:, pl.ds(h, kept, stride=HEADS), :].reshape(n_keys, HEAD_DIM).astype(BF16)
            fj = lax.broadcasted_iota(jnp.int32, (ds, n_keys), 1)
            return head, fj // kept * dil + fj % kept, n_sub * dil
        n_keys = kc_ref.shape[0] // HEADS
        head = lambda ref: _head_rows(ref, h, n_keys).astype(BF16)
        return head, lax.broadcasted_iota(jnp.int32, (ds, n_keys), 1), n_keys

    scores = []
    for h, gi in units:
        win, dil = DIL_GROUPS[gi]
        cols = slice(gi * D_GRP + h * HEAD_DIM, gi * D_GRP + (h + 1) * HEAD_DIM)
        head, pj, cache_len = keys_of(h, gi)
        dist = cache_len + lax.broadcasted_iota(jnp.int32, pj.shape, 0) - pj
        mask_c = (dist % dil == 0) & (dist <= dil * (win // dil - 1))
        mask_n = (qn >= pn) & ((qn - pn) % dil == 0)
        q = q_ref[:, cols].astype(BF16)
        scores.append((jnp.where(mask_c, _dot_nt(q, head(cache_refs[2 * gi])), NEG),
                       jnp.where(mask_n, _dot_nt(q, kn_ref[:, cols].astype(BF16)), NEG)))
    probs = []
    for s_c, s_n in scores:
        mx = jnp.maximum(jnp.max(s_c, axis=-1, keepdims=True), jnp.max(s_n, axis=-1, keepdims=True))
        p_c, p_n = jnp.exp(s_c - mx), jnp.exp(s_n - mx)
        l = jnp.sum(p_c, axis=-1, keepdims=True) + jnp.sum(p_n, axis=-1, keepdims=True)
        probs.append((p_c.astype(BF16), p_n.astype(BF16), l, mx + jnp.log(l)))
    outs = {}
    for (h, gi), (p_c, p_n, l, lse) in zip(units, probs):
        cols = slice(gi * D_GRP + h * HEAD_DIM, gi * D_GRP + (h + 1) * HEAD_DIM)
        head, _, _ = keys_of(h, gi)
        o = _dot(p_c, head(cache_refs[2 * gi + 1])) + _dot(p_n, vn_ref[:, cols].astype(BF16))
        outs[h, gi] = (o / l, lse)
    for h in range(HEADS):
        per_group = [outs[h, gi] for gi in range(N_DGROUPS)]
        merged = _merge3([o for o, _ in per_group], [lse for _, lse in per_group])
        b_ref[:, h * HEAD_DIM:(h + 1) * HEAD_DIM] = merged.astype(b_ref.dtype)


def dilated_attn_sample(qkv, caches, layer):
    b, ds, _ = qkv.shape
    caches = list(caches)
    cache_specs = []
    for gi, (win, dil) in enumerate(DIL_GROUPS):
        layers, _, rows, _ = caches[2 * gi].shape
        cache_len = rows // HEADS
        assert cache_len - dil * (win // dil - 1) >= 0
        kept = dil // 2
        if cache_len % dil == 0 and ds <= kept and kept * HEADS % SUBLANES == 0:
            for c in (2 * gi, 2 * gi + 1):
                caches[c] = caches[c].reshape(layers, b, cache_len // dil, dil * HEADS, HEAD_DIM)
            cache_specs += [pl.BlockSpec((None, None, cache_len // dil, kept * HEADS, HEAD_DIM),
                                         lambda i: (layer, i, 0, 0, 0))] * 2
        else:
            cache_specs += [pl.BlockSpec((None, None, rows, HEAD_DIM), lambda i: (layer, i, 0, 0))] * 2

    new = lambda which: pl.BlockSpec((None, ds, D_ATT), lambda i: (i, 0, which))
    return pl.pallas_call(
        _dil_sample_kernel,
        grid=(b,),
        in_specs=[new(0), new(1), new(2)] + cache_specs,
        out_specs=pl.BlockSpec((None, ds, D_GRP), lambda i: (i, 0, 0)),
        out_shape=jax.ShapeDtypeStruct((b, ds, D_GRP), BF16),
        compiler_params=_params(("parallel",), 48),
        name="dil_attn_sample",
    )(qkv, qkv, qkv, *caches)


def _mix_out_kernel(x_ref, a_ref, b_ref, wa_ref, wb_ref, y_ref):
    y_ref[...] = x_ref[...] + _dot(a_ref[...], wa_ref[...]) + _dot(b_ref[...], wb_ref[...])


def mix_out(x, a, b, w_out, layer, tm):
    m = x.shape[0]
    row = lambda width: pl.BlockSpec((tm, width), lambda i: (i, 0))
    return pl.pallas_call(
        _mix_out_kernel,
        grid=(m // tm,),
        in_specs=[row(D_MODEL), row(D_CONV), row(D_GRP),
                  pl.BlockSpec((None, D_CONV, D_MODEL), lambda i: (layer, 0, 0)),
                  pl.BlockSpec((None, D_GRP, D_MODEL), lambda i: (layer, D_CONV // D_GRP, 0))],
        out_specs=row(D_MODEL),
        out_shape=jax.ShapeDtypeStruct((m, D_MODEL), F32),
        compiler_params=_params(("parallel",), 48),
        name="mix_out",
    )(x, a, b, w_out, w_out)


def _mix_mem_kernel(x_ref, c_ref, lg_ref, lb_ref, b_ref, wa_ref, wb_ref,
                    g_ref, wq_ref, qg_ref, k_ref, v_ref, wo_ref, y_ref):
    x1 = x_ref[...] + _dot(b_ref[...], wb_ref[...])
    x1 = x1 + _dot(_ln_silu(c_ref[...], lg_ref, lb_ref).astype(BF16), wa_ref[...])
    y_ref[...] = _mem_attn_tail(x1, 1, g_ref, wq_ref, qg_ref, _kv_head_of(k_ref, v_ref), wo_ref)


def mix_out_mem(x, c, lg, lb, b, w_out, layer, mem_args, batch, tm):
    m = x.shape[0]
    nt = m // batch // tm
    row = lambda width: pl.BlockSpec((tm, width), lambda i, j: (i * nt + j, 0))
    vec = _resident((1, D_CONV), (0, 0))
    mem_specs, mem_ops = _mem_operands(*mem_args)
    return pl.pallas_call(
        _mix_mem_kernel,
        grid=(batch, nt),
        in_specs=[row(D_MODEL), row(D_CONV), vec, vec, row(D_GRP),
                  _resident((None, D_CONV, D_MODEL), (layer, 0, 0)),
                  _resident((None, D_GRP, D_MODEL), (layer, D_CONV // D_GRP, 0))] + mem_specs,
        out_specs=row(D_MODEL),
        out_shape=jax.ShapeDtypeStruct((m, D_MODEL), F32),
        compiler_params=_params(("parallel", "parallel"), 56),
        name="mix_out_mem",
    )(x, c, lg, lb, b, w_out, w_out, *mem_ops)


def _sgu_gated(u_ref, gv_ref, lg_ref, lb_ref, ws_ref, bs_ref):
    gv = gv_ref[...]
    mu = jnp.mean(gv, axis=-1, keepdims=True)
    vc = gv - mu
    v = vc * lax.rsqrt(jnp.mean(vc * vc, axis=-1, keepdims=True) + EPS) * lg_ref[...] + lb_ref[...]
    vb = v.astype(BF16)
    rows, cm = gv.shape[0], ws_ref.shape[1]
    gated = []
    for g in range(N_SG):
        cols = slice(g * D_SG, (g + 1) * D_SG)
        bias = bs_ref[:, g:g + 1]
        if cm == SUBLANES:
            v3 = vb[:, cols].astype(F32).reshape(rows // cm, cm, D_SG)
            wsg = ws_ref[g].astype(F32)
            sv = bias[None] + sum(v3[:, s:s + 1, :] * wsg[:, s:s + 1][None] for s in range(cm))
            sv = sv.reshape(rows, D_SG)
        else:
            sv = jnp.concatenate([_dot(ws_ref[g], vb[c * cm:(c + 1) * cm, cols]) + bias
                                  for c in range(rows // cm)], axis=0)
        gated.append((u_ref[:, cols] * sv).astype(BF16))
    return jnp.concatenate(gated, axis=-1), v


def _sgu_kernel(x_ref, u_ref, gv_ref, lg_ref, lb_ref, ws_ref, bs_ref, w_ref, *out_refs, emit_v):
    gated, v = _sgu_gated(u_ref, gv_ref, lg_ref, lb_ref, ws_ref, bs_ref)
    if emit_v:
        out_refs[1][...] = v
    out_refs[0][...] = x_ref[...] + _dot(gated, w_ref[...])


def _sgu_mem_kernel(x_ref, u_ref, gv_ref, lg_ref, lb_ref, ws_ref, bs_ref, w_ref,
                    g_ref, wq_ref, qg_ref, k_ref, v_ref, wo_ref, y_ref):
    gated, _ = _sgu_gated(u_ref, gv_ref, lg_ref, lb_ref, ws_ref, bs_ref)
    x1 = x_ref[...] + _dot(gated, w_ref[...])
    y_ref[...] = _mem_attn_tail(x1, 1, g_ref, wq_ref, qg_ref, _kv_head_of(k_ref, v_ref), wo_ref)


def sgu_out_mem(x, z, lg, lb, ws, bs, w_out, layer, mem_args, batch, tm):
    m = x.shape[0]
    cm = ws.shape[1]
    nt = m // batch // tm
    assert tm % cm == 0
    row = lambda jblk: pl.BlockSpec((tm, D_GATE), lambda i, j: (i * nt + j, jblk))
    vec = pl.BlockSpec((1, D_GATE), lambda i, j: (0, 0))
    mem_specs, mem_ops = _mem_operands(*mem_args)
    return pl.pallas_call(
        _sgu_mem_kernel,
        grid=(batch, nt),
        in_specs=[row(0), row(0), row(1), vec, vec,
                  _resident((N_SG, cm, cm), (0, 0, 0)), _resident((cm, N_SG), (0, 0)),
                  _resident((None, D_GATE, D_MODEL), (layer, 0, 0))] + mem_specs,
        out_specs=row(0),
        out_shape=jax.ShapeDtypeStruct((m, D_MODEL), F32),
        compiler_params=_params(("parallel", "parallel"), 56),
        name="sgu_out_mem",
    )(x, z, z, lg, lb, ws, bs, w_out, *mem_ops)


def sgu_out(x, z, lg, lb, ws, bs, w_out, layer, tm, emit_v):
    m = x.shape[0]
    cm = ws.shape[1]
    assert tm % cm == 0
    row = lambda jblk: pl.BlockSpec((tm, D_GATE), lambda i: (i, jblk))
    vec = pl.BlockSpec((1, D_GATE), lambda i: (0, 0))
    out_specs = [row(0)]
    out_shape = [jax.ShapeDtypeStruct((m, D_MODEL), F32)]
    if emit_v:
        out_specs.append(row(0))
        out_shape.append(jax.ShapeDtypeStruct((m, D_GATE), F32))
    res = pl.pallas_call(
        functools.partial(_sgu_kernel, emit_v=emit_v),
        grid=(m // tm,),
        in_specs=[row(0), row(0), row(1), vec, vec,
                  pl.BlockSpec((N_SG, cm, cm), lambda i: (0, 0, 0)),
                  pl.BlockSpec((cm, N_SG), lambda i: (0, 0)),
                  pl.BlockSpec((None, D_GATE, D_MODEL), lambda i: (layer, 0, 0))],
        out_specs=out_specs,
        out_shape=out_shape,
        compiler_params=_params(("parallel",), 52),
        name="sgu_out",
    )(x, z, z, lg, lb, ws, bs, w_out)
    return res if emit_v else (res[0], None)


def _mem_attn_tail(x, n_seq, g_ref, wq_ref, qg_ref, kv_head, wo_ref):
    tm = x.shape[0] // n_seq
    h = (_rms(x) * g_ref[...]).astype(BF16)
    q = _head_norm(_dot(h, wq_ref[...]), qg_ref[...]).astype(BF16)
    units = [(b, hd) for b in range(n_seq) for hd in range(HEADS)]
    scores = [_dot_nt(q[b * tm:(b + 1) * tm, hd * HEAD_DIM:(hd + 1) * HEAD_DIM], kv_head(b, hd)[0]) for b, hd in units]
    outs = [_softmax_pv(s, kv_head(b, hd)[1])[0].astype(BF16) for s, (b, hd) in zip(scores, units)]
    per_seq = [jnp.concatenate(outs[b * HEADS:(b + 1) * HEADS], axis=-1) for b in range(n_seq)]
    o = jnp.concatenate(per_seq, axis=0) if n_seq > 1 else per_seq[0]
    return x + _dot(o, wo_ref[...])


def _kv_head_of(k_ref, v_ref):
    n_mem = k_ref.shape[0] // HEADS
    return lambda b, hd: (_head_rows(k_ref, hd, n_mem).astype(BF16), _head_rows(v_ref, hd, n_mem).astype(BF16))


def _resident(shape, index):
    return pl.BlockSpec(shape, lambda *_: index, pipeline_mode=pl.Buffered(1))


def _mem_operands(g, wq, q_gain, k, v, kv_layer, wo, layer):
    specs = [_resident((1, D_MODEL), (0, 0)),
             _resident((None, D_MODEL, D_MEMATT), (layer, 0, 0)),
             _resident((1, D_MEMATT), (0, 0)),
             pl.BlockSpec((None, None, k.shape[2], HEAD_DIM), lambda i, j: (kv_layer, i, 0, 0)),
             pl.BlockSpec((None, None, k.shape[2], HEAD_DIM), lambda i, j: (kv_layer, i, 0, 0)),
             _resident((None, D_MEMATT, D_MODEL), (layer, 0, 0))]
    return specs, (g, wq, q_gain, k, v, wo)


def _mem_attn_kernel(x_ref, g_ref, wq_ref, qg_ref, k_ref, v_ref, wo_ref, y_ref):
    bb, tm, _ = x_ref.shape
    n_mem = k_ref.shape[1] // HEADS

    def kv_head(b, hd):
        rows = pl.ds(hd, n_mem, stride=HEADS)
        return k_ref[b, rows, :].astype(BF16), v_ref[b, rows, :].astype(BF16)

    x = x_ref[...].reshape(bb * tm, D_MODEL)
    y_ref[...] = _mem_attn_tail(x, bb, g_ref, wq_ref, qg_ref, kv_head, wo_ref).reshape(bb, tm, D_MODEL)


def mem_attn(x, g, wq, q_gain, k, v, kv_layer, wo, layer, bb, tm):
    b, t, _ = x.shape
    full = lambda shape: pl.BlockSpec(shape, lambda i, j: (0,) * len(shape))
    kv = pl.BlockSpec((None, bb, k.shape[2], HEAD_DIM), lambda i, j: (kv_layer, i, 0, 0))
    xs = pl.BlockSpec((bb, tm, D_MODEL), lambda i, j: (i, j, 0))
    return pl.pallas_call(
        _mem_attn_kernel,
        grid=(b // bb, t // tm),
        in_specs=[xs, full((1, D_MODEL)),
                  pl.BlockSpec((None, D_MODEL, D_MEMATT), lambda i, j: (layer, 0, 0)),
                  full((1, D_MEMATT)), kv, kv,
                  pl.BlockSpec((None, D_MEMATT, D_MODEL), lambda i, j: (layer, 0, 0))],
        out_specs=xs,
        out_shape=jax.ShapeDtypeStruct((b, t, D_MODEL), F32),
        compiler_params=_params(("parallel", "parallel"), 40),
        name="mem_attn",
    )(x, g, wq, q_gain, k, v, wo)


def _tile_heads(g):
    return jnp.tile(g, HEADS)


def _row_tile(m):
    return min(m, ROW_TILE)


def kernel(x_prompt, x_sample, mem_prompt, state_conv, cache_k_w128, cache_v_w128, cache_k_w512, cache_v_w512,
           cache_k_w2048, cache_v_w2048, cache_mem_k, cache_mem_v, g_mix, w_in_e, conv_w, conv_b, conv_ln_g,
           conv_ln_b, q_norm_e, k_norm_e, w_out_e, w_in_o, b_in_o, v_ln_g, v_ln_b, w_s, b_s, w_out_o, g_xmem,
           g_mem, wq_mem, wk_mem, wv_mem, q_norm_mem, k_norm_mem, wo_mem, g_ffn, w_ffn1, w_ffn2):
    depth = g_mix.shape[0]
    bp, tp, _ = x_prompt.shape
    bs, ts, _ = x_sample.shape
    mp, ms = bp * tp, bs * ts
    scale = HEAD_DIM ** -0.5
    row = lambda v: v.reshape(1, -1)
    th_rows = lambda c: c.reshape(c.shape[0], c.shape[1], c.shape[2] * HEADS, HEAD_DIM)
    caches = [th_rows(c) for c in (cache_k_w128, cache_v_w128, cache_k_w512, cache_v_w512,
                                   cache_k_w2048, cache_v_w2048)]
    mem_k_s, mem_v_s = th_rows(cache_mem_k), th_rows(cache_mem_v)

    w_in_e, w_out_e, w_in_o, w_out_o = (w.astype(BF16) for w in (w_in_e, w_out_e, w_in_o, w_out_o))
    qkv_group0 = 2 * D_CONV // D_GRP
    wq_mem, wo_mem = wq_mem.astype(BF16), wo_mem.astype(BF16)
    wkv_mem = jnp.concatenate([wk_mem, wv_mem], axis=2).astype(BF16)

    xp = x_prompt.reshape(mp, D_MODEL)
    xs = x_sample.reshape(ms, D_MODEL)
    mem = mem_prompt.reshape(bp * N_MEM, D_MODEL)
    tm_p, tm_s, tm_mem = _row_tile(mp), _row_tile(ms), _row_tile(bp * N_MEM)
    assert tp % tm_p == 0 or tm_p % tp == 0
    assert ms == tm_s

    conv_pl, conv_sl, kv_pl, kv_sl, memk_pl, memv_pl, chunk_sl = [], [], [], [], [], [], []
    for i in range(depth):
        j = i // 2
        kgain = jnp.stack([_tile_heads(k_norm_mem[i]), jnp.ones((D_MEMATT,), F32)])
        mk, mv = headnorm_proj(mem, row(g_mem[i]), wkv_mem, i, 0, kgain, 1, tm_mem, D_MEMATT,
                               [(0, tm_mem, 1), (1, tm_mem, 1)], False)
        memk_pl.append(mk.reshape(bp, N_MEM, HEADS, HEAD_DIM))
        memv_pl.append(mv.reshape(bp, N_MEM, HEADS, HEAD_DIM))
        qgain = row(_tile_heads(q_norm_mem[i]) * scale)
        mem_p = (row(g_xmem[i]), wq_mem, qgain, mk.reshape(1, bp, N_MEM * HEADS, HEAD_DIM),
                 mv.reshape(1, bp, N_MEM * HEADS, HEAD_DIM), 0, wo_mem, i)
        mem_s = (row(g_xmem[i]), wq_mem, qgain, mem_k_s, mem_v_s, i, wo_mem, i)

        if i % 2 == 0:
            gains = jnp.concatenate([
                jnp.stack([_tile_heads(q_norm_e[j, gi]) * scale for gi in range(N_DGROUPS)]),
                jnp.stack([_tile_heads(k_norm_e[j, gi]) for gi in range(N_DGROUPS)]),
                jnp.ones((N_DGROUPS, D_GRP), F32)])
            conv_args = (conv_w[j], row(conv_b[j]), row(conv_ln_g[j]), row(conv_ln_b[j]))

            def kv_tiles(seq, tm):
                tiles = []
                for gi, (win, _) in enumerate(DIL_GROUPS):
                    keep = min(win, seq)
                    assert keep == seq or (keep <= tm and seq % tm == 0)
                    spec = (tm, 1) if keep == seq else (keep, seq // tm)
                    tiles += [((1 + which) * N_DGROUPS + gi,) + spec for which in range(2)]
                return tiles

            a = glu_proj(xp, row(g_mix[i]), w_in_e, j, tm_p)
            qkv, *new_kv = headnorm_proj(xp, row(g_mix[i]), w_in_e, j, qkv_group0, gains, 2 * N_DGROUPS, tm_p, D_ATT,
                                         kv_tiles(tp, tm_p), True)
            kv_pl.append([kv.reshape(bp, -1, HEADS, HEAD_DIM) for kv in new_kv])
            a3 = a.reshape(bp, tp, D_CONV)
            conv_pl.append(a3[:, tp - (CONV_W - 1):])
            c_out = conv_prompt(a3, *conv_args[:2], tt=FUSED_ROW_TILE).reshape(mp, D_CONV)
            b_out = dilated_attn_prompt(qkv, bp, tp)
            xp = mix_out_mem(xp, c_out, *conv_args[2:], b_out, w_out_e, j, mem_p, bp, FUSED_ROW_TILE)

            a = glu_proj(xs, row(g_mix[i]), w_in_e, j, tm_s)
            qkv, *new_kv = headnorm_proj(xs, row(g_mix[i]), w_in_e, j, qkv_group0, gains, 2 * N_DGROUPS, tm_s, D_ATT,
                                         kv_tiles(ts, tm_s), True)
            kv_sl.append([kv.reshape(bs, -1, HEADS, HEAD_DIM) for kv in new_kv])
            apad = jnp.concatenate([state_conv[j], a.reshape(bs, ts, D_CONV)], axis=1)
            conv_sl.append(apad[:, apad.shape[1] - (CONV_W - 1):])
            a_out = conv_sample(apad, *conv_args).reshape(ms, D_CONV)
            b_out = dilated_attn_sample(qkv.reshape(bs, ts, 3 * D_ATT), caches, j).reshape(ms, D_GRP)
            xs = mix_out(xs, a_out, b_out, w_out_e, j, tm_s)
        else:
            sgu_vecs = (row(v_ln_g[j]), row(v_ln_b[j]))
            tril = jnp.tril(jnp.ones((CHUNK, CHUNK), F32))

            def spatial(t):
                c = min(CHUNK, t)
                assert c in (SUBLANES, CHUNK) and t % c == 0
                return (w_s[j][:, :c, :c] * tril[:c, :c]).astype(BF16), b_s[j][:, :c].T

            z = gelu_proj(xp, row(g_mix[i]), w_in_o, j, row(b_in_o[j]), tm_p)
            xp = sgu_out_mem(xp, z, *sgu_vecs, *spatial(tp), w_out_o, j, mem_p, bp, FUSED_ROW_TILE)

            z = gelu_proj(xs, row(g_mix[i]), w_in_o, j, row(b_in_o[j]), tm_s)
            xs, v = sgu_out(xs, z, *sgu_vecs, *spatial(ts), w_out_o, j, SGU_SAMPLE_ROW_TILE, emit_v=True)
            chunk_sl.append(v.reshape(bs, ts, D_GATE))

        xs = mem_attn(xs.reshape(bs, ts, D_MODEL), *mem_s, MEM_SEQS_PER_STEP, ts).reshape(ms, D_MODEL)
        xs, w1, w2 = ffn_cast(xs, row(g_ffn[i]), w_ffn1, w_ffn2, i, FFN_CAST_HIDDEN_TILE)
        xp = ffn(xp, row(g_ffn[i]), w1, w2, 0, tm_p, FFN_HIDDEN_TILE)

    stack = lambda items: jnp.stack(items)
    kv_p = [stack([kv[n] for kv in kv_pl]) for n in range(2 * N_DGROUPS)]
    kv_s = [stack([kv[n] for kv in kv_sl]) for n in range(2 * N_DGROUPS)]
    return (xp.reshape(bp, tp, D_MODEL), xs.reshape(bs, ts, D_MODEL), stack(conv_pl), stack(conv_sl),
            *kv_p, *kv_s, stack(memk_pl), stack(memv_pl), stack(chunk_sl))
```

```python
import functools
import math

import numpy as np
import jax
import jax.numpy as jnp
from jax import lax
from jax.experimental import pallas as pl
from jax.experimental.pallas import tpu as pltpu

D_MODEL = 2048
EPS = 1e-6
NEG = -1e30
D_CONV = D_MODEL // 2
CONV_W = 31
HEAD_DIM = 128
DIL_GROUPS = ((128, 1), (512, 4), (2048, 16))
N_DGROUPS = len(DIL_GROUPS)
HEADS = 4
D_GRP = HEADS * HEAD_DIM
D_ATT = N_DGROUPS * D_GRP
CHUNK = 128
D_GATE = D_MODEL
N_SG = 8
D_SG = D_GATE // N_SG
N_MEM = 256
D_MEMATT = HEADS * HEAD_DIM
D_FF = 4 * D_MODEL

F32 = jnp.float32
BF16 = jnp.bfloat16
MIB = 1024 * 1024
NORM_ROWS = 256
CONV_HALO = 32
SUBLANES = 8
MERGE_ROWS = 256
BAND = DIL_GROUPS[0][0] // DIL_GROUPS[0][1]
GATHER_STRIDE = 4
ATTN_UNROLL = 16
CONV_ROWS = 16
ROW_TILE = 1024
FUSED_ROW_TILE = 512
SGU_SAMPLE_ROW_TILE = 256
FFN_HIDDEN_TILE = 1024
FFN_CAST_HIDDEN_TILE = 512
MEM_SEQS_PER_STEP = 8


def _params(sem, vmem_mib):
    return pltpu.CompilerParams(dimension_semantics=sem, vmem_limit_bytes=vmem_mib * MIB)


def _dot(a, b):
    return jnp.dot(a, b, preferred_element_type=F32)


def _dot_nt(a, b):
    return lax.dot_general(a, b, (((1,), (1,)), ((), ())), preferred_element_type=F32)


def _rms(x):
    return x * lax.rsqrt(jnp.mean(x * x, axis=-1, keepdims=True) + EPS)


def _norm_to_scratch(x_ref, g_ref, h_ref):
    rows = x_ref.shape[0]
    step = min(NORM_ROWS, rows)

    def body(c, carry):
        r = pl.multiple_of(c * step, step)
        x = x_ref[pl.ds(r, step), :]
        h_ref[pl.ds(r, step), :] = (_rms(x) * g_ref[...]).astype(BF16)
        return carry

    lax.fori_loop(0, rows // step, body, 0)


def _head_norm(acc, gain):
    parts = [_rms(acc[:, h * HEAD_DIM:(h + 1) * HEAD_DIM]) for h in range(acc.shape[1] // HEAD_DIM)]
    return jnp.concatenate(parts, axis=-1) * gain


def _head_rows(ref, h, rows):
    return ref[pl.ds(h, rows, stride=HEADS), :]


def _softmax_pv(s, v):
    mx = jnp.max(s, axis=-1, keepdims=True)
    p = jnp.exp(s - mx)
    l = jnp.sum(p, axis=-1, keepdims=True)
    return _dot(p.astype(BF16), v) / l, mx + jnp.log(l)


def _merge3(outs, lses):
    mx = jnp.maximum(jnp.maximum(lses[0], lses[1]), lses[2])
    es = [jnp.exp(l - mx) for l in lses]
    return (es[0] * outs[0] + es[1] * outs[1] + es[2] * outs[2]) / (es[0] + es[1] + es[2])


def _glu_kernel(x_ref, g_ref, wv_ref, wg_ref, o_ref, h_ref):
    @pl.when(pl.program_id(1) == 0)
    def _():
        _norm_to_scratch(x_ref, g_ref, h_ref)

    h = h_ref[...]
    val = _dot(h, wv_ref[...])
    gate = _dot(h, wg_ref[...])
    o_ref[...] = val * jax.nn.sigmoid(gate)


def glu_proj(x, g, w_in, layer, tm):
    m = x.shape[0]
    tn = D_CONV
    nj = D_CONV // tn
    return pl.pallas_call(
        _glu_kernel,
        grid=(m // tm, nj),
        in_specs=[
            pl.BlockSpec((tm, D_MODEL), lambda i, j: (i, 0)),
            pl.BlockSpec((1, D_MODEL), lambda i, j: (0, 0)),
            pl.BlockSpec((None, D_MODEL, tn), lambda i, j: (layer, 0, j)),
            pl.BlockSpec((None, D_MODEL, tn), lambda i, j: (layer, 0, j + nj)),
        ],
        out_specs=pl.BlockSpec((tm, tn), lambda i, j: (i, j)),
        out_shape=jax.ShapeDtypeStruct((m, D_CONV), F32),
        scratch_shapes=[pltpu.VMEM((tm, D_MODEL), BF16)],
        compiler_params=_params(("parallel", "arbitrary"), 48),
        name="glu_proj",
    )(x, g, w_in, w_in)


def _headnorm_proj_kernel(x_ref, g_ref, *rest, per_tile, n_normed, emit_main, th_tiles):
    w_refs, gain_ref, rest = rest[:per_tile], rest[per_tile], rest[per_tile + 1:]
    n_th = len(th_tiles)
    if emit_main:
        o_ref, th_refs, h_ref = rest[0], rest[1:1 + n_th], rest[1 + n_th]
    else:
        th_refs, h_ref, o_ref = rest[:n_th], rest[n_th], rest[n_th + 1]
    i, j = pl.program_id(0), pl.program_id(1)
    tm = x_ref.shape[0]

    @pl.when(j == 0)
    def _():
        _norm_to_scratch(x_ref, g_ref, h_ref)

    h = h_ref[...]
    for s in range(per_tile):
        cols = slice(s * D_GRP, (s + 1) * D_GRP)
        acc = _dot(h, w_refs[s][...])
        o_ref[:, cols] = jnp.where(j * per_tile + s < n_normed, _head_norm(acc, gain_ref[0, :, cols]), acc)

    for th_ref, (col, period) in zip(th_refs, th_tiles):
        rows = th_ref.shape[0] // HEADS
        c0 = col % per_tile * D_GRP

        @pl.when((j == col // per_tile) & (i % period == period - 1))
        def _():
            for h in range(HEADS):
                th_ref[pl.ds(h, rows, stride=HEADS), :] = o_ref[tm - rows:, c0 + h * HEAD_DIM:c0 + (h + 1) * HEAD_DIM]


def _th_index(period, i, j):
    return (i // period, 0)


def _w_group_index(layer, first, per_tile, s, i, j):
    return (layer, 0, first + j * per_tile + s)


def headnorm_proj(x, g, w, layer, group0, gains, n_normed, tm, tn, th_tiles, emit_main):
    m = x.shape[0]
    n = gains.shape[0] * D_GRP
    nj = n // tn
    per_tile = tn // D_GRP
    assert n % tn == 0 and tn % D_GRP == 0
    out_specs, out_shape = [], []
    if emit_main:
        out_specs.append(pl.BlockSpec((tm, tn), lambda i, j: (i, j)))
        out_shape.append(jax.ShapeDtypeStruct((m, n), F32))
    for _, keep, period in th_tiles:
        assert keep <= tm and (m // tm) % period == 0
        mode = {} if period == 1 else {"pipeline_mode": pl.Buffered(1)}
        out_specs.append(pl.BlockSpec((keep * HEADS, HEAD_DIM), functools.partial(_th_index, period), **mode))
        out_shape.append(jax.ShapeDtypeStruct((m // tm // period * keep * HEADS, HEAD_DIM), F32))
    scratch = [pltpu.VMEM((tm, D_MODEL), BF16)]
    if not emit_main:
        scratch.append(pltpu.VMEM((tm, tn), F32))
    return pl.pallas_call(
        functools.partial(_headnorm_proj_kernel, per_tile=per_tile, n_normed=n_normed, emit_main=emit_main,
                          th_tiles=tuple((col, period) for col, _, period in th_tiles)),
        grid=(m // tm, nj),
        in_specs=[pl.BlockSpec((tm, D_MODEL), lambda i, j: (i, 0)),
                  pl.BlockSpec((1, D_MODEL), lambda i, j: (0, 0))]
        + [pl.BlockSpec((None, D_MODEL, D_GRP), functools.partial(_w_group_index, layer, group0, per_tile, s))
           for s in range(per_tile)]
        + [pl.BlockSpec((1, 1, tn), lambda i, j: (j, 0, 0))],
        out_specs=out_specs,
        out_shape=out_shape,
        scratch_shapes=scratch,
        compiler_params=_params(("arbitrary", "arbitrary"), 61),
        name="headnorm_proj",
    )(x, g, *([w] * per_tile), gains.reshape(nj, 1, tn))


def _gelu_proj_kernel(x_ref, g_ref, w_ref, b_ref, o_ref, h_ref):
    @pl.when(pl.program_id(1) == 0)
    def _():
        _norm_to_scratch(x_ref, g_ref, h_ref)

    z = _dot(h_ref[...], w_ref[...]) + b_ref[...]
    o_ref[...] = 0.5 * z * (1.0 + lax.erf(z * np.float32(math.sqrt(0.5))))


def gelu_proj(x, g, w, layer, b, tm):
    m = x.shape[0]
    n = w.shape[2]
    tn = 2048
    return pl.pallas_call(
        _gelu_proj_kernel,
        grid=(m // tm, n // tn),
        in_specs=[
            pl.BlockSpec((tm, D_MODEL), lambda i, j: (i, 0)),
            pl.BlockSpec((1, D_MODEL), lambda i, j: (0, 0)),
            pl.BlockSpec((None, D_MODEL, tn), lambda i, j: (layer, 0, j)),
            pl.BlockSpec((1, tn), lambda i, j: (0, j)),
        ],
        out_specs=pl.BlockSpec((tm, tn), lambda i, j: (i, j)),
        out_shape=jax.ShapeDtypeStruct((m, n), F32),
        scratch_shapes=[pltpu.VMEM((tm, D_MODEL), BF16)],
        compiler_params=_params(("parallel", "arbitrary"), 60),
        name="gelu_proj",
    )(x, g, w, b)


def _ffn_kernel(x_ref, g_ref, w1_ref, w2_ref, o_ref, h_ref):
    @pl.when(pl.program_id(1) == 0)
    def _():
        _norm_to_scratch(x_ref, g_ref, h_ref)
        o_ref[...] = x_ref[...]

    hid = jnp.maximum(_dot(h_ref[...], w1_ref[...]), 0.0)
    o_ref[...] += _dot((hid * hid).astype(BF16), w2_ref[...])


def ffn(x, g, w1, w2, layer, tm, tf):
    m = x.shape[0]
    return pl.pallas_call(
        _ffn_kernel,
        grid=(m // tm, D_FF // tf),
        in_specs=[
            pl.BlockSpec((tm, D_MODEL), lambda i, f: (i, 0)),
            pl.BlockSpec((1, D_MODEL), lambda i, f: (0, 0)),
            pl.BlockSpec((None, D_MODEL, tf), lambda i, f: (layer, 0, f)),
            pl.BlockSpec((None, tf, D_MODEL), lambda i, f: (layer, f, 0)),
        ],
        out_specs=pl.BlockSpec((tm, D_MODEL), lambda i, f: (i, 0)),
        out_shape=jax.ShapeDtypeStruct((m, D_MODEL), F32),
        scratch_shapes=[pltpu.VMEM((tm, D_MODEL), BF16)],
        compiler_params=_params(("parallel", "arbitrary"), 60),
        name="ffn",
    )(x, g, w1, w2)


def _ffn_cast_kernel(x_ref, g_ref, w1_ref, w2_ref, o_ref, w1b_ref, w2b_ref, h_ref):
    @pl.when(pl.program_id(0) == 0)
    def _():
        _norm_to_scratch(x_ref, g_ref, h_ref)
        o_ref[...] = x_ref[...]

    w1 = w1_ref[...].astype(BF16)
    w2 = w2_ref[...].astype(BF16)
    w1b_ref[...] = w1
    w2b_ref[...] = w2
    hid = jnp.maximum(_dot(h_ref[...], w1), 0.0)
    o_ref[...] += _dot((hid * hid).astype(BF16), w2)


def ffn_cast(x, g, w1, w2, layer, tf):
    m = x.shape[0]
    const = lambda shape: pl.BlockSpec(shape, lambda f: (0,) * len(shape))
    return pl.pallas_call(
        _ffn_cast_kernel,
        grid=(D_FF // tf,),
        in_specs=[const((m, D_MODEL)), const((1, D_MODEL)),
                  pl.BlockSpec((None, D_MODEL, tf), lambda f: (layer, 0, f)),
                  pl.BlockSpec((None, tf, D_MODEL), lambda f: (layer, f, 0))],
        out_specs=[const((m, D_MODEL)),
                   pl.BlockSpec((None, D_MODEL, tf), lambda f: (0, 0, f)),
                   pl.BlockSpec((None, tf, D_MODEL), lambda f: (0, f, 0))],
        out_shape=[jax.ShapeDtypeStruct((m, D_MODEL), F32),
                   jax.ShapeDtypeStruct((1, D_MODEL, D_FF), BF16),
                   jax.ShapeDtypeStruct((1, D_FF, D_MODEL), BF16)],
        scratch_shapes=[pltpu.VMEM((m, D_MODEL), BF16)],
        compiler_params=_params(("arbitrary",), 48),
        name="ffn_cast",
    )(x, g, w1, w2)


def _ln_silu(acc, lg_ref, lb_ref):
    mu = jnp.mean(acc, axis=-1, keepdims=True)
    xc = acc - mu
    y = xc * lax.rsqrt(jnp.mean(xc * xc, axis=-1, keepdims=True) + EPS)
    y = y * lg_ref[...] + lb_ref[...]
    return y * jax.nn.sigmoid(y)


def _conv_prompt_kernel(a_ref, halo_ref, w_ref, cb_ref, o_ref, sh_ref, wrep_ref):
    tt = a_ref.shape[0]
    first = pl.program_id(1) == 0
    for k in range(CONV_W):
        wrep_ref[k] = jnp.broadcast_to(w_ref[k:k + 1, :], (SUBLANES, D_CONV))
    sh_ref[0, 0:CONV_HALO, :] = jnp.where(first, 0.0, halo_ref[...])
    sh_ref[0, CONV_HALO:, :] = a_ref[...]
    span = tt + CONV_HALO - SUBLANES
    for s in range(1, SUBLANES):
        sh_ref[s, 0:span, :] = sh_ref[0, s:s + span, :]
    lead = CONV_HALO - (CONV_W - 1)
    groups = CONV_ROWS // SUBLANES

    def conv_rows(c, carry):
        r = pl.multiple_of(c * CONV_ROWS, CONV_ROWS)
        acc = jnp.zeros((groups, SUBLANES, D_CONV), F32) + cb_ref[...]
        for k in range(CONV_W):
            q, s = divmod(k + lead, SUBLANES)
            x = sh_ref[s, pl.ds(r + q * SUBLANES, CONV_ROWS), :]
            acc = acc + x.reshape(groups, SUBLANES, D_CONV) * wrep_ref[k]
        o_ref[pl.ds(r, CONV_ROWS), :] = acc.reshape(CONV_ROWS, D_CONV)
        return carry

    lax.fori_loop(0, tt // CONV_ROWS, conv_rows, 0)


def conv_prompt(a, w, cb, tt):
    b, t, _ = a.shape
    hb = tt // CONV_HALO
    vec = pl.BlockSpec((1, D_CONV), lambda i, j: (0, 0))
    return pl.pallas_call(
        _conv_prompt_kernel,
        grid=(b, t // tt),
        in_specs=[
            pl.BlockSpec((None, tt, D_CONV), lambda i, j: (i, j, 0)),
            pl.BlockSpec((None, CONV_HALO, D_CONV), lambda i, j: (i, jnp.maximum(j * hb - 1, 0), 0)),
            pl.BlockSpec((CONV_W, D_CONV), lambda i, j: (0, 0)),
            vec,
        ],
        out_specs=pl.BlockSpec((None, tt, D_CONV), lambda i, j: (i, j, 0)),
        out_shape=jax.ShapeDtypeStruct((b, t, D_CONV), F32),
        scratch_shapes=[pltpu.VMEM((SUBLANES, CONV_HALO + tt, D_CONV), F32),
                        pltpu.VMEM((CONV_W, SUBLANES, D_CONV), F32)],
        compiler_params=_params(("parallel", "arbitrary"), 40),
        name="conv_prompt",
    )(a, a, w, cb)


def _conv_sample_kernel(apad_ref, w_ref, cb_ref, lg_ref, lb_ref, o_ref):
    rows = o_ref.shape[0]
    acc = jnp.zeros((rows, D_CONV), F32) + cb_ref[...]
    for k in range(CONV_W):
        acc = acc + apad_ref[k:k + rows, :] * w_ref[k:k + 1, :]
    o_ref[...] = _ln_silu(acc, lg_ref, lb_ref).astype(o_ref.dtype)


def conv_sample(apad, w, cb, lg, lb):
    b, tp, _ = apad.shape
    t = tp - (CONV_W - 1)
    vec = pl.BlockSpec((1, D_CONV), lambda i: (0, 0))
    return pl.pallas_call(
        _conv_sample_kernel,
        grid=(b,),
        in_specs=[
            pl.BlockSpec((None, tp, D_CONV), lambda i: (i, 0, 0)),
            pl.BlockSpec((CONV_W, D_CONV), lambda i: (0, 0)),
            vec, vec, vec,
        ],
        out_specs=pl.BlockSpec((None, t, D_CONV), lambda i: (i, 0, 0)),
        out_shape=jax.ShapeDtypeStruct((b, t, D_CONV), BF16),
        compiler_params=_params(("parallel",), 32),
        name="conv_sample",
    )(apad, w, cb, lg, lb)


def _dil_prompt_kernel(*refs):
    qkv_refs = refs[:3 * N_DGROUPS]
    b_ref, qbuf, kbuf, vbuf, o_sc, l_sc, stage = refs[3 * N_DGROUPS:]
    t = b_ref.shape[0]
    n = BAND
    qi = lax.broadcasted_iota(jnp.int32, (n, 2 * n), 0)
    kj = lax.broadcasted_iota(jnp.int32, (n, 2 * n), 1)
    band = (kj > qi) & (kj <= qi + n)

    for gi, (win, dil) in enumerate(DIL_GROUPS):
        assert win // dil == n
        q_ref, k_ref, v_ref = qkv_refs[3 * gi:3 * gi + 3]
        s_len = t // dil
        n_blk = s_len // n
        pitch = s_len + n

        def place(r, q_rows, k_rows, v_rows, s_len=s_len, pitch=pitch):
            k0 = pl.multiple_of(r * pitch, n)
            qbuf[pl.ds(pl.multiple_of(r * s_len, n), s_len), :] = q_rows.astype(BF16)
            kbuf[pl.ds(k0, n), :] = jnp.zeros((n, HEAD_DIM), BF16)
            vbuf[pl.ds(k0, n), :] = jnp.zeros((n, HEAD_DIM), BF16)
            kbuf[pl.ds(k0 + n, s_len), :] = k_rows.astype(BF16)
            vbuf[pl.ds(k0 + n, s_len), :] = v_rows.astype(BF16)

        if dil > GATHER_STRIDE and dil % GATHER_STRIDE == 0:
            outer = dil // GATHER_STRIDE
            mid = t // GATHER_STRIDE

            def gather(r_in, carry, refs=(q_ref, k_ref, v_ref), s_len=s_len, outer=outer, mid=mid, place=place):
                for w, ref in enumerate(refs):
                    stage[w, 0:mid, :] = ref[pl.ds(r_in, mid, stride=GATHER_STRIDE), :]
                for m in range(outer):
                    place(r_in + m * GATHER_STRIDE,
                          *[stage[w, pl.ds(m, s_len, stride=outer), :] for w in range(3)])
                return carry

            lax.fori_loop(0, GATHER_STRIDE, gather, 0)
        else:
            def gather(r, carry, refs=(q_ref, k_ref, v_ref), dil=dil, s_len=s_len, place=place):
                rows = pl.ds(r, s_len, stride=dil) if dil > 1 else pl.ds(0, s_len)
                place(r, *[ref[rows, :] for ref in refs])
                return carry

            lax.fori_loop(0, dil, gather, 0)

        def units(it, carry, dil=dil, n_blk=n_blk, pitch=pitch, gi=gi):
            for j in range(ATTN_UNROLL):
                u = it * ATTN_UNROLL + j
                if n_blk == 1:
                    r, blk = u, 0
                elif dil == 1:
                    r, blk = 0, u
                else:
                    r, blk = lax.div(u, jnp.int32(n_blk)), lax.rem(u, jnp.int32(n_blk))
                q = qbuf[pl.ds(pl.multiple_of(u * n, n), n), :]
                k0 = pl.multiple_of(r * pitch + blk * n, n)
                mask = band & (kj >= jnp.where(blk == 0, n, 0))
                s = jnp.where(mask, _dot_nt(q, kbuf[pl.ds(k0, 2 * n), :]), NEG)
                o, lse = _softmax_pv(s, vbuf[pl.ds(k0, 2 * n), :])
                start = r + blk * (n * dil)
                dst = pl.ds(start, n, stride=dil) if dil > 1 else pl.ds(pl.multiple_of(start, n), n)
                o_sc[gi, dst, :] = o
                l_sc[gi, dst, :] = jnp.broadcast_to(lse, (n, HEAD_DIM))
            return carry

        assert (dil * n_blk) % ATTN_UNROLL == 0
        lax.fori_loop(0, dil * n_blk // ATTN_UNROLL, units, 0)

    def merge(c, carry):
        rows = pl.ds(pl.multiple_of(c * MERGE_ROWS, MERGE_ROWS), MERGE_ROWS)
        outs = [o_sc[gi, rows, :] for gi in range(N_DGROUPS)]
        lses = [l_sc[gi, rows, :] for gi in range(N_DGROUPS)]
        b_ref[rows, :] = _merge3(outs, lses).astype(b_ref.dtype)
        return carry

    lax.fori_loop(0, t // MERGE_ROWS, merge, 0)


def dilated_attn_prompt(qkv, batch, seq):
    assert seq % (BAND * max(d for _, d in DIL_GROUPS)) == 0 and seq % MERGE_ROWS == 0
    kv_rows = max(seq + dil * BAND for _, dil in DIL_GROUPS)

    def spec(which, gi):
        return pl.BlockSpec((seq, HEAD_DIM), lambda b, h: (b, (which * N_DGROUPS + gi) * HEADS + h))

    in_specs = [spec(which, gi) for gi in range(N_DGROUPS) for which in range(3)]
    return pl.pallas_call(
        _dil_prompt_kernel,
        grid=(batch, HEADS),
        in_specs=in_specs,
        out_specs=pl.BlockSpec((seq, HEAD_DIM), lambda b, h: (b, h)),
        out_shape=jax.ShapeDtypeStruct((batch * seq, D_GRP), BF16),
        scratch_shapes=[pltpu.VMEM((seq, HEAD_DIM), BF16), pltpu.VMEM((kv_rows, HEAD_DIM), BF16),
                        pltpu.VMEM((kv_rows, HEAD_DIM), BF16),
                        pltpu.VMEM((N_DGROUPS, seq, HEAD_DIM), F32), pltpu.VMEM((N_DGROUPS, seq, HEAD_DIM), F32),
                        pltpu.VMEM((3, seq // GATHER_STRIDE, HEAD_DIM), F32)],
        compiler_params=_params(("parallel", "parallel"), 40),
        name="dil_attn_prompt",
    )(*([qkv] * (3 * N_DGROUPS)))


def _dil_sample_kernel(q_ref, kn_ref, vn_ref, *rest):
    cache_refs, b_ref = rest[:2 * N_DGROUPS], rest[2 * N_DGROUPS]
    ds = q_ref.shape[0]
    qn = lax.broadcasted_iota(jnp.int32, (ds, ds), 0)
    pn = lax.broadcasted_iota(jnp.int32, (ds, ds), 1)
    units = [(h, gi) for h in range(HEADS) for gi in range(N_DGROUPS)]

    def keys_of(h, gi):
        dil = DIL_GROUPS[gi][1]
        kc_ref = cache_refs[2 * gi]
        if len(kc_ref.shape) == 3:
            n_sub, kept = kc_ref.shape[0], kc_ref.shape[1] // HEADS
            n_keys = n_sub * kept
            head = lambda ref: ref[:, pl.ds(h, kept, stride=HEADS), :].reshape(n_keys, HEAD_DIM).astype(BF16)
            fj = lax.broadcasted_iota(jnp.int32, (ds, n_keys), 1)
            return head, fj // kept * dil + fj % kept, n_sub * dil
        n_keys = kc_ref.shape[0] // HEADS
        head = lambda ref: _head_rows(ref, h, n_keys).astype(BF16)
        return head, lax.broadcasted_iota(jnp.int32, (ds, n_keys), 1), n_keys

    scores = []
    for h, gi in units:
        win, dil = DIL_GROUPS[gi]
        cols = slice(gi * D_GRP + h * HEAD_DIM, gi * D_GRP + (h + 1) * HEAD_DIM)
        head, pj, cache_len = keys_of(h, gi)
        dist = cache_len + lax.broadcasted_iota(jnp.int32, pj.shape, 0) - pj
        mask_c = (dist % dil == 0) & (dist <= dil * (win // dil - 1))
        mask_n = (qn >= pn) & ((qn - pn) % dil == 0)
        q = q_ref[:, cols].astype(BF16)
        scores.append((jnp.where(mask_c, _dot_nt(q, head(cache_refs[2 * gi])), NEG),
                       jnp.where(mask_n, _dot_nt(q, kn_ref[:, cols].astype(BF16)), NEG)))
    probs = []
    for s_c, s_n in scores:
        mx = jnp.maximum(jnp.max(s_c, axis=-1, keepdims=True), jnp.max(s_n, axis=-1, keepdims=True))
        p_c, p_n = jnp.exp(s_c - mx), jnp.exp(s_n - mx)
        l = jnp.sum(p_c, axis=-1, keepdims=True) + jnp.sum(p_n, axis=-1, keepdims=True)
        probs.append((p_c.astype(BF16), p_n.astype(BF16), l, mx + jnp.log(l)))
    outs = {}
    for (h, gi), (p_c, p_n, l, lse) in zip(units, probs):
        cols = slice(gi * D_GRP + h * HEAD_DIM, gi * D_GRP + (h + 1) * HEAD_DIM)
        head, _, _ = keys_of(h, gi)
        o = _dot(p_c, head(cache_refs[2 * gi + 1])) + _dot(p_n, vn_ref[:, cols].astype(BF16))
        outs[h, gi] = (o / l, lse)
    for h in range(HEADS):
        per_group = [outs[h, gi] for gi in range(N_DGROUPS)]
        merged = _merge3([o for o, _ in per_group], [lse for _, lse in per_group])
        b_ref[:, h * HEAD_DIM:(h + 1) * HEAD_DIM] = merged.astype(b_ref.dtype)


def dilated_attn_sample(qkv, caches, layer):
    b, ds, _ = qkv.shape
    caches = list(caches)
    cache_specs = []
    for gi, (win, dil) in enumerate(DIL_GROUPS):
        layers, _, rows, _ = caches[2 * gi].shape
        cache_len = rows // HEADS
        assert cache_len - dil * (win // dil - 1) >= 0
        kept = dil // 2
        if cache_len % dil == 0 and ds <= kept and kept * HEADS % SUBLANES == 0:
            for c in (2 * gi, 2 * gi + 1):
                caches[c] = caches[c].reshape(layers, b, cache_len // dil, dil * HEADS, HEAD_DIM)
            cache_specs += [pl.BlockSpec((None, None, cache_len // dil, kept * HEADS, HEAD_DIM),
                                         lambda i: (layer, i, 0, 0, 0))] * 2
        else:
            cache_specs += [pl.BlockSpec((None, None, rows, HEAD_DIM), lambda i: (layer, i, 0, 0))] * 2

    new = lambda which: pl.BlockSpec((None, ds, D_ATT), lambda i: (i, 0, which))
    return pl.pallas_call(
        _dil_sample_kernel,
        grid=(b,),
        in_specs=[new(0), new(1), new(2)] + cache_specs,
        out_specs=pl.BlockSpec((None, ds, D_GRP), lambda i: (i, 0, 0)),
        out_shape=jax.ShapeDtypeStruct((b, ds, D_GRP), BF16),
        compiler_params=_params(("parallel",), 48),
        name="dil_attn_sample",
    )(qkv, qkv, qkv, *caches)


def _mix_out_kernel(x_ref, a_ref, b_ref, wa_ref, wb_ref, y_ref):
    y_ref[...] = x_ref[...] + _dot(a_ref[...], wa_ref[...]) + _dot(b_ref[...], wb_ref[...])


def mix_out(x, a, b, w_out, layer, tm):
    m = x.shape[0]
    row = lambda width: pl.BlockSpec((tm, width), lambda i: (i, 0))
    return pl.pallas_call(
        _mix_out_kernel,
        grid=(m // tm,),
        in_specs=[row(D_MODEL), row(D_CONV), row(D_GRP),
                  pl.BlockSpec((None, D_CONV, D_MODEL), lambda i: (layer, 0, 0)),
                  pl.BlockSpec((None, D_GRP, D_MODEL), lambda i: (layer, D_CONV // D_GRP, 0))],
        out_specs=row(D_MODEL),
        out_shape=jax.ShapeDtypeStruct((m, D_MODEL), F32),
        compiler_params=_params(("parallel",), 48),
        name="mix_out",
    )(x, a, b, w_out, w_out)


def _mix_mem_kernel(x_ref, c_ref, lg_ref, lb_ref, b_ref, wa_ref, wb_ref,
                    g_ref, wq_ref, qg_ref, k_ref, v_ref, wo_ref, y_ref):
    x1 = x_ref[...] + _dot(b_ref[...], wb_ref[...])
    x1 = x1 + _dot(_ln_silu(c_ref[...], lg_ref, lb_ref).astype(BF16), wa_ref[...])
    y_ref[...] = _mem_attn_tail(x1, 1, g_ref, wq_ref, qg_ref, _kv_head_of(k_ref, v_ref), wo_ref)


def mix_out_mem(x, c, lg, lb, b, w_out, layer, mem_args, batch, tm):
    m = x.shape[0]
    nt = m // batch // tm
    row = lambda width: pl.BlockSpec((tm, width), lambda i, j: (i * nt + j, 0))
    vec = _resident((1, D_CONV), (0, 0))
    mem_specs, mem_ops = _mem_operands(*mem_args)
    return pl.pallas_call(
        _mix_mem_kernel,
        grid=(batch, nt),
        in_specs=[row(D_MODEL), row(D_CONV), vec, vec, row(D_GRP),
                  _resident((None, D_CONV, D_MODEL), (layer, 0, 0)),
                  _resident((None, D_GRP, D_MODEL), (layer, D_CONV // D_GRP, 0))] + mem_specs,
        out_specs=row(D_MODEL),
        out_shape=jax.ShapeDtypeStruct((m, D_MODEL), F32),
        compiler_params=_params(("parallel", "parallel"), 56),
        name="mix_out_mem",
    )(x, c, lg, lb, b, w_out, w_out, *mem_ops)


def _sgu_gated(u_ref, gv_ref, lg_ref, lb_ref, ws_ref, bs_ref):
    gv = gv_ref[...]
    mu = jnp.mean(gv, axis=-1, keepdims=True)
    vc = gv - mu
    v = vc * lax.rsqrt(jnp.mean(vc * vc, axis=-1, keepdims=True) + EPS) * lg_ref[...] + lb_ref[...]
    vb = v.astype(BF16)
    rows, cm = gv.shape[0], ws_ref.shape[1]
    gated = []
    for g in range(N_SG):
        cols = slice(g * D_SG, (g + 1) * D_SG)
        bias = bs_ref[:, g:g + 1]
        if cm == SUBLANES:
            v3 = vb[:, cols].astype(F32).reshape(rows // cm, cm, D_SG)
            wsg = ws_ref[g].astype(F32)
            sv = bias[None] + sum(v3[:, s:s + 1, :] * wsg[:, s:s + 1][None] for s in range(cm))
            sv = sv.reshape(rows, D_SG)
        else:
            sv = jnp.concatenate([_dot(ws_ref[g], vb[c * cm:(c + 1) * cm, cols]) + bias
                                  for c in range(rows // cm)], axis=0)
        gated.append((u_ref[:, cols] * sv).astype(BF16))
    return jnp.concatenate(gated, axis=-1), v


def _sgu_kernel(x_ref, u_ref, gv_ref, lg_ref, lb_ref, ws_ref, bs_ref, w_ref, *out_refs, emit_v):
    gated, v = _sgu_gated(u_ref, gv_ref, lg_ref, lb_ref, ws_ref, bs_ref)
    if emit_v:
        out_refs[1][...] = v
    out_refs[0][...] = x_ref[...] + _dot(gated, w_ref[...])


def _sgu_mem_kernel(x_ref, u_ref, gv_ref, lg_ref, lb_ref, ws_ref, bs_ref, w_ref,
                    g_ref, wq_ref, qg_ref, k_ref, v_ref, wo_ref, y_ref):
    gated, _ = _sgu_gated(u_ref, gv_ref, lg_ref, lb_ref, ws_ref, bs_ref)
    x1 = x_ref[...] + _dot(gated, w_ref[...])
    y_ref[...] = _mem_attn_tail(x1, 1, g_ref, wq_ref, qg_ref, _kv_head_of(k_ref, v_ref), wo_ref)


def sgu_out_mem(x, z, lg, lb, ws, bs, w_out, layer, mem_args, batch, tm):
    m = x.shape[0]
    cm = ws.shape[1]
    nt = m // batch // tm
    assert tm % cm == 0
    row = lambda jblk: pl.BlockSpec((tm, D_GATE), lambda i, j: (i * nt + j, jblk))
    vec = pl.BlockSpec((1, D_GATE), lambda i, j: (0, 0))
    mem_specs, mem_ops = _mem_operands(*mem_args)
    return pl.pallas_call(
        _sgu_mem_kernel,
        grid=(batch, nt),
        in_specs=[row(0), row(0), row(1), vec, vec,
                  _resident((N_SG, cm, cm), (0, 0, 0)), _resident((cm, N_SG), (0, 0)),
                  _resident((None, D_GATE, D_MODEL), (layer, 0, 0))] + mem_specs,
        out_specs=row(0),
        out_shape=jax.ShapeDtypeStruct((m, D_MODEL), F32),
        compiler_params=_params(("parallel", "parallel"), 56),
        name="sgu_out_mem",
    )(x, z, z, lg, lb, ws, bs, w_out, *mem_ops)


def sgu_out(x, z, lg, lb, ws, bs, w_out, layer, tm, emit_v):
    m = x.shape[0]
    cm = ws.shape[1]
    assert tm % cm == 0
    row = lambda jblk: pl.BlockSpec((tm, D_GATE), lambda i: (i, jblk))
    vec = pl.BlockSpec((1, D_GATE), lambda i: (0, 0))
    out_specs = [row(0)]
    out_shape = [jax.ShapeDtypeStruct((m, D_MODEL), F32)]
    if emit_v:
        out_specs.append(row(0))
        out_shape.append(jax.ShapeDtypeStruct((m, D_GATE), F32))
    res = pl.pallas_call(
        functools.partial(_sgu_kernel, emit_v=emit_v),
        grid=(m // tm,),
        in_specs=[row(0), row(0), row(1), vec, vec,
                  pl.BlockSpec((N_SG, cm, cm), lambda i: (0, 0, 0)),
                  pl.BlockSpec((cm, N_SG), lambda i: (0, 0)),
                  pl.BlockSpec((None, D_GATE, D_MODEL), lambda i: (layer, 0, 0))],
        out_specs=out_specs,
        out_shape=out_shape,
        compiler_params=_params(("parallel",), 52),
        name="sgu_out",
    )(x, z, z, lg, lb, ws, bs, w_out)
    return res if emit_v else (res[0], None)


def _mem_attn_tail(x, n_seq, g_ref, wq_ref, qg_ref, kv_head, wo_ref):
    tm = x.shape[0] // n_seq
    h = (_rms(x) * g_ref[...]).astype(BF16)
    q = _head_norm(_dot(h, wq_ref[...]), qg_ref[...]).astype(BF16)
    units = [(b, hd) for b in range(n_seq) for hd in range(HEADS)]
    scores = [_dot_nt(q[b * tm:(b + 1) * tm, hd * HEAD_DIM:(hd + 1) * HEAD_DIM], kv_head(b, hd)[0]) for b, hd in units]
    outs = [_softmax_pv(s, kv_head(b, hd)[1])[0].astype(BF16) for s, (b, hd) in zip(scores, units)]
    per_seq = [jnp.concatenate(outs[b * HEADS:(b + 1) * HEADS], axis=-1) for b in range(n_seq)]
    o = jnp.concatenate(per_seq, axis=0) if n_seq > 1 else per_seq[0]
    return x + _dot(o, wo_ref[...])


def _kv_head_of(k_ref, v_ref):
    n_mem = k_ref.shape[0] // HEADS
    return lambda b, hd: (_head_rows(k_ref, hd, n_mem).astype(BF16), _head_rows(v_ref, hd, n_mem).astype(BF16))


def _resident(shape, index):
    return pl.BlockSpec(shape, lambda *_: index, pipeline_mode=pl.Buffered(1))


def _mem_operands(g, wq, q_gain, k, v, kv_layer, wo, layer):
    specs = [_resident((1, D_MODEL), (0, 0)),
             _resident((None, D_MODEL, D_MEMATT), (layer, 0, 0)),
             _resident((1, D_MEMATT), (0, 0)),
             pl.BlockSpec((None, None, k.shape[2], HEAD_DIM), lambda i, j: (kv_layer, i, 0, 0)),
             pl.BlockSpec((None, None, k.shape[2], HEAD_DIM), lambda i, j: (kv_layer, i, 0, 0)),
             _resident((None, D_MEMATT, D_MODEL), (layer, 0, 0))]
    return specs, (g, wq, q_gain, k, v, wo)


def _mem_attn_kernel(x_ref, g_ref, wq_ref, qg_ref, k_ref, v_ref, wo_ref, y_ref):
    bb, tm, _ = x_ref.shape
    n_mem = k_ref.shape[1] // HEADS

    def kv_head(b, hd):
        rows = pl.ds(hd, n_mem, stride=HEADS)
        return k_ref[b, rows, :].astype(BF16), v_ref[b, rows, :].astype(BF16)

    x = x_ref[...].reshape(bb * tm, D_MODEL)
    y_ref[...] = _mem_attn_tail(x, bb, g_ref, wq_ref, qg_ref, kv_head, wo_ref).reshape(bb, tm, D_MODEL)


def mem_attn(x, g, wq, q_gain, k, v, kv_layer, wo, layer, bb, tm):
    b, t, _ = x.shape
    full = lambda shape: pl.BlockSpec(shape, lambda i, j: (0,) * len(shape))
    kv = pl.BlockSpec((None, bb, k.shape[2], HEAD_DIM), lambda i, j: (kv_layer, i, 0, 0))
    xs = pl.BlockSpec((bb, tm, D_MODEL), lambda i, j: (i, j, 0))
    return pl.pallas_call(
        _mem_attn_kernel,
        grid=(b // bb, t // tm),
        in_specs=[xs, full((1, D_MODEL)),
                  pl.BlockSpec((None, D_MODEL, D_MEMATT), lambda i, j: (layer, 0, 0)),
                  full((1, D_MEMATT)), kv, kv,
                  pl.BlockSpec((None, D_MEMATT, D_MODEL), lambda i, j: (layer, 0, 0))],
        out_specs=xs,
        out_shape=jax.ShapeDtypeStruct((b, t, D_MODEL), F32),
        compiler_params=_params(("parallel", "parallel"), 40),
        name="mem_attn",
    )(x, g, wq, q_gain, k, v, wo)


def _tile_heads(g):
    return jnp.tile(g, HEADS)


def _row_tile(m):
    return min(m, ROW_TILE)


def kernel(x_prompt, x_sample, mem_prompt, state_conv, cache_k_w128, cache_v_w128, cache_k_w512, cache_v_w512,
           cache_k_w2048, cache_v_w2048, cache_mem_k, cache_mem_v, g_mix, w_in_e, conv_w, conv_b, conv_ln_g,
           conv_ln_b, q_norm_e, k_norm_e, w_out_e, w_in_o, b_in_o, v_ln_g, v_ln_b, w_s, b_s, w_out_o, g_xmem,
           g_mem, wq_mem, wk_mem, wv_mem, q_norm_mem, k_norm_mem, wo_mem, g_ffn, w_ffn1, w_ffn2):
    depth = g_mix.shape[0]
    bp, tp, _ = x_prompt.shape
    bs, ts, _ = x_sample.shape
    mp, ms = bp * tp, bs * ts
    scale = HEAD_DIM ** -0.5
    row = lambda v: v.reshape(1, -1)
    th_rows = lambda c: c.reshape(c.shape[0], c.shape[1], c.shape[2] * HEADS, HEAD_DIM)
    caches = [th_rows(c) for c in (cache_k_w128, cache_v_w128, cache_k_w512, cache_v_w512,
                                   cache_k_w2048, cache_v_w2048)]
    mem_k_s, mem_v_s = th_rows(cache_mem_k), th_rows(cache_mem_v)

    w_in_e, w_out_e, w_in_o, w_out_o = (w.astype(BF16) for w in (w_in_e, w_out_e, w_in_o, w_out_o))
    qkv_group0 = 2 * D_CONV // D_GRP
    wq_mem, wo_mem = wq_mem.astype(BF16), wo_mem.astype(BF16)
    wkv_mem = jnp.concatenate([wk_mem, wv_mem], axis=2).astype(BF16)

    xp = x_prompt.reshape(mp, D_MODEL)
    xs = x_sample.reshape(ms, D_MODEL)
    mem = mem_prompt.reshape(bp * N_MEM, D_MODEL)
    tm_p, tm_s, tm_mem = _row_tile(mp), _row_tile(ms), _row_tile(bp * N_MEM)
    assert tp % tm_p == 0 or tm_p % tp == 0
    assert ms == tm_s

    conv_pl, conv_sl, kv_pl, kv_sl, memk_pl, memv_pl, chunk_sl = [], [], [], [], [], [], []
    for i in range(depth):
        j = i // 2
        kgain = jnp.stack([_tile_heads(k_norm_mem[i]), jnp.ones((D_MEMATT,), F32)])
        mk, mv = headnorm_proj(mem, row(g_mem[i]), wkv_mem, i, 0, kgain, 1, tm_mem, D_MEMATT,
                               [(0, tm_mem, 1), (1, tm_mem, 1)], False)
        memk_pl.append(mk.reshape(bp, N_MEM, HEADS, HEAD_DIM))
        memv_pl.append(mv.reshape(bp, N_MEM, HEADS, HEAD_DIM))
        qgain = row(_tile_heads(q_norm_mem[i]) * scale)
        mem_p = (row(g_xmem[i]), wq_mem, qgain, mk.reshape(1, bp, N_MEM * HEADS, HEAD_DIM),
                 mv.reshape(1, bp, N_MEM * HEADS, HEAD_DIM), 0, wo_mem, i)
        mem_s = (row(g_xmem[i]), wq_mem, qgain, mem_k_s, mem_v_s, i, wo_mem, i)

        if i % 2 == 0:
            gains = jnp.concatenate([
                jnp.stack([_tile_heads(q_norm_e[j, gi]) * scale for gi in range(N_DGROUPS)]),
                jnp.stack([_tile_heads(k_norm_e[j, gi]) for gi in range(N_DGROUPS)]),
                jnp.ones((N_DGROUPS, D_GRP), F32)])
            conv_args = (conv_w[j], row(conv_b[j]), row(conv_ln_g[j]), row(conv_ln_b[j]))

            def kv_tiles(seq, tm):
                tiles = []
                for gi, (win, _) in enumerate(DIL_GROUPS):
                    keep = min(win, seq)
                    assert keep == seq or (keep <= tm and seq % tm == 0)
                    spec = (tm, 1) if keep == seq else (keep, seq // tm)
                    tiles += [((1 + which) * N_DGROUPS + gi,) + spec for which in range(2)]
                return tiles

            a = glu_proj(xp, row(g_mix[i]), w_in_e, j, tm_p)
            qkv, *new_kv = headnorm_proj(xp, row(g_mix[i]), w_in_e, j, qkv_group0, gains, 2 * N_DGROUPS, tm_p, D_ATT,
                                         kv_tiles(tp, tm_p), True)
            kv_pl.append([kv.reshape(bp, -1, HEADS, HEAD_DIM) for kv in new_kv])
            a3 = a.reshape(bp, tp, D_CONV)
            conv_pl.append(a3[:, tp - (CONV_W - 1):])
            c_out = conv_prompt(a3, *conv_args[:2], tt=FUSED_ROW_TILE).reshape(mp, D_CONV)
            b_out = dilated_attn_prompt(qkv, bp, tp)
            xp = mix_out_mem(xp, c_out, *conv_args[2:], b_out, w_out_e, j, mem_p, bp, FUSED_ROW_TILE)

            a = glu_proj(xs, row(g_mix[i]), w_in_e, j, tm_s)
            qkv, *new_kv = headnorm_proj(xs, row(g_mix[i]), w_in_e, j, qkv_group0, gains, 2 * N_DGROUPS, tm_s, D_ATT,
                                         kv_tiles(ts, tm_s), True)
            kv_sl.append([kv.reshape(bs, -1, HEADS, HEAD_DIM) for kv in new_kv])
            apad = jnp.concatenate([state_conv[j], a.reshape(bs, ts, D_CONV)], axis=1)
            conv_sl.append(apad[:, apad.shape[1] - (CONV_W - 1):])
            a_out = conv_sample(apad, *conv_args).reshape(ms, D_CONV)
            b_out = dilated_attn_sample(qkv.reshape(bs, ts, 3 * D_ATT), caches, j).reshape(ms, D_GRP)
            xs = mix_out(xs, a_out, b_out, w_out_e, j, tm_s)
        else:
            sgu_vecs = (row(v_ln_g[j]), row(v_ln_b[j]))
            tril = jnp.tril(jnp.ones((CHUNK, CHUNK), F32))

            def spatial(t):
                c = min(CHUNK, t)
                assert c in (SUBLANES, CHUNK) and t % c == 0
                return (w_s[j][:, :c, :c] * tril[:c, :c]).astype(BF16), b_s[j][:, :c].T

            z = gelu_proj(xp, row(g_mix[i]), w_in_o, j, row(b_in_o[j]), tm_p)
            xp = sgu_out_mem(xp, z, *sgu_vecs, *spatial(tp), w_out_o, j, mem_p, bp, FUSED_ROW_TILE)

            z = gelu_proj(xs, row(g_mix[i]), w_in_o, j, row(b_in_o[j]), tm_s)
            xs, v = sgu_out(xs, z, *sgu_vecs, *spatial(ts), w_out_o, j, SGU_SAMPLE_ROW_TILE, emit_v=True)
            chunk_sl.append(v.reshape(bs, ts, D_GATE))

        xs = mem_attn(xs.reshape(bs, ts, D_MODEL), *mem_s, MEM_SEQS_PER_STEP, ts).reshape(ms, D_MODEL)
        xs, w1, w2 = ffn_cast(xs, row(g_ffn[i]), w_ffn1, w_ffn2, i, FFN_CAST_HIDDEN_TILE)
        xp = ffn(xp, row(g_ffn[i]), w1, w2, 0, tm_p, FFN_HIDDEN_TILE)

    stack = lambda items: jnp.stack(items)
    kv_p = [stack([kv[n] for kv in kv_pl]) for n in range(2 * N_DGROUPS)]
    kv_s = [stack([kv[n] for kv in kv_sl]) for n in range(2 * N_DGROUPS)]
    return (xp.reshape(bp, tp, D_MODEL), xs.reshape(bs, ts, D_MODEL), stack(conv_pl), stack(conv_sl),
            *kv_p, *kv_s, stack(memk_pl), stack(memv_pl), stack(chunk_sl))
```
